```python
import jax, jax.numpy as jnp
from jax import lax
import numpy as np

D_MODEL = 2048
BATCH = 2
SEQ = 4096
DEPTH = 2
DEC_BATCH = 128
DEC_SEQ = 8
PAST_LEN = 8192
PAGE_SIZE = 128

HEAD_DIM = 64
N_HEADS = 16
N_KV_HEADS = 4
GROUP = N_HEADS // N_KV_HEADS
D_ATTN = N_HEADS * HEAD_DIM
D_KV = N_KV_HEADS * HEAD_DIM
WINDOW = 128
BLOCK = WINDOW
D_CONV = D_MODEL // 2
CONV_K = 31
D_FF = 3 * D_MODEL
FFN_K = 3
EPS = 1e-6
SPLITS = (D_CONV, 2 * D_CONV, 2 * D_CONV + D_ATTN, 2 * D_CONV + D_ATTN + D_KV,
          2 * D_CONV + D_ATTN + 2 * D_KV, 2 * D_CONV + D_ATTN + 2 * D_KV + D_MODEL)
IN_COLS = SPLITS[-1] + D_MODEL
NEG = -1e30

kernel_name = "hybrid_conformer_conv_swa_sink_gated_decoder_step"


def _rmsnorm(x, g):
    xf = x.astype(jnp.float32)
    xf = xf * lax.rsqrt(jnp.mean(xf * xf, axis=-1, keepdims=True) + EPS)
    return (xf * g.astype(jnp.float32)).astype(x.dtype)


def _layernorm(x, g, b):
    xf = x.astype(jnp.float32)
    mu = jnp.mean(xf, axis=-1, keepdims=True)
    xc = xf - mu
    xf = xc * lax.rsqrt(jnp.mean(xc * xc, axis=-1, keepdims=True) + EPS)
    return (xf * g.astype(jnp.float32) + b.astype(jnp.float32)).astype(x.dtype)


def _causal_dwconv(u, buf, w, b):
    xx = jnp.concatenate([buf.astype(u.dtype), u], axis=1)
    out = lax.conv_general_dilated(
        xx, w[:, None, :].astype(u.dtype), window_strides=(1,), padding='VALID',
        dimension_numbers=('NWC', 'WIO', 'NWC'), feature_group_count=u.shape[-1])
    return out + b.astype(u.dtype), xx[:, -(w.shape[0] - 1):]


def _alibi_slopes():
    return jnp.exp2(-8.0 * jnp.arange(1, N_HEADS + 1, dtype=jnp.float32) / N_HEADS)


def _sink_attend(q, k, v, q_pos, k_pos, sinks):
    lead = q.shape[:-3]
    tq = q.shape[-3]
    qg = q.reshape(*lead, tq, N_KV_HEADS, GROUP, HEAD_DIM)
    s = jnp.einsum('...qkgd,...skd->...kgqs', qg, k).astype(jnp.float32) * (HEAD_DIM ** -0.5)
    dist = q_pos[..., :, None] - k_pos[..., None, :]
    valid = (dist >= 0) & (dist <= WINDOW) & (k_pos[..., None, :] >= 0)
    slopes = _alibi_slopes().reshape(N_KV_HEADS, GROUP, 1, 1)
    s = s - slopes * dist[..., None, None, :, :].astype(jnp.float32)
    s = jnp.where(valid[..., None, None, :, :], s, NEG)
    sink = sinks.astype(jnp.float32).reshape(N_KV_HEADS, GROUP, 1, 1)
    m = jnp.maximum(jnp.max(s, axis=-1, keepdims=True), sink)
    p = jnp.exp(s - m)
    denom = jnp.sum(p, axis=-1, keepdims=True) + jnp.exp(sink - m)
    o = jnp.einsum('...kgqs,...skd->...qkgd', (p / denom).astype(v.dtype), v)
    return o.reshape(*lead, tq, D_ATTN)


def _banded_attention(q, k, v, sinks):
    n, t = q.shape[:2]
    nb = t // BLOCK
    qb = q.reshape(n, nb, BLOCK, N_HEADS, HEAD_DIM)
    kb = k.reshape(n, nb, BLOCK, N_KV_HEADS, HEAD_DIM)
    vb = v.reshape(n, nb, BLOCK, N_KV_HEADS, HEAD_DIM)
    pad = jnp.zeros_like(kb[:, :1])
    kk = jnp.concatenate([jnp.concatenate([pad, kb[:, :-1]], axis=1), kb], axis=2)
    vv = jnp.concatenate([jnp.concatenate([pad, vb[:, :-1]], axis=1), vb], axis=2)
    starts = jnp.arange(nb, dtype=jnp.int32)[:, None] * BLOCK
    q_pos = starts + jnp.arange(BLOCK, dtype=jnp.int32)[None, :]
    k_pos = starts - BLOCK + jnp.arange(2 * BLOCK, dtype=jnp.int32)[None, :]
    o = _sink_attend(qb, kk, vv, q_pos, k_pos, sinks)
    return o.reshape(n, t, D_ATTN)


def _layer(x, conv_buf, k_buf, v_buf, ffn_buf, p, prompt):
    (norm1_g, w_in, conv_w, conv_b, conv_ln_g, conv_ln_b, w_conv_out, attn_sinks,
     w_attn_out, w_out, norm2_g, w_up, ffn_conv_w, ffn_conv_b, w_down) = p
    n, t = x.shape[:2]
    h = _rmsnorm(x, norm1_g)
    proj = h @ w_in
    glu_a, glu_g, q, k, v, gate_c, gate_a = jnp.split(proj, SPLITS, axis=-1)
    u = glu_a * jax.nn.sigmoid(glu_g)
    c, new_conv = _causal_dwconv(u, conv_buf, conv_w, conv_b)
    c = jax.nn.silu(_layernorm(c, conv_ln_g, conv_ln_b))
    branch_c = c @ w_conv_out
    q = q.reshape(n, t, N_HEADS, HEAD_DIM)
    k = k.reshape(n, t, N_KV_HEADS, HEAD_DIM)
    v = v.reshape(n, t, N_KV_HEADS, HEAD_DIM)
    if prompt:
        a = _banded_attention(q, k, v, attn_sinks)
        new_k, new_v = k[:, -WINDOW:], v[:, -WINDOW:]
    else:
        kk = jnp.concatenate([k_buf.astype(k.dtype), k], axis=1)
        vv = jnp.concatenate([v_buf.astype(v.dtype), v], axis=1)
        q_pos = PAST_LEN + jnp.arange(t, dtype=jnp.int32)
        k_pos = PAST_LEN - WINDOW + jnp.arange(WINDOW + t, dtype=jnp.int32)
        a = _sink_attend(q, kk, vv, q_pos, k_pos, attn_sinks)
        new_k, new_v = kk[:, -WINDOW:], vv[:, -WINDOW:]
    branch_a = a @ w_attn_out
    merged = jax.nn.sigmoid(gate_c) * branch_c + jax.nn.sigmoid(gate_a) * branch_a
    x = x + merged @ w_out
    h2 = _rmsnorm(x, norm2_g)
    up, new_ffn = _causal_dwconv(h2 @ w_up, ffn_buf, ffn_conv_w, ffn_conv_b)
    g, val = jnp.split(up, 2, axis=-1)
    x = x + (jax.nn.silu(g) * val) @ w_down
    return x, new_conv, new_k, new_v, new_ffn


def setup_inputs(seed: int = 0) -> dict:
    key = jax.random.key(seed)
    ks = jax.random.split(key, 24)
    f = jnp.float32
    nrm = lambda k, shape, s: jax.random.normal(k, shape, f) * s
    ffn_w = jnp.zeros((DEPTH, FFN_K, 2 * D_FF), f).at[:, -1].set(1.0) + nrm(ks[18], (DEPTH, FFN_K, 2 * D_FF), 0.1)
    return {
        "x_prompt": nrm(ks[0], (BATCH, SEQ, D_MODEL), 1.0),
        "x_sample": nrm(ks[1], (DEC_BATCH, DEC_SEQ, D_MODEL), 1.0),
        "cache_k": nrm(ks[2], (DEPTH, DEC_BATCH, WINDOW, N_KV_HEADS, HEAD_DIM), 1.0),
        "cache_v": nrm(ks[3], (DEPTH, DEC_BATCH, WINDOW, N_KV_HEADS, HEAD_DIM), 1.0),
        "state_conv": nrm(ks[4], (DEPTH, DEC_BATCH, CONV_K - 1, D_CONV), 0.5),
        "state_ffn_conv": nrm(ks[5], (DEPTH, DEC_BATCH, FFN_K - 1, 2 * D_FF), 1.0),
        "norm1_g": 1.0 + nrm(ks[6], (DEPTH, D_MODEL), 0.02),
        "w_in": nrm(ks[7], (DEPTH, D_MODEL, IN_COLS), D_MODEL ** -0.5),
        "conv_w": nrm(ks[8], (DEPTH, CONV_K, D_CONV), CONV_K ** -0.5),
        "conv_b": nrm(ks[9], (DEPTH, D_CONV), 0.02),
        "conv_ln_g": 1.0 + nrm(ks[10], (DEPTH, D_CONV), 0.02),
        "conv_ln_b": nrm(ks[11], (DEPTH, D_CONV), 0.02),
        "w_conv_out": nrm(ks[12], (DEPTH, D_CONV, D_MODEL), D_CONV ** -0.5),
        "attn_sinks": nrm(ks[13], (DEPTH, N_HEADS), 1.0),
        "w_attn_out": nrm(ks[14], (DEPTH, D_ATTN, D_MODEL), D_ATTN ** -0.5),
        "w_out": nrm(ks[15], (DEPTH, D_MODEL, D_MODEL), D_MODEL ** -0.5),
        "norm2_g": 1.0 + nrm(ks[16], (DEPTH, D_MODEL), 0.02),
        "w_up": nrm(ks[17], (DEPTH, D_MODEL, 2 * D_FF), D_MODEL ** -0.5),
        "ffn_conv_w": ffn_w,
        "ffn_conv_b": nrm(ks[19], (DEPTH, 2 * D_FF), 0.02),
        "w_down": nrm(ks[20], (DEPTH, D_FF, D_MODEL), D_FF ** -0.5),
        "final_norm_g": 1.0 + nrm(ks[21], (D_MODEL,), 0.02),
    }


def reference(x_prompt, x_sample, cache_k, cache_v, state_conv, state_ffn_conv,
              norm1_g, w_in, conv_w, conv_b, conv_ln_g, conv_ln_b, w_conv_out,
              attn_sinks, w_attn_out, w_out, norm2_g, w_up, ffn_conv_w, ffn_conv_b,
              w_down, final_norm_g):
    xp, xs = x_prompt, x_sample
    kp_l, vp_l, cp_l, fp_l = [], [], [], []
    ks_l, vs_l, cs_l, fs_l = [], [], [], []
    for l in range(DEPTH):
        p = (norm1_g[l], w_in[l], conv_w[l], conv_b[l], conv_ln_g[l], conv_ln_b[l],
             w_conv_out[l], attn_sinks[l], w_attn_out[l], w_out[l], norm2_g[l],
             w_up[l], ffn_conv_w[l], ffn_conv_b[l], w_down[l])
        zero_conv = jnp.zeros((BATCH, CONV_K - 1, D_CONV), xp.dtype)
        zero_ffn = jnp.zeros((BATCH, FFN_K - 1, 2 * D_FF), xp.dtype)
        xp, c_p, k_p, v_p, f_p = _layer(xp, zero_conv, None, None, zero_ffn, p, True)
        xs, c_s, k_s, v_s, f_s = _layer(xs, state_conv[l], cache_k[l], cache_v[l],
                                        state_ffn_conv[l], p, False)
        kp_l.append(k_p); vp_l.append(v_p); cp_l.append(c_p); fp_l.append(f_p)
        ks_l.append(k_s); vs_l.append(v_s); cs_l.append(c_s); fs_l.append(f_s)
    y_prompt = _rmsnorm(xp, final_norm_g)
    y_sample = _rmsnorm(xs, final_norm_g)
    return (y_prompt, y_sample,
            jnp.stack(kp_l), jnp.stack(vp_l), jnp.stack(cp_l), jnp.stack(fp_l),
            jnp.stack(ks_l), jnp.stack(vs_l), jnp.stack(cs_l), jnp.stack(fs_l))
```

```python
import functools

import jax
import jax.numpy as jnp
from jax import lax
from jax.experimental import pallas as pl
from jax.experimental.pallas import tpu as pltpu

F32 = jnp.float32
BF16 = jnp.bfloat16

D_MODEL = 2048
HEAD_DIM = 64
N_HEADS = 16
N_KV_HEADS = 4
GROUP = N_HEADS // N_KV_HEADS
D_ATTN = N_HEADS * HEAD_DIM
D_KV = N_KV_HEADS * HEAD_DIM
WINDOW = 128
D_CONV = D_MODEL // 2
CONV_K = 31
D_FF = 3 * D_MODEL
FFN_K = 3
EPS = 1e-6
IN_COLS = 2 * D_CONV + D_ATTN + 2 * D_KV + 2 * D_MODEL
NEG = -1e30
SCALE = HEAD_DIM ** -0.5
SLOPES = tuple(2.0 ** (-8.0 * (h + 1) / N_HEADS) for h in range(N_HEADS))

V7X_SUBLANES = 8
V7X_SCOPED_VMEM_CAP_BYTES = 60000 * 1024

IN_TILE = 512
COL_GLU_A = 0
COL_GLU_G = D_CONV
COL_GATE_C = 2 * D_CONV
COL_GATE_A = COL_GATE_C + D_MODEL
COL_Q = COL_GATE_A + D_MODEL
COL_K = COL_Q + D_ATTN
COL_V = COL_K + D_KV

CONV_HALO = 32
CONV_ROW_CHUNK = 64
CONV_LANE_CHUNK = 256
FFN_HALO = V7X_SUBLANES


def _nbytes(shape, dtype):
    n = 1
    for s in shape:
        n *= s
    return n * jnp.dtype(dtype).itemsize


def _params(semantics, pipelined, resident=(), temps=()):
    est = 2 * sum(_nbytes(s, d) for s, d in pipelined)
    est += sum(_nbytes(s, d) for s, d in resident)
    est += sum(_nbytes(s, d) for s, d in temps)
    limit = min(V7X_SCOPED_VMEM_CAP_BYTES, est + est // 4)
    return pltpu.CompilerParams(dimension_semantics=semantics, vmem_limit_bytes=limit)


def _rms(x, g):
    return x * lax.rsqrt(jnp.mean(x * x, axis=-1, keepdims=True) + EPS) * g


def _dot(a, b):
    return jnp.dot(a, b, preferred_element_type=F32)


def _inproj_body(x_ref, g_ref, w_ref, o_ref, h_ref):
    @pl.when(pl.program_id(1) == 0)
    def _():
        h_ref[...] = _rms(x_ref[...], g_ref[...]).astype(BF16)

    o_ref[...] = _dot(h_ref[...], w_ref[...])


def _w_in_tile(j):
    n_glu = 2 * D_CONV // IN_TILE
    n_qkv = (D_ATTN + 2 * D_KV) // IN_TILE
    n_gate = 2 * D_MODEL // IN_TILE
    return jnp.where(j < n_glu, j, jnp.where(j < n_glu + n_gate, j + n_qkv, j - n_gate))


def _inproj(x, g, w, tm):
    m = x.shape[0]
    blocks = [((tm, D_MODEL), F32), ((D_MODEL, IN_TILE), BF16), ((tm, IN_TILE), F32)]
    return pl.pallas_call(
        _inproj_body,
        grid=(m // tm, IN_COLS // IN_TILE),
        in_specs=[pl.BlockSpec((tm, D_MODEL), lambda i, j: (i, 0)),
                  pl.BlockSpec((1, D_MODEL), lambda i, j: (0, 0)),
                  pl.BlockSpec((D_MODEL, IN_TILE), lambda i, j: (0, _w_in_tile(j)))],
        out_specs=pl.BlockSpec((tm, IN_TILE), lambda i, j: (i, j)),
        out_shape=jax.ShapeDtypeStruct((m, IN_COLS), F32),
        scratch_shapes=[pltpu.VMEM((tm, D_MODEL), BF16)],
        compiler_params=_params(("parallel", "arbitrary"), blocks,
                                resident=[((tm, D_MODEL), BF16)], temps=[((tm, D_MODEL), F32)]),
        name="inproj",
    )(x, g.reshape(1, D_MODEL), w)


def _ln_silu(c, g, b):
    mu = jnp.mean(c, axis=-1, keepdims=True)
    xc = c - mu
    y = xc * lax.rsqrt(jnp.mean(xc * xc, axis=-1, keepdims=True) + EPS) * g + b
    return y * jax.nn.sigmoid(y)


def _conv_prompt_body(a_ref, g_ref, w_ref, b_ref, lng_ref, lnb_ref, act_ref, st_ref, xx_ref, cv_ref, *, tt):
    i = pl.program_id(1)

    @pl.when(i == 0)
    def _():
        xx_ref[0:CONV_HALO, :] = jnp.zeros((CONV_HALO, D_CONV), F32)

    xx_ref[CONV_HALO:CONV_HALO + tt, :] = a_ref[...] * jax.nn.sigmoid(g_ref[...])

    first = CONV_HALO - (CONV_K - 1)

    for r0 in range(0, tt, CONV_ROW_CHUNK):
        for lc in range(D_CONV // CONV_LANE_CHUNK):
            ls = pl.ds(lc * CONV_LANE_CHUNK, CONV_LANE_CHUNK)
            acc = jnp.broadcast_to(b_ref[:, ls], (CONV_ROW_CHUNK, CONV_LANE_CHUNK))
            for j in range(CONV_K):
                acc = acc + w_ref[j:j + 1, ls] * xx_ref[r0 + first + j:r0 + first + j + CONV_ROW_CHUNK, ls]
            cv_ref[r0:r0 + CONV_ROW_CHUNK, ls] = acc
    act_ref[...] = _ln_silu(cv_ref[...], lng_ref[...], lnb_ref[...]).astype(BF16)

    @pl.when(i == pl.num_programs(1) - 1)
    def _():
        st_ref[0] = xx_ref[tt + first:tt + CONV_HALO, :]

    xx_ref[0:CONV_HALO, :] = xx_ref[tt:tt + CONV_HALO, :]


def _conv_prompt(proj, n, t, w, b, lng, lnb, tt=512):
    tps = t // tt
    vec = pl.BlockSpec((1, D_CONV), lambda s, i: (0, 0))
    blocks = [((tt, D_CONV), F32)] * 2 + [((tt, D_CONV), BF16)]
    return pl.pallas_call(
        functools.partial(_conv_prompt_body, tt=tt),
        grid=(n, tps),
        in_specs=[pl.BlockSpec((tt, D_CONV), lambda s, i: (s * tps + i, COL_GLU_A // D_CONV)),
                  pl.BlockSpec((tt, D_CONV), lambda s, i: (s * tps + i, COL_GLU_G // D_CONV)),
                  pl.BlockSpec((CONV_K, D_CONV), lambda s, i: (0, 0)),
                  vec, vec, vec],
        out_specs=[pl.BlockSpec((tt, D_CONV), lambda s, i: (s * tps + i, 0)),
                   pl.BlockSpec((1, CONV_K - 1, D_CONV), lambda s, i: (s, 0, 0))],
        out_shape=[jax.ShapeDtypeStruct((n * t, D_CONV), BF16),
                   jax.ShapeDtypeStruct((n, CONV_K - 1, D_CONV), F32)],
        scratch_shapes=[pltpu.VMEM((CONV_HALO + tt, D_CONV), F32), pltpu.VMEM((tt, D_CONV), F32)],
        compiler_params=_params(("parallel", "arbitrary"), blocks,
                                resident=[((CONV_HALO + tt, D_CONV), F32), ((tt, D_CONV), F32)],
                                temps=[((tt, D_CONV), F32)] * 3),
        name="conv_prompt",
    )(proj, proj, w, b.reshape(1, -1), lng.reshape(1, -1), lnb.reshape(1, -1))


def _conv_sample_body(a_ref, g_ref, s_ref, w_ref, b_ref, lng_ref, lnb_ref, act_ref, st_ref, xx_ref, cv_ref,
                      *, nb, t):
    hist = CONV_K - 1
    xx_ref[:, 0:hist, :] = s_ref[...]
    u = a_ref[...] * jax.nn.sigmoid(g_ref[...])
    xx_ref[:, hist:hist + t, :] = u.reshape(nb, t, D_CONV)
    st_ref[...] = xx_ref[:, t:t + hist, :]
    for lc in range(D_CONV // CONV_LANE_CHUNK):
        ls = pl.ds(lc * CONV_LANE_CHUNK, CONV_LANE_CHUNK)
        acc = jnp.broadcast_to(b_ref[:, ls], (nb, t, CONV_LANE_CHUNK))
        for j in range(CONV_K):
            acc = acc + w_ref[j:j + 1, ls] * xx_ref[:, j:j + t, ls]
        cv_ref[:, ls] = acc.reshape(nb * t, CONV_LANE_CHUNK)
    act_ref[...] = _ln_silu(cv_ref[...], lng_ref[...], lnb_ref[...]).astype(BF16)


def _conv_sample(proj, state, n, t, w, b, lng, lnb, nb=16):
    hist = CONV_K - 1
    rows = nb * t
    xx_rows = -(-(hist + t) // V7X_SUBLANES) * V7X_SUBLANES
    vec = pl.BlockSpec((1, D_CONV), lambda s: (0, 0))
    blocks = [((rows, D_CONV), F32)] * 2 + [((nb, CONV_HALO, D_CONV), F32)] * 2 + [((rows, D_CONV), BF16)]
    return pl.pallas_call(
        functools.partial(_conv_sample_body, nb=nb, t=t),
        grid=(n // nb,),
        in_specs=[pl.BlockSpec((rows, D_CONV), lambda s: (s, COL_GLU_A // D_CONV)),
                  pl.BlockSpec((rows, D_CONV), lambda s: (s, COL_GLU_G // D_CONV)),
                  pl.BlockSpec((nb, hist, D_CONV), lambda s: (s, 0, 0)),
                  pl.BlockSpec((CONV_K, D_CONV), lambda s: (0, 0)),
                  vec, vec, vec],
        out_specs=[pl.BlockSpec((rows, D_CONV), lambda s: (s, 0)),
                   pl.BlockSpec((nb, hist, D_CONV), lambda s: (s, 0, 0))],
        out_shape=[jax.ShapeDtypeStruct((n * t, D_CONV), BF16),
                   jax.ShapeDtypeStruct((n, hist, D_CONV), F32)],
        scratch_shapes=[pltpu.VMEM((nb, xx_rows, D_CONV), F32), pltpu.VMEM((rows, D_CONV), F32)],
        compiler_params=_params(("parallel",), blocks,
                                resident=[((nb, xx_rows, D_CONV), F32), ((rows, D_CONV), F32)],
                                temps=[((rows, D_CONV), F32)] * 3),
        name="conv_sample",
    )(proj, proj, state, w, b.reshape(1, -1), lng.reshape(1, -1), lnb.reshape(1, -1))


def _softmax_pv(s, valid, distf, slope, sink, v):
    s = jnp.where(valid, s - slope * distf, NEG)
    m = jnp.maximum(jnp.max(s, axis=-1, keepdims=True), sink)
    p = jnp.exp(s - m)
    denom = jnp.sum(p, axis=-1, keepdims=True) + jnp.exp(sink - m)
    return p, denom


def _attn_prompt_body(sink_ref, q_ref, kc_ref, vc_ref, kp_ref, vp_ref, o_ref):
    i = pl.program_id(1)
    blk = WINDOW
    q = q_ref[...] * SCALE
    k = jnp.concatenate([kp_ref[...], kc_ref[...]], axis=0).astype(BF16)
    v = jnp.concatenate([vp_ref[...], vc_ref[...]], axis=0).astype(BF16)
    r = lax.broadcasted_iota(jnp.int32, (blk, 2 * blk), 0)
    c = lax.broadcasted_iota(jnp.int32, (blk, 2 * blk), 1)
    dist = blk + r - c
    valid = (dist >= 0) & (dist <= WINDOW) & ((c >= blk) | (i > 0))
    distf = dist.astype(F32)
    for kv in range(N_KV_HEADS):
        kh = k[:, kv * HEAD_DIM:(kv + 1) * HEAD_DIM]
        vh = v[:, kv * HEAD_DIM:(kv + 1) * HEAD_DIM]
        heads = [kv * GROUP + g for g in range(GROUP)]
        qs = jnp.concatenate([q[:, h * HEAD_DIM:(h + 1) * HEAD_DIM] for h in heads], axis=0).astype(BF16)
        s = lax.dot_general(qs, kh, (((1,), (1,)), ((), ())), preferred_element_type=F32)
        ps, ds = [], []
        for g, h in enumerate(heads):
            p, d = _softmax_pv(s[g * blk:(g + 1) * blk], valid, distf, SLOPES[h], sink_ref[0, h], vh)
            ps.append(p.astype(BF16))
            ds.append(d)
        o = _dot(jnp.concatenate(ps, axis=0), vh)
        for g, h in enumerate(heads):
            o_ref[:, h * HEAD_DIM:(h + 1) * HEAD_DIM] = (o[g * blk:(g + 1) * blk] / ds[g]).astype(BF16)


def _attn_prompt(proj, sinks, n, t):
    blk = WINDOW
    nb = t // blk
    kcol, vcol = COL_K // D_KV, COL_V // D_KV
    blocks = [((blk, D_ATTN), F32)] + [((blk, D_KV), F32)] * 4 + [((blk, D_ATTN), BF16)]
    return pl.pallas_call(
        _attn_prompt_body,
        grid=(n, nb),
        in_specs=[pl.BlockSpec(memory_space=pltpu.SMEM),
                  pl.BlockSpec((blk, D_ATTN), lambda s, i: (s * nb + i, COL_Q // D_ATTN)),
                  pl.BlockSpec((blk, D_KV), lambda s, i: (s * nb + i, kcol)),
                  pl.BlockSpec((blk, D_KV), lambda s, i: (s * nb + i, vcol)),
                  pl.BlockSpec((blk, D_KV), lambda s, i: (s * nb + jnp.maximum(i - 1, 0), kcol)),
                  pl.BlockSpec((blk, D_KV), lambda s, i: (s * nb + jnp.maximum(i - 1, 0), vcol))],
        out_specs=pl.BlockSpec((blk, D_ATTN), lambda s, i: (s * nb + i, 0)),
        out_shape=jax.ShapeDtypeStruct((n * t, D_ATTN), BF16),
        compiler_params=_params(("parallel", "parallel"), blocks,
                                temps=[((GROUP * blk, 2 * blk), F32)] * 8),
        name="attn_prompt",
    )(sinks.reshape(1, N_HEADS), proj, proj, proj, proj, proj)


def _attn_sample_body(sink_ref, q_ref, kn_ref, vn_ref, ck_ref, cv_ref, o_ref, nk_ref, nv_ref, kk_ref, vv_ref,
                      *, nb, t):
    keys = 2 * WINDOW
    for src_new, src_cache, dst, out in ((kn_ref, ck_ref, kk_ref, nk_ref), (vn_ref, cv_ref, vv_ref, nv_ref)):
        dst[:, 0:WINDOW, :] = src_cache[...]
        dst[:, WINDOW:WINDOW + t, :] = src_new[...].reshape(nb, t, D_KV)
        dst[:, WINDOW + t:keys, :] = jnp.zeros((nb, keys - WINDOW - t, D_KV), F32)
        out[...] = dst[:, t:t + WINDOW, :]
    q3 = (q_ref[...] * SCALE).reshape(nb, t, D_ATTN)
    k3 = kk_ref[...].astype(BF16)
    v3 = vv_ref[...].astype(BF16)
    rows = GROUP * t
    r = lax.broadcasted_iota(jnp.int32, (rows, keys), 0)
    c = lax.broadcasted_iota(jnp.int32, (rows, keys), 1)
    dist = WINDOW + r % t - c
    valid = (dist >= 0) & (dist <= WINDOW)
    distf = dist.astype(F32)
    gi = lax.broadcasted_iota(jnp.int32, (rows, 1), 0) // t
    for kv in range(N_KV_HEADS):
        heads = [kv * GROUP + g for g in range(GROUP)]
        slope = jnp.zeros((rows, 1), F32)
        sink = jnp.zeros((rows, 1), F32)
        for g, h in enumerate(heads):
            slope = jnp.where(gi == g, SLOPES[h], slope)
            sink = jnp.where(gi == g, sink_ref[0, h], sink)
        kh = k3[:, :, kv * HEAD_DIM:(kv + 1) * HEAD_DIM]
        vh = v3[:, :, kv * HEAD_DIM:(kv + 1) * HEAD_DIM]
        qs = jnp.concatenate([q3[:, :, h * HEAD_DIM:(h + 1) * HEAD_DIM] for h in heads], axis=1).astype(BF16)
        s = jnp.einsum("bqd,bkd->bqk", qs, kh, preferred_element_type=F32)
        p, d = _softmax_pv(s, valid, distf, slope, sink, vh)
        o = jnp.einsum("bqk,bkd->bqd", p.astype(BF16), vh, preferred_element_type=F32) / d
        for g, h in enumerate(heads):
            o_ref[:, h * HEAD_DIM:(h + 1) * HEAD_DIM] = (
                o[:, g * t:(g + 1) * t, :].reshape(nb * t, HEAD_DIM).astype(BF16))


def _attn_sample(proj, cache_k, cache_v, sinks, n, t, nb=8):
    rows = nb * t
    keys = 2 * WINDOW
    kcol, vcol = COL_K // D_KV, COL_V // D_KV
    cache = pl.BlockSpec((nb, WINDOW, D_KV), lambda s: (s, 0, 0))
    blocks = ([((rows, D_ATTN), F32)] + [((rows, D_KV), F32)] * 2 + [((nb, WINDOW, D_KV), F32)] * 4
              + [((rows, D_ATTN), BF16)])
    return pl.pallas_call(
        functools.partial(_attn_sample_body, nb=nb, t=t),
        grid=(n // nb,),
        in_specs=[pl.BlockSpec(memory_space=pltpu.SMEM),
                  pl.BlockSpec((rows, D_ATTN), lambda s: (s, COL_Q // D_ATTN)),
                  pl.BlockSpec((rows, D_KV), lambda s: (s, kcol)),
                  pl.BlockSpec((rows, D_KV), lambda s: (s, vcol)),
                  cache, cache],
        out_specs=[pl.BlockSpec((rows, D_ATTN), lambda s: (s, 0)), cache, cache],
        out_shape=[jax.ShapeDtypeStruct((n * t, D_ATTN), BF16),
                   jax.ShapeDtypeStruct((n, WINDOW, D_KV), F32),
                   jax.ShapeDtypeStruct((n, WINDOW, D_KV), F32)],
        scratch_shapes=[pltpu.VMEM((nb, keys, D_KV), F32)] * 2,
        compiler_params=_params(("parallel",), blocks,
                                resident=[((nb, keys, D_KV), F32)] * 2,
                                temps=[((nb, keys, D_KV), F32)] * 2 + [((nb, GROUP * t, keys), F32)] * 6),
        name="attn_sample",
    )(sinks.reshape(1, N_HEADS), proj, proj, proj, cache_k, cache_v)


def _mix_body(c_ref, a_ref, gc_ref, ga_ref, x_ref, wco_ref, wao_ref, wo_ref, n2_ref, x1_ref, h2_ref):
    branch_c = _dot(c_ref[...], wco_ref[...])
    branch_a = _dot(a_ref[...], wao_ref[...])
    merged = jax.nn.sigmoid(gc_ref[...]) * branch_c + jax.nn.sigmoid(ga_ref[...]) * branch_a
    x1 = x_ref[...] + _dot(merged.astype(BF16), wo_ref[...])
    x1_ref[...] = x1
    h2_ref[...] = _rms(x1, n2_ref[...]).astype(BF16)


def _mix(c_act, a_act, proj, x, wco, wao, wo, n2, tm=256):
    m = x.shape[0]
    row = lambda width, col: pl.BlockSpec((tm, width), lambda i: (i, col))
    const = lambda shape: pl.BlockSpec(shape, lambda i: (0, 0), pipeline_mode=pl.Buffered(1))
    blocks = ([((tm, D_CONV), BF16), ((tm, D_ATTN), BF16)] + [((tm, D_MODEL), F32)] * 4 + [((tm, D_MODEL), BF16)])
    weights = [((D_CONV, D_MODEL), BF16), ((D_ATTN, D_MODEL), BF16), ((D_MODEL, D_MODEL), BF16)]
    return pl.pallas_call(
        _mix_body,
        grid=(m // tm,),
        in_specs=[row(D_CONV, 0), row(D_ATTN, 0),
                  row(D_MODEL, COL_GATE_C // D_MODEL), row(D_MODEL, COL_GATE_A // D_MODEL),
                  row(D_MODEL, 0),
                  const((D_CONV, D_MODEL)), const((D_ATTN, D_MODEL)), const((D_MODEL, D_MODEL)),
                  pl.BlockSpec((1, D_MODEL), lambda i: (0, 0))],
        out_specs=[row(D_MODEL, 0), row(D_MODEL, 0)],
        out_shape=[jax.ShapeDtypeStruct((m, D_MODEL), F32), jax.ShapeDtypeStruct((m, D_MODEL), BF16)],
        compiler_params=_params(("parallel",), blocks, resident=weights, temps=[((tm, D_MODEL), F32)] * 4),
        name="mix",
    )(c_act, a_act, proj, proj, x, wco, wao, wo, n2.reshape(1, D_MODEL))


FFN_TILE = 512


def _ffn_conv(e_ref, w_ref, b_ref, tm):
    first = FFN_HALO - (FFN_K - 1)
    out = b_ref[...]
    for j in range(FFN_K):
        out = out + w_ref[j:j + 1, :] * e_ref[first + j:first + j + tm, :]
    return out


def _up_prompt_body(h_ref, wg_ref, wv_ref, cwg_ref, cwv_ref, cbg_ref, cbv_ref, act_ref, sg_ref, sv_ref,
                    eg_ref, ev_ref, *, tm, tiles_per_seq):
    pos = pl.program_id(1) % tiles_per_seq

    @pl.when(pos == 0)
    def _():
        eg_ref[0:FFN_HALO, :] = jnp.zeros((FFN_HALO, FFN_TILE), F32)
        ev_ref[0:FFN_HALO, :] = jnp.zeros((FFN_HALO, FFN_TILE), F32)

    h = h_ref[...]
    eg_ref[FFN_HALO:FFN_HALO + tm, :] = _dot(h, wg_ref[...])
    ev_ref[FFN_HALO:FFN_HALO + tm, :] = _dot(h, wv_ref[...])
    gate = _ffn_conv(eg_ref, cwg_ref, cbg_ref, tm)
    val = _ffn_conv(ev_ref, cwv_ref, cbv_ref, tm)
    act_ref[...] = (gate * jax.nn.sigmoid(gate) * val).astype(BF16)

    @pl.when(pos == tiles_per_seq - 1)
    def _():
        hist = FFN_K - 1
        sg_ref[0] = eg_ref[FFN_HALO + tm - hist:FFN_HALO + tm, :]
        sv_ref[0] = ev_ref[FFN_HALO + tm - hist:FFN_HALO + tm, :]

    eg_ref[0:FFN_HALO, :] = eg_ref[tm:tm + FFN_HALO, :]
    ev_ref[0:FFN_HALO, :] = ev_ref[tm:tm + FFN_HALO, :]


def _up_prompt(h2, n, t, w_up, cw, cb, tm=1024):
    m = n * t
    tps = t // tm
    nt = D_FF // FFN_TILE
    hist = FFN_K - 1
    col = lambda rows, off: pl.BlockSpec((rows, FFN_TILE), lambda j, i: (0, j + off))
    blocks = ([((tm, D_MODEL), BF16)] + [((D_MODEL, FFN_TILE), BF16)] * 2 + [((tm, FFN_TILE), BF16)])
    scratch = [((FFN_HALO + tm, FFN_TILE), F32)] * 2
    state = pl.BlockSpec((1, hist, FFN_TILE), lambda j, i: (i // tps, 0, j))
    return pl.pallas_call(
        functools.partial(_up_prompt_body, tm=tm, tiles_per_seq=tps),
        grid=(nt, m // tm),
        in_specs=[pl.BlockSpec((tm, D_MODEL), lambda j, i: (i, 0)),
                  col(D_MODEL, 0), col(D_MODEL, nt), col(FFN_K, 0), col(FFN_K, nt), col(1, 0), col(1, nt)],
        out_specs=[pl.BlockSpec((tm, FFN_TILE), lambda j, i: (i, j)), state, state],
        out_shape=[jax.ShapeDtypeStruct((m, D_FF), BF16),
                   jax.ShapeDtypeStruct((n, hist, D_FF), F32), jax.ShapeDtypeStruct((n, hist, D_FF), F32)],
        scratch_shapes=[pltpu.VMEM(s, d) for s, d in scratch],
        compiler_params=_params(("parallel", "arbitrary"), blocks, resident=scratch,
                                temps=[((tm, FFN_TILE), F32)] * 4),
        name="up_prompt",
    )(h2, w_up, w_up, cw, cw, cb.reshape(1, -1), cb.reshape(1, -1))


def _ffn_conv_sample(u3, st_ref, w_ref, b_ref, tpos):
    st0 = st_ref[:, 0:1, :]
    st1 = st_ref[:, 1:2, :]
    prev1 = jnp.where(tpos == 0, st1, pltpu.roll(u3, 1, axis=1))
    prev2 = jnp.where(tpos == 0, st0, jnp.where(tpos == 1, st1, pltpu.roll(u3, 2, axis=1)))
    return b_ref[...] + w_ref[0:1, :] * prev2 + w_ref[1:2, :] * prev1 + w_ref[2:3, :] * u3


def _up_sample_body(h_ref, wg_ref, wv_ref, cwg_ref, cwv_ref, cbg_ref, cbv_ref, stg_ref, stv_ref,
                    act_ref, sg_ref, sv_ref, *, n, t):
    hist = FFN_K - 1
    h = h_ref[...]
    ug = _dot(h, wg_ref[...]).reshape(n, t, FFN_TILE)
    uv = _dot(h, wv_ref[...]).reshape(n, t, FFN_TILE)
    tpos = lax.broadcasted_iota(jnp.int32, (1, t, 1), 1)
    gate = _ffn_conv_sample(ug, stg_ref, cwg_ref, cbg_ref, tpos)
    val = _ffn_conv_sample(uv, stv_ref, cwv_ref, cbv_ref, tpos)
    act_ref[...] = (gate * jax.nn.sigmoid(gate) * val).reshape(n * t, FFN_TILE).astype(BF16)
    sg_ref[...] = ug[:, t - hist:t, :]
    sv_ref[...] = uv[:, t - hist:t, :]


def _up_sample(h2, state, n, t, w_up, cw, cb):
    assert t == V7X_SUBLANES and FFN_K == 3
    m = n * t
    nt = D_FF // FFN_TILE
    hist = FFN_K - 1
    col = lambda rows, off: pl.BlockSpec((rows, FFN_TILE), lambda j: (0, j + off))
    st = lambda off: pl.BlockSpec((n, hist, FFN_TILE), lambda j: (0, 0, j + off))
    blocks = ([((D_MODEL, FFN_TILE), BF16)] * 2 + [((n, V7X_SUBLANES, FFN_TILE), F32)] * 4
              + [((m, FFN_TILE), BF16)])
    return pl.pallas_call(
        functools.partial(_up_sample_body, n=n, t=t),
        grid=(nt,),
        in_specs=[pl.BlockSpec((m, D_MODEL), lambda j: (0, 0)),
                  col(D_MODEL, 0), col(D_MODEL, nt), col(FFN_K, 0), col(FFN_K, nt), col(1, 0), col(1, nt),
                  st(0), st(nt)],
        out_specs=[pl.BlockSpec((m, FFN_TILE), lambda j: (0, j)), st(0), st(0)],
        out_shape=[jax.ShapeDtypeStruct((m, D_FF), BF16),
                   jax.ShapeDtypeStruct((n, hist, D_FF), F32), jax.ShapeDtypeStruct((n, hist, D_FF), F32)],
        compiler_params=_params(("parallel",), blocks, resident=[((m, D_MODEL), BF16)] * 2,
                                temps=[((m, FFN_TILE), F32)] * 8),
        name="up_sample",
    )(h2, w_up, w_up, cw, cw, cb.reshape(1, -1), cb.reshape(1, -1), state, state)


DOWN_TILE = 512


def _down_body(act_ref, w_ref, x_ref, o_ref):
    o_ref[...] = x_ref[...] + _dot(act_ref[...], w_ref[...])


def _down(act, w_down, x1, tm=1024):
    m = x1.shape[0]
    blocks = [((tm, D_FF), BF16), ((D_FF, DOWN_TILE), BF16), ((tm, DOWN_TILE), F32), ((tm, DOWN_TILE), F32)]
    return pl.pallas_call(
        _down_body,
        grid=(m // tm, D_MODEL // DOWN_TILE),
        in_specs=[pl.BlockSpec((tm, D_FF), lambda i, j: (i, 0)),
                  pl.BlockSpec((D_FF, DOWN_TILE), lambda i, j: (0, j)),
                  pl.BlockSpec((tm, DOWN_TILE), lambda i, j: (i, j))],
        out_specs=pl.BlockSpec((tm, DOWN_TILE), lambda i, j: (i, j)),
        out_shape=jax.ShapeDtypeStruct((m, D_MODEL), F32),
        compiler_params=_params(("parallel", "parallel"), blocks, temps=[((tm, DOWN_TILE), F32)]),
        name="down",
    )(act, w_down, x1)


def _final_norm_body(x_ref, g_ref, o_ref):
    o_ref[...] = _rms(x_ref[...], g_ref[...])


def _final_norm(x, g, tm=512):
    m = x.shape[0]
    blocks = [((tm, D_MODEL), F32)] * 2
    return pl.pallas_call(
        _final_norm_body,
        grid=(m // tm,),
        in_specs=[pl.BlockSpec((tm, D_MODEL), lambda i: (i, 0)), pl.BlockSpec((1, D_MODEL), lambda i: (0, 0))],
        out_specs=pl.BlockSpec((tm, D_MODEL), lambda i: (i, 0)),
        out_shape=jax.ShapeDtypeStruct((m, D_MODEL), F32),
        compiler_params=_params(("parallel",), blocks, temps=[((tm, D_MODEL), F32)]),
        name="final_norm",
    )(x, g.reshape(1, D_MODEL))


def kernel(x_prompt, x_sample, cache_k, cache_v, state_conv, state_ffn_conv, norm1_g, w_in, conv_w, conv_b,
           conv_ln_g, conv_ln_b, w_conv_out, attn_sinks, w_attn_out, w_out, norm2_g, w_up, ffn_conv_w,
           ffn_conv_b, w_down, final_norm_g):
    depth = w_in.shape[0]
    n_p, t_p, _ = x_prompt.shape
    n_s, t_s, _ = x_sample.shape
    xp = x_prompt.reshape(n_p * t_p, D_MODEL)
    xs = x_sample.reshape(n_s * t_s, D_MODEL)
    w_in_b, w_co_b, w_ao_b, w_o_b, w_up_b, w_dn_b = (
        w.astype(BF16) for w in (w_in, w_conv_out, w_attn_out, w_out, w_up, w_down))
    outs = {k: [] for k in ("kp", "vp", "cp", "fp", "ks", "vs", "cs", "fs")}
    for l in range(depth):
        conv_args = (conv_w[l], conv_b[l], conv_ln_g[l], conv_ln_b[l])
        proj = _inproj(xp, norm1_g[l], w_in_b[l], tm=1024)
        c_act, c_state = _conv_prompt(proj, n_p, t_p, *conv_args)
        a_act = _attn_prompt(proj, attn_sinks[l], n_p, t_p)
        x1, h2 = _mix(c_act, a_act, proj, xp, w_co_b[l], w_ao_b[l], w_o_b[l], norm2_g[l])
        act, f_g, f_v = _up_prompt(h2, n_p, t_p, w_up_b[l], ffn_conv_w[l], ffn_conv_b[l])
        xp = _down(act, w_dn_b[l], x1)
        kv = proj.reshape(n_p, t_p, IN_COLS)[:, t_p - WINDOW:, :]
        outs["kp"].append(kv[:, :, COL_K:COL_K + D_KV].reshape(n_p, WINDOW, N_KV_HEADS, HEAD_DIM))
        outs["vp"].append(kv[:, :, COL_V:COL_V + D_KV].reshape(n_p, WINDOW, N_KV_HEADS, HEAD_DIM))
        outs["cp"].append(c_state)
        outs["fp"].append(jnp.concatenate([f_g, f_v], axis=-1))
        proj = _inproj(xs, norm1_g[l], w_in_b[l], tm=n_s * t_s)
        c_act, c_state = _conv_sample(proj, state_conv[l], n_s, t_s, *conv_args)
        a_act, k_new, v_new = _attn_sample(proj, cache_k[l].reshape(n_s, WINDOW, D_KV),
                                           cache_v[l].reshape(n_s, WINDOW, D_KV), attn_sinks[l], n_s, t_s)
        x1, h2 = _mix(c_act, a_act, proj, xs, w_co_b[l], w_ao_b[l], w_o_b[l], norm2_g[l])
        act, f_g, f_v = _up_sample(h2, state_ffn_conv[l], n_s, t_s, w_up_b[l], ffn_conv_w[l], ffn_conv_b[l])
        xs = _down(act, w_dn_b[l], x1)
        outs["ks"].append(k_new.reshape(n_s, WINDOW, N_KV_HEADS, HEAD_DIM))
        outs["vs"].append(v_new.reshape(n_s, WINDOW, N_KV_HEADS, HEAD_DIM))
        outs["cs"].append(c_state)
        outs["fs"].append(jnp.concatenate([f_g, f_v], axis=-1))
    y_prompt = _final_norm(xp, final_norm_g).reshape(n_p, t_p, D_MODEL)
    y_sample = _final_norm(xs, final_norm_g).reshape(n_s, t_s, D_MODEL)
    return (y_prompt, y_sample) + tuple(jnp.stack(outs[k]) for k in ("kp", "vp", "cp", "fp", "ks", "vs", "cs", "fs"))
```

```python
import functools

import jax
import jax.numpy as jnp
from jax import lax
from jax.experimental import pallas as pl
from jax.experimental.pallas import tpu as pltpu

F32 = jnp.float32
BF16 = jnp.bfloat16

D_MODEL = 2048
HEAD_DIM = 64
N_HEADS = 16
N_KV_HEADS = 4
GROUP = N_HEADS // N_KV_HEADS
D_ATTN = N_HEADS * HEAD_DIM
D_KV = N_KV_HEADS * HEAD_DIM
WINDOW = 128
D_CONV = D_MODEL // 2
CONV_K = 31
D_FF = 3 * D_MODEL
FFN_K = 3
EPS = 1e-6
IN_COLS = 2 * D_CONV + D_ATTN + 2 * D_KV + 2 * D_MODEL
NEG = -1e30
SCALE = HEAD_DIM ** -0.5
SLOPES = tuple(2.0 ** (-8.0 * (h + 1) / N_HEADS) for h in range(N_HEADS))

V7X_SUBLANES = 8
V7X_SCOPED_VMEM_CAP_BYTES = 60000 * 1024

IN_TILE = 512
COL_GLU_A = 0
COL_GLU_G = D_CONV
COL_GATE_C = 2 * D_CONV
COL_GATE_A = COL_GATE_C + D_MODEL
COL_Q = COL_GATE_A + D_MODEL
COL_K = COL_Q + D_ATTN
COL_V = COL_K + D_KV

CONV_HALO = 32
CONV_ROW_CHUNK = 64
CONV_LANE_CHUNK = 256
FFN_HALO = V7X_SUBLANES
FFN_TILE = 512
DOWN_TILE = 512


def _nbytes(shape, dtype):
    n = 1
    for s in shape:
        n *= s
    return n * jnp.dtype(dtype).itemsize


def _params(semantics, pipelined, resident=(), temps=()):
    est = 2 * sum(_nbytes(s, d) for s, d in pipelined)
    est += sum(_nbytes(s, d) for s, d in resident)
    est += sum(_nbytes(s, d) for s, d in temps)
    limit = min(V7X_SCOPED_VMEM_CAP_BYTES, est + est // 4)
    return pltpu.CompilerParams(dimension_semantics=semantics, vmem_limit_bytes=limit)


def _layer_vec(l, width, rows=1):
    return pl.BlockSpec((None, rows, width), lambda *_: (l, 0, 0))


def _as_rows(p):
    return p.reshape(p.shape[0], 1, p.shape[1])


def _rms(x, g):
    return x * lax.rsqrt(jnp.mean(x * x, axis=-1, keepdims=True) + EPS) * g


def _dot(a, b):
    return jnp.dot(a, b, preferred_element_type=F32)


def _skip_first_ref(body):
    def wrapped(_, *refs):
        body(*refs)
    return wrapped


def _stacked_call(body, l, prev, n_alias_out, **kw):
    if l == 0:
        return pl.pallas_call(body, **kw)
    for _ in prev:
        body = _skip_first_ref(body)
    kw["in_specs"] = [pl.BlockSpec(memory_space=pl.ANY)] * len(prev) + list(kw["in_specs"])
    call = pl.pallas_call(body, input_output_aliases={i: i for i in range(n_alias_out)}, **kw)
    return lambda *args: call(*prev, *args)


def _inproj_body(x_ref, g_ref, w_ref, o_ref, h_ref):
    @pl.when(pl.program_id(1) == 0)
    def _():
        h_ref[...] = _rms(x_ref[...], g_ref[...]).astype(BF16)

    o_ref[...] = _dot(h_ref[...], w_ref[...].astype(BF16))


def _w_in_tile(j):
    n_glu = 2 * D_CONV // IN_TILE
    n_qkv = (D_ATTN + 2 * D_KV) // IN_TILE
    n_gate = 2 * D_MODEL // IN_TILE
    return jnp.where(j < n_glu, j, jnp.where(j < n_glu + n_gate, j + n_qkv, j - n_gate))


def _inproj(x, g, w, l, tm):
    m = x.shape[0]
    blocks = [((tm, D_MODEL), F32), ((D_MODEL, IN_TILE), F32), ((tm, IN_TILE), F32)]
    return pl.pallas_call(
        _inproj_body,
        grid=(m // tm, IN_COLS // IN_TILE),
        in_specs=[pl.BlockSpec((tm, D_MODEL), lambda i, j: (i, 0)),
                  _layer_vec(l, D_MODEL),
                  pl.BlockSpec((None, D_MODEL, IN_TILE), lambda i, j: (l, 0, _w_in_tile(j)))],
        out_specs=pl.BlockSpec((tm, IN_TILE), lambda i, j: (i, j)),
        out_shape=jax.ShapeDtypeStruct((m, IN_COLS), F32),
        scratch_shapes=[pltpu.VMEM((tm, D_MODEL), BF16)],
        compiler_params=_params(("parallel", "arbitrary"), blocks,
                                resident=[((tm, D_MODEL), BF16), ((D_MODEL, IN_TILE), BF16)],
                                temps=[((tm, D_MODEL), F32)]),
        name="inproj",
    )(x, _as_rows(g), w)


def _ln_silu(c, g, b):
    mu = jnp.mean(c, axis=-1, keepdims=True)
    xc = c - mu
    y = xc * lax.rsqrt(jnp.mean(xc * xc, axis=-1, keepdims=True) + EPS) * g + b
    return y * jax.nn.sigmoid(y)


def _conv_prompt_body(a_ref, g_ref, w_ref, b_ref, lng_ref, lnb_ref, act_ref, st_ref, xx_ref, cv_ref, *, tt):
    i = pl.program_id(1)

    @pl.when(i == 0)
    def _():
        xx_ref[0:CONV_HALO, :] = jnp.zeros((CONV_HALO, D_CONV), F32)

    xx_ref[CONV_HALO:CONV_HALO + tt, :] = a_ref[...] * jax.nn.sigmoid(g_ref[...])

    first = CONV_HALO - (CONV_K - 1)
    for r0 in range(0, tt, CONV_ROW_CHUNK):
        for lc in range(D_CONV // CONV_LANE_CHUNK):
            ls = pl.ds(lc * CONV_LANE_CHUNK, CONV_LANE_CHUNK)
            acc = jnp.broadcast_to(b_ref[:, ls], (CONV_ROW_CHUNK, CONV_LANE_CHUNK))
            for j in range(CONV_K):
                acc = acc + w_ref[j:j + 1, ls] * xx_ref[r0 + first + j:r0 + first + j + CONV_ROW_CHUNK, ls]
            cv_ref[r0:r0 + CONV_ROW_CHUNK, ls] = acc
    act_ref[...] = _ln_silu(cv_ref[...], lng_ref[...], lnb_ref[...]).astype(BF16)

    @pl.when(i == pl.num_programs(1) - 1)
    def _():
        st_ref[0] = xx_ref[tt + first:tt + CONV_HALO, :]

    xx_ref[0:CONV_HALO, :] = xx_ref[tt:tt + CONV_HALO, :]


def _conv_prompt(proj, n, t, w, b, lng, lnb, l, tt=512):
    tps = t // tt
    vec = _layer_vec(l, D_CONV)
    blocks = [((tt, D_CONV), F32)] * 2 + [((tt, D_CONV), BF16)]
    return pl.pallas_call(
        functools.partial(_conv_prompt_body, tt=tt),
        grid=(n, tps),
        in_specs=[pl.BlockSpec((tt, D_CONV), lambda s, i: (s * tps + i, COL_GLU_A // D_CONV)),
                  pl.BlockSpec((tt, D_CONV), lambda s, i: (s * tps + i, COL_GLU_G // D_CONV)),
                  _layer_vec(l, D_CONV, CONV_K), vec, vec, vec],
        out_specs=[pl.BlockSpec((tt, D_CONV), lambda s, i: (s * tps + i, 0)),
                   pl.BlockSpec((1, CONV_K - 1, D_CONV), lambda s, i: (s, 0, 0))],
        out_shape=[jax.ShapeDtypeStruct((n * t, D_CONV), BF16),
                   jax.ShapeDtypeStruct((n, CONV_K - 1, D_CONV), F32)],
        scratch_shapes=[pltpu.VMEM((CONV_HALO + tt, D_CONV), F32), pltpu.VMEM((tt, D_CONV), F32)],
        compiler_params=_params(("parallel", "arbitrary"), blocks,
                                resident=[((CONV_HALO + tt, D_CONV), F32), ((tt, D_CONV), F32)],
                                temps=[((tt, D_CONV), F32)] * 3),
        name="conv_prompt",
    )(proj, proj, w, _as_rows(b), _as_rows(lng), _as_rows(lnb))


def _conv_sample_body(a_ref, g_ref, s_ref, w_ref, b_ref, lng_ref, lnb_ref, st_ref, act_ref, xx_ref, cv_ref,
                      *, nb, t):
    hist = CONV_K - 1
    xx_ref[:, 0:hist, :] = s_ref[...]
    u = a_ref[...] * jax.nn.sigmoid(g_ref[...])
    xx_ref[:, hist:hist + t, :] = u.reshape(nb, t, D_CONV)
    st_ref[...] = xx_ref[:, t:t + hist, :]
    for lc in range(D_CONV // CONV_LANE_CHUNK):
        ls = pl.ds(lc * CONV_LANE_CHUNK, CONV_LANE_CHUNK)
        acc = jnp.broadcast_to(b_ref[:, ls], (nb, t, CONV_LANE_CHUNK))
        for j in range(CONV_K):
            acc = acc + w_ref[j:j + 1, ls] * xx_ref[:, j:j + t, ls]
        cv_ref[:, ls] = acc.reshape(nb * t, CONV_LANE_CHUNK)
    act_ref[...] = _ln_silu(cv_ref[...], lng_ref[...], lnb_ref[...]).astype(BF16)


def _conv_sample(proj, state, prev, n, t, w, b, lng, lnb, l, nb=16):
    depth = state.shape[0]
    hist = CONV_K - 1
    rows = nb * t
    xx_rows = -(-(hist + t) // V7X_SUBLANES) * V7X_SUBLANES
    vec = _layer_vec(l, D_CONV)
    st = pl.BlockSpec((None, nb, hist, D_CONV), lambda s: (l, s, 0, 0))
    blocks = [((rows, D_CONV), F32)] * 2 + [((nb, CONV_HALO, D_CONV), F32)] * 2 + [((rows, D_CONV), BF16)]
    return _stacked_call(
        functools.partial(_conv_sample_body, nb=nb, t=t), l, prev, 1,
        grid=(n // nb,),
        in_specs=[pl.BlockSpec((rows, D_CONV), lambda s: (s, COL_GLU_A // D_CONV)),
                  pl.BlockSpec((rows, D_CONV), lambda s: (s, COL_GLU_G // D_CONV)),
                  st, _layer_vec(l, D_CONV, CONV_K), vec, vec, vec],
        out_specs=[st, pl.BlockSpec((rows, D_CONV), lambda s: (s, 0))],
        out_shape=[jax.ShapeDtypeStruct((depth, n, hist, D_CONV), F32),
                   jax.ShapeDtypeStruct((n * t, D_CONV), BF16)],
        scratch_shapes=[pltpu.VMEM((nb, xx_rows, D_CONV), F32), pltpu.VMEM((rows, D_CONV), F32)],
        compiler_params=_params(("parallel",), blocks,
                                resident=[((nb, xx_rows, D_CONV), F32), ((rows, D_CONV), F32)],
                                temps=[((rows, D_CONV), F32)] * 3),
        name="conv_sample",
    )(proj, proj, state, w, _as_rows(b), _as_rows(lng), _as_rows(lnb))


def _softmax_terms(s, valid, distf, slope, sink):
    s = jnp.where(valid, s - slope * distf, NEG)
    m = jnp.maximum(jnp.max(s, axis=-1, keepdims=True), sink)
    p = jnp.exp(s - m)
    denom = jnp.sum(p, axis=-1, keepdims=True) + jnp.exp(sink - m)
    return p, denom


def _attn_prompt_body(sink_ref, q_ref, kc_ref, vc_ref, kp_ref, vp_ref, o_ref, *, l):
    i = pl.program_id(1)
    blk = WINDOW
    q = q_ref[...] * SCALE
    k = jnp.concatenate([kp_ref[...], kc_ref[...]], axis=0).astype(BF16)
    v = jnp.concatenate([vp_ref[...], vc_ref[...]], axis=0).astype(BF16)
    r = lax.broadcasted_iota(jnp.int32, (blk, 2 * blk), 0)
    c = lax.broadcasted_iota(jnp.int32, (blk, 2 * blk), 1)
    dist = blk + r - c
    valid = (dist >= 0) & (dist <= WINDOW) & ((c >= blk) | (i > 0))
    distf = dist.astype(F32)
    for kv in range(N_KV_HEADS):
        kh = k[:, kv * HEAD_DIM:(kv + 1) * HEAD_DIM]
        vh = v[:, kv * HEAD_DIM:(kv + 1) * HEAD_DIM]
        heads = [kv * GROUP + g for g in range(GROUP)]
        qs = jnp.concatenate([q[:, h * HEAD_DIM:(h + 1) * HEAD_DIM] for h in heads], axis=0).astype(BF16)
        s = lax.dot_general(qs, kh, (((1,), (1,)), ((), ())), preferred_element_type=F32)
        ps, ds = [], []
        for g, h in enumerate(heads):
            p, d = _softmax_terms(s[g * blk:(g + 1) * blk], valid, distf, SLOPES[h], sink_ref[l, h])
            ps.append(p.astype(BF16))
            ds.append(d)
        o = _dot(jnp.concatenate(ps, axis=0), vh)
        for g, h in enumerate(heads):
            o_ref[:, h * HEAD_DIM:(h + 1) * HEAD_DIM] = (o[g * blk:(g + 1) * blk] / ds[g]).astype(BF16)


def _attn_prompt(proj, sinks, n, t, l):
    blk = WINDOW
    nb = t // blk
    kcol, vcol = COL_K // D_KV, COL_V // D_KV
    blocks = [((blk, D_ATTN), F32)] + [((blk, D_KV), F32)] * 4 + [((blk, D_ATTN), BF16)]
    return pl.pallas_call(
        functools.partial(_attn_prompt_body, l=l),
        grid=(n, nb),
        in_specs=[pl.BlockSpec(memory_space=pltpu.SMEM),
                  pl.BlockSpec((blk, D_ATTN), lambda s, i: (s * nb + i, COL_Q // D_ATTN)),
                  pl.BlockSpec((blk, D_KV), lambda s, i: (s * nb + i, kcol)),
                  pl.BlockSpec((blk, D_KV), lambda s, i: (s * nb + i, vcol)),
                  pl.BlockSpec((blk, D_KV), lambda s, i: (s * nb + jnp.maximum(i - 1, 0), kcol)),
                  pl.BlockSpec((blk, D_KV), lambda s, i: (s * nb + jnp.maximum(i - 1, 0), vcol))],
        out_specs=pl.BlockSpec((blk, D_ATTN), lambda s, i: (s * nb + i, 0)),
        out_shape=jax.ShapeDtypeStruct((n * t, D_ATTN), BF16),
        compiler_params=_params(("parallel", "parallel"), blocks,
                                temps=[((GROUP * blk, 2 * blk), F32)] * 8),
        name="attn_prompt",
    )(sinks, proj, proj, proj, proj, proj)


def _attn_sample_body(sink_ref, q_ref, kn_ref, vn_ref, ck_ref, cv_ref, nk_ref, nv_ref, o_ref, kk_ref, vv_ref,
                      *, nb, t, l):
    keys = 2 * WINDOW
    for src_new, src_cache, dst, out in ((kn_ref, ck_ref, kk_ref, nk_ref), (vn_ref, cv_ref, vv_ref, nv_ref)):
        dst[:, 0:WINDOW, :] = src_cache[...]
        dst[:, WINDOW:WINDOW + t, :] = src_new[...].reshape(nb, t, D_KV)
        dst[:, WINDOW + t:keys, :] = jnp.zeros((nb, keys - WINDOW - t, D_KV), F32)
        out[...] = dst[:, t:t + WINDOW, :]
    q3 = (q_ref[...] * SCALE).reshape(nb, t, D_ATTN)
    k3 = kk_ref[...].astype(BF16)
    v3 = vv_ref[...].astype(BF16)
    rows = GROUP * t
    r = lax.broadcasted_iota(jnp.int32, (rows, keys), 0)
    c = lax.broadcasted_iota(jnp.int32, (rows, keys), 1)
    dist = WINDOW + r % t - c
    valid = (dist >= 0) & (dist <= WINDOW)
    distf = dist.astype(F32)
    gi = lax.broadcasted_iota(jnp.int32, (rows, 1), 0) // t
    for kv in range(N_KV_HEADS):
        heads = [kv * GROUP + g for g in range(GROUP)]
        slope = jnp.zeros((rows, 1), F32)
        sink = jnp.zeros((rows, 1), F32)
        for g, h in enumerate(heads):
            slope = jnp.where(gi == g, SLOPES[h], slope)
            sink = jnp.where(gi == g, sink_ref[l, h], sink)
        kh = k3[:, :, kv * HEAD_DIM:(kv + 1) * HEAD_DIM]
        vh = v3[:, :, kv * HEAD_DIM:(kv + 1) * HEAD_DIM]
        qs = jnp.concatenate([q3[:, :, h * HEAD_DIM:(h + 1) * HEAD_DIM] for h in heads], axis=1).astype(BF16)
        s = jnp.einsum("bqd,bkd->bqk", qs, kh, preferred_element_type=F32)
        p, d = _softmax_terms(s, valid, distf, slope, sink)
        o = jnp.einsum("bqk,bkd->bqd", p.astype(BF16), vh, preferred_element_type=F32) / d
        for g, h in enumerate(heads):
            o_ref[:, h * HEAD_DIM:(h + 1) * HEAD_DIM] = (
                o[:, g * t:(g + 1) * t, :].reshape(nb * t, HEAD_DIM).astype(BF16))


def _attn_sample(proj, cache_k, cache_v, prev, sinks, n, t, l, nb=8):
    depth = cache_k.shape[0]
    rows = nb * t
    keys = 2 * WINDOW
    kcol, vcol = COL_K // D_KV, COL_V // D_KV
    cache = pl.BlockSpec((None, nb, WINDOW, D_KV), lambda s: (l, s, 0, 0))
    stacked = jax.ShapeDtypeStruct((depth, n, WINDOW, D_KV), F32)
    blocks = ([((rows, D_ATTN), F32)] + [((rows, D_KV), F32)] * 2 + [((nb, WINDOW, D_KV), F32)] * 4
              + [((rows, D_ATTN), BF16)])
    return _stacked_call(
        functools.partial(_attn_sample_body, nb=nb, t=t, l=l), l, prev, 2,
        grid=(n // nb,),
        in_specs=[pl.BlockSpec(memory_space=pltpu.SMEM),
                  pl.BlockSpec((rows, D_ATTN), lambda s: (s, COL_Q // D_ATTN)),
                  pl.BlockSpec((rows, D_KV), lambda s: (s, kcol)),
                  pl.BlockSpec((rows, D_KV), lambda s: (s, vcol)),
                  cache, cache],
        out_specs=[cache, cache, pl.BlockSpec((rows, D_ATTN), lambda s: (s, 0))],
        out_shape=[stacked, stacked, jax.ShapeDtypeStruct((n * t, D_ATTN), BF16)],
        scratch_shapes=[pltpu.VMEM((nb, keys, D_KV), F32)] * 2,
        compiler_params=_params(("parallel",), blocks,
                                resident=[((nb, keys, D_KV), F32)] * 2,
                                temps=[((nb, keys, D_KV), F32)] * 2 + [((nb, GROUP * t, keys), F32)] * 6),
        name="attn_sample",
    )(sinks, proj, proj, proj, cache_k, cache_v)


def _mix_body(c_ref, a_ref, gc_ref, ga_ref, x_ref, wco_ref, wao_ref, wo_ref, n2_ref, x1_ref, h2_ref):
    branch_c = _dot(c_ref[...], wco_ref[...])
    branch_a = _dot(a_ref[...], wao_ref[...])
    merged = jax.nn.sigmoid(gc_ref[...]) * branch_c + jax.nn.sigmoid(ga_ref[...]) * branch_a
    x1 = x_ref[...] + _dot(merged.astype(BF16), wo_ref[...])
    x1_ref[...] = x1
    h2_ref[...] = _rms(x1, n2_ref[...]).astype(BF16)


def _mix(c_act, a_act, proj, x, wco, wao, wo, n2, l, tm=256):
    m = x.shape[0]
    row = lambda width, col: pl.BlockSpec((tm, width), lambda i: (i, col))
    const = lambda k: pl.BlockSpec((None, k, D_MODEL), lambda i: (l, 0, 0), pipeline_mode=pl.Buffered(1))
    blocks = ([((tm, D_CONV), BF16), ((tm, D_ATTN), BF16)] + [((tm, D_MODEL), F32)] * 4 + [((tm, D_MODEL), BF16)])
    weights = [((D_CONV, D_MODEL), BF16), ((D_ATTN, D_MODEL), BF16), ((D_MODEL, D_MODEL), BF16)]
    return pl.pallas_call(
        _mix_body,
        grid=(m // tm,),
        in_specs=[row(D_CONV, 0), row(D_ATTN, 0),
                  row(D_MODEL, COL_GATE_C // D_MODEL), row(D_MODEL, COL_GATE_A // D_MODEL),
                  row(D_MODEL, 0),
                  const(D_CONV), const(D_ATTN), const(D_MODEL),
                  _layer_vec(l, D_MODEL)],
        out_specs=[row(D_MODEL, 0), row(D_MODEL, 0)],
        out_shape=[jax.ShapeDtypeStruct((m, D_MODEL), F32), jax.ShapeDtypeStruct((m, D_MODEL), BF16)],
        compiler_params=_params(("parallel",), blocks, resident=weights, temps=[((tm, D_MODEL), F32)] * 4),
        name="mix",
    )(c_act, a_act, proj, proj, x, wco, wao, wo, _as_rows(n2))


def _ffn_conv(e_ref, w_ref, b_ref, tm):
    first = FFN_HALO - (FFN_K - 1)
    out = b_ref[...]
    for j in range(FFN_K):
        out = out + w_ref[j:j + 1, :] * e_ref[first + j:first + j + tm, :]
    return out


def _up_prompt_body(h_ref, wg_ref, wv_ref, cwg_ref, cwv_ref, cbg_ref, cbv_ref, act_ref, sg_ref, sv_ref,
                    eg_ref, ev_ref, wgb_ref, wvb_ref, *, tm, tiles_per_seq):
    i = pl.program_id(1)
    pos = i % tiles_per_seq

    @pl.when(i == 0)
    def _():
        wgb_ref[...] = wg_ref[...].astype(BF16)
        wvb_ref[...] = wv_ref[...].astype(BF16)

    @pl.when(pos == 0)
    def _():
        eg_ref[0:FFN_HALO, :] = jnp.zeros((FFN_HALO, FFN_TILE), F32)
        ev_ref[0:FFN_HALO, :] = jnp.zeros((FFN_HALO, FFN_TILE), F32)

    h = h_ref[...]
    eg_ref[FFN_HALO:FFN_HALO + tm, :] = _dot(h, wgb_ref[...])
    ev_ref[FFN_HALO:FFN_HALO + tm, :] = _dot(h, wvb_ref[...])
    gate = _ffn_conv(eg_ref, cwg_ref, cbg_ref, tm)
    val = _ffn_conv(ev_ref, cwv_ref, cbv_ref, tm)
    act_ref[...] = (gate * jax.nn.sigmoid(gate) * val).astype(BF16)

    @pl.when(pos == tiles_per_seq - 1)
    def _():
        hist = FFN_K - 1
        sg_ref[0] = eg_ref[FFN_HALO + tm - hist:FFN_HALO + tm, :]
        sv_ref[0] = ev_ref[FFN_HALO + tm - hist:FFN_HALO + tm, :]

    eg_ref[0:FFN_HALO, :] = eg_ref[tm:tm + FFN_HALO, :]
    ev_ref[0:FFN_HALO, :] = ev_ref[tm:tm + FFN_HALO, :]


def _up_prompt(h2, n, t, w_up, cw, cb, l, tm=1024):
    m = n * t
    tps = t // tm
    nt = D_FF // FFN_TILE
    hist = FFN_K - 1
    col = lambda rows, off: pl.BlockSpec((None, rows, FFN_TILE), lambda j, i: (l, 0, j + off))
    blocks = ([((tm, D_MODEL), BF16)] + [((D_MODEL, FFN_TILE), F32)] * 2 + [((tm, FFN_TILE), BF16)])
    scratch = [((FFN_HALO + tm, FFN_TILE), F32)] * 2 + [((D_MODEL, FFN_TILE), BF16)] * 2
    state = pl.BlockSpec((1, hist, FFN_TILE), lambda j, i: (i // tps, 0, j))
    return pl.pallas_call(
        functools.partial(_up_prompt_body, tm=tm, tiles_per_seq=tps),
        grid=(nt, m // tm),
        in_specs=[pl.BlockSpec((tm, D_MODEL), lambda j, i: (i, 0)),
                  col(D_MODEL, 0), col(D_MODEL, nt), col(FFN_K, 0), col(FFN_K, nt), col(1, 0), col(1, nt)],
        out_specs=[pl.BlockSpec((tm, FFN_TILE), lambda j, i: (i, j)), state, state],
        out_shape=[jax.ShapeDtypeStruct((m, D_FF), BF16),
                   jax.ShapeDtypeStruct((n, hist, D_FF), F32), jax.ShapeDtypeStruct((n, hist, D_FF), F32)],
        scratch_shapes=[pltpu.VMEM(s, d) for s, d in scratch],
        compiler_params=_params(("parallel", "arbitrary"), blocks, resident=scratch,
                                temps=[((tm, FFN_TILE), F32)] * 4),
        name="up_prompt",
    )(h2, w_up, w_up, cw, cw, _as_rows(cb), _as_rows(cb))


def _ffn_conv_sample(u3, st_ref, w_ref, b_ref, tpos):
    st0 = st_ref[:, 0:1, :]
    st1 = st_ref[:, 1:2, :]
    prev1 = jnp.where(tpos == 0, st1, pltpu.roll(u3, 1, axis=1))
    prev2 = jnp.where(tpos == 0, st0, jnp.where(tpos == 1, st1, pltpu.roll(u3, 2, axis=1)))
    return b_ref[...] + w_ref[0:1, :] * prev2 + w_ref[1:2, :] * prev1 + w_ref[2:3, :] * u3


def _up_sample_body(h_ref, wg_ref, wv_ref, cwg_ref, cwv_ref, cbg_ref, cbv_ref, stg_ref, stv_ref,
                    act_ref, sg_ref, sv_ref, *, n, t):
    hist = FFN_K - 1
    h = h_ref[...]
    ug = _dot(h, wg_ref[...].astype(BF16)).reshape(n, t, FFN_TILE)
    uv = _dot(h, wv_ref[...].astype(BF16)).reshape(n, t, FFN_TILE)
    tpos = lax.broadcasted_iota(jnp.int32, (1, t, 1), 1)
    gate = _ffn_conv_sample(ug, stg_ref, cwg_ref, cbg_ref, tpos)
    val = _ffn_conv_sample(uv, stv_ref, cwv_ref, cbv_ref, tpos)
    act_ref[...] = (gate * jax.nn.sigmoid(gate) * val).reshape(n * t, FFN_TILE).astype(BF16)
    sg_ref[...] = ug[:, t - hist:t, :]
    sv_ref[...] = uv[:, t - hist:t, :]


def _up_sample(h2, state, n, t, w_up, cw, cb, l):
    assert t == V7X_SUBLANES and FFN_K == 3
    m = n * t
    nt = D_FF // FFN_TILE
    hist = FFN_K - 1
    col = lambda rows, off: pl.BlockSpec((None, rows, FFN_TILE), lambda j: (l, 0, j + off))
    st_in = lambda off: pl.BlockSpec((None, n, hist, FFN_TILE), lambda j: (l, 0, 0, j + off))
    st_out = pl.BlockSpec((n, hist, FFN_TILE), lambda j: (0, 0, j))
    blocks = ([((D_MODEL, FFN_TILE), F32)] * 2 + [((n, V7X_SUBLANES, FFN_TILE), F32)] * 4
              + [((m, FFN_TILE), BF16)])
    return pl.pallas_call(
        functools.partial(_up_sample_body, n=n, t=t),
        grid=(nt,),
        in_specs=[pl.BlockSpec((m, D_MODEL), lambda j: (0, 0)),
                  col(D_MODEL, 0), col(D_MODEL, nt), col(FFN_K, 0), col(FFN_K, nt), col(1, 0), col(1, nt),
                  st_in(0), st_in(nt)],
        out_specs=[pl.BlockSpec((m, FFN_TILE), lambda j: (0, j)), st_out, st_out],
        out_shape=[jax.ShapeDtypeStruct((m, D_FF), BF16),
                   jax.ShapeDtypeStruct((n, hist, D_FF), F32), jax.ShapeDtypeStruct((n, hist, D_FF), F32)],
        compiler_params=_params(("parallel",), blocks,
                                resident=[((m, D_MODEL), BF16)] * 2 + [((D_MODEL, FFN_TILE), BF16)] * 2,
                                temps=[((m, FFN_TILE), F32)] * 8),
        name="up_sample",
    )(h2, w_up, w_up, cw, cw, _as_rows(cb), _as_rows(cb), state, state)


def _down_body(act_ref, w_ref, x_ref, o_ref, wb_ref):
    @pl.when(pl.program_id(1) == 0)
    def _():
        wb_ref[...] = w_ref[...].astype(BF16)

    o_ref[...] = x_ref[...] + _dot(act_ref[...], wb_ref[...])


def _down(act, w_down, x1, l, tm=512):
    m = x1.shape[0]
    blocks = [((tm, D_FF), BF16), ((D_FF, DOWN_TILE), F32), ((tm, DOWN_TILE), F32), ((tm, DOWN_TILE), F32)]
    return pl.pallas_call(
        _down_body,
        grid=(D_MODEL // DOWN_TILE, m // tm),
        in_specs=[pl.BlockSpec((tm, D_FF), lambda j, i: (i, 0)),
                  pl.BlockSpec((None, D_FF, DOWN_TILE), lambda j, i: (l, 0, j)),
                  pl.BlockSpec((tm, DOWN_TILE), lambda j, i: (i, j))],
        out_specs=pl.BlockSpec((tm, DOWN_TILE), lambda j, i: (i, j)),
        out_shape=jax.ShapeDtypeStruct((m, D_MODEL), F32),
        scratch_shapes=[pltpu.VMEM((D_FF, DOWN_TILE), BF16)],
        compiler_params=_params(("parallel", "arbitrary"), blocks, resident=[((D_FF, DOWN_TILE), BF16)],
                                temps=[((tm, DOWN_TILE), F32)]),
        name="down",
    )(act, w_down, x1)


def _final_norm_body(x_ref, g_ref, o_ref):
    o_ref[...] = _rms(x_ref[...], g_ref[...])


def _final_norm(x, g, tm=512):
    m = x.shape[0]
    blocks = [((tm, D_MODEL), F32)] * 2
    return pl.pallas_call(
        _final_norm_body,
        grid=(m // tm,),
        in_specs=[pl.BlockSpec((tm, D_MODEL), lambda i: (i, 0)), pl.BlockSpec((1, D_MODEL), lambda i: (0, 0))],
        out_specs=pl.BlockSpec((tm, D_MODEL), lambda i: (i, 0)),
        out_shape=jax.ShapeDtypeStruct((m, D_MODEL), F32),
        compiler_params=_params(("parallel",), blocks, temps=[((tm, D_MODEL), F32)]),
        name="final_norm",
    )(x, g.reshape(1, D_MODEL))


def kernel(x_prompt, x_sample, cache_k, cache_v, state_conv, state_ffn_conv, norm1_g, w_in, conv_w, conv_b,
           conv_ln_g, conv_ln_b, w_conv_out, attn_sinks, w_attn_out, w_out, norm2_g, w_up, ffn_conv_w,
           ffn_conv_b, w_down, final_norm_g):
    depth = w_in.shape[0]
    n_p, t_p, _ = x_prompt.shape
    n_s, t_s, _ = x_sample.shape
    xp = x_prompt.reshape(n_p * t_p, D_MODEL)
    xs = x_sample.reshape(n_s * t_s, D_MODEL)
    w_co_b, w_ao_b, w_o_b = (w.astype(BF16) for w in (w_conv_out, w_attn_out, w_out))
    cache_k = cache_k.reshape(depth, n_s, WINDOW, D_KV)
    cache_v = cache_v.reshape(depth, n_s, WINDOW, D_KV)
    outs = {k: [] for k in ("kp", "vp", "cp", "fp", "fs")}
    c_states, kv_states = (), ()
    for l in range(depth):
        conv_args = (conv_w, conv_b, conv_ln_g, conv_ln_b, l)
        proj = _inproj(xp, norm1_g, w_in, l, tm=1024)
        c_act, c_state = _conv_prompt(proj, n_p, t_p, *conv_args)
        a_act = _attn_prompt(proj, attn_sinks, n_p, t_p, l)
        x1, h2 = _mix(c_act, a_act, proj, xp, w_co_b, w_ao_b, w_o_b, norm2_g, l)
        act, f_g, f_v = _up_prompt(h2, n_p, t_p, w_up, ffn_conv_w, ffn_conv_b, l)
        xp = _down(act, w_down, x1, l)
        kv = proj.reshape(n_p, t_p, IN_COLS)[:, t_p - WINDOW:, :]
        outs["kp"].append(kv[:, :, COL_K:COL_K + D_KV].reshape(n_p, WINDOW, N_KV_HEADS, HEAD_DIM))
        outs["vp"].append(kv[:, :, COL_V:COL_V + D_KV].reshape(n_p, WINDOW, N_KV_HEADS, HEAD_DIM))
        outs["cp"].append(c_state)
        outs["fp"].append(jnp.concatenate([f_g, f_v], axis=-1))
        proj = _inproj(xs, norm1_g, w_in, l, tm=n_s * t_s)
        c_stack, c_act = _conv_sample(proj, state_conv, c_states, n_s, t_s, *conv_args)
        k_stack, v_stack, a_act = _attn_sample(proj, cache_k, cache_v, kv_states, attn_sinks, n_s, t_s, l)
        c_states, kv_states = (c_stack,), (k_stack, v_stack)
        x1, h2 = _mix(c_act, a_act, proj, xs, w_co_b, w_ao_b, w_o_b, norm2_g, l)
        act, f_g, f_v = _up_sample(h2, state_ffn_conv, n_s, t_s, w_up, ffn_conv_w, ffn_conv_b, l)
        xs = _down(act, w_down, x1, l)
        outs["fs"].append(jnp.concatenate([f_g, f_v], axis=-1))
    y_prompt = _final_norm(xp, final_norm_g).reshape(n_p, t_p, D_MODEL)
    y_sample = _final_norm(xs, final_norm_g).reshape(n_s, t_s, D_MODEL)
    kv_shape = (depth, n_s, WINDOW, N_KV_HEADS, HEAD_DIM)
    return (y_prompt, y_sample, jnp.stack(outs["kp"]), jnp.stack(outs["vp"]), jnp.stack(outs["cp"]),
            jnp.stack(outs["fp"]), k_stack.reshape(kv_shape), v_stack.reshape(kv_shape), c_stack,
            jnp.stack(outs["fs"]))
```

```python
import functools

import jax
import jax.numpy as jnp
from jax import lax
from jax.experimental import pallas as pl
from jax.experimental.pallas import tpu as pltpu

F32 = jnp.float32
BF16 = jnp.bfloat16

D_MODEL = 2048
HEAD_DIM = 64
N_HEADS = 16
N_KV_HEADS = 4
GROUP = N_HEADS // N_KV_HEADS
D_ATTN = N_HEADS * HEAD_DIM
D_KV = N_KV_HEADS * HEAD_DIM
WINDOW = 128
D_CONV = D_MODEL // 2
CONV_K = 31
D_FF = 3 * D_MODEL
FFN_K = 3
EPS = 1e-6
IN_COLS = 2 * D_CONV + D_ATTN + 2 * D_KV + 2 * D_MODEL
NEG = -1e30
SCALE = HEAD_DIM ** -0.5
SLOPES = tuple(2.0 ** (-8.0 * (h + 1) / N_HEADS) for h in range(N_HEADS))

V7X_SUBLANES = 8
V7X_SCOPED_VMEM_CAP_BYTES = 60000 * 1024

IN_TILE = 512
COL_GLU_A = 0
COL_GLU_G = D_CONV
COL_GATE_C = 2 * D_CONV
COL_GATE_A = COL_GATE_C + D_MODEL
COL_Q = COL_GATE_A + D_MODEL
COL_K = COL_Q + D_ATTN
COL_V = COL_K + D_KV

CONV_HALO = 32
CONV_ROW_CHUNK = 64
CONV_LANE_CHUNK = 256
FFN_HALO = V7X_SUBLANES
FFN_TILE = 512
DOWN_TILE = 512


def _nbytes(shape, dtype):
    n = 1
    for s in shape:
        n *= s
    return n * jnp.dtype(dtype).itemsize


def _params(semantics, pipelined, resident=(), temps=()):
    est = 2 * sum(_nbytes(s, d) for s, d in pipelined)
    est += sum(_nbytes(s, d) for s, d in resident)
    est += sum(_nbytes(s, d) for s, d in temps)
    limit = min(V7X_SCOPED_VMEM_CAP_BYTES, est + est // 4)
    return pltpu.CompilerParams(dimension_semantics=semantics, vmem_limit_bytes=limit)


def _layer_vec(l, width, rows=1):
    return pl.BlockSpec((None, rows, width), lambda *_: (l, 0, 0))


def _as_rows(p):
    return p.reshape(p.shape[0], 1, p.shape[1])


def _rms(x, g):
    return x * lax.rsqrt(jnp.mean(x * x, axis=-1, keepdims=True) + EPS) * g


def _dot(a, b):
    return jnp.dot(a, b, preferred_element_type=F32)


def _skip_first_ref(body):
    def wrapped(_, *refs):
        body(*refs)
    return wrapped


def _stacked_call(body, l, prev, n_alias_out, **kw):
    if l == 0:
        return pl.pallas_call(body, **kw)
    for _ in prev:
        body = _skip_first_ref(body)
    kw["in_specs"] = [pl.BlockSpec(memory_space=pl.ANY)] * len(prev) + list(kw["in_specs"])
    call = pl.pallas_call(body, input_output_aliases={i: i for i in range(n_alias_out)}, **kw)
    return lambda *args: call(*prev, *args)


def _inproj_body(x_ref, g_ref, w_ref, o_ref, h_ref):
    @pl.when(pl.program_id(1) == 0)
    def _():
        h_ref[...] = _rms(x_ref[...], g_ref[...]).astype(BF16)

    o_ref[...] = _dot(h_ref[...], w_ref[...].astype(BF16)).astype(BF16)


def _w_in_tile(j):
    n_glu = 2 * D_CONV // IN_TILE
    n_qkv = (D_ATTN + 2 * D_KV) // IN_TILE
    n_gate = 2 * D_MODEL // IN_TILE
    return jnp.where(j < n_glu, j, jnp.where(j < n_glu + n_gate, j + n_qkv, j - n_gate))


def _inproj(x, g, w, l, tm):
    m = x.shape[0]
    blocks = [((tm, D_MODEL), F32), ((D_MODEL, IN_TILE), F32), ((tm, IN_TILE), F32)]
    return pl.pallas_call(
        _inproj_body,
        grid=(m // tm, IN_COLS // IN_TILE),
        in_specs=[pl.BlockSpec((tm, D_MODEL), lambda i, j: (i, 0)),
                  _layer_vec(l, D_MODEL),
                  pl.BlockSpec((None, D_MODEL, IN_TILE), lambda i, j: (l, 0, _w_in_tile(j)))],
        out_specs=pl.BlockSpec((tm, IN_TILE), lambda i, j: (i, j)),
        out_shape=jax.ShapeDtypeStruct((m, IN_COLS), BF16),
        scratch_shapes=[pltpu.VMEM((tm, D_MODEL), BF16)],
        compiler_params=_params(("parallel", "arbitrary"), blocks,
                                resident=[((tm, D_MODEL), BF16), ((D_MODEL, IN_TILE), BF16)],
                                temps=[((tm, D_MODEL), F32)]),
        name="inproj",
    )(x, _as_rows(g), w)


def _ln_silu(c, g, b):
    mu = jnp.mean(c, axis=-1, keepdims=True)
    xc = c - mu
    y = xc * lax.rsqrt(jnp.mean(xc * xc, axis=-1, keepdims=True) + EPS) * g + b
    return y * jax.nn.sigmoid(y)


def _conv_prompt_body(a_ref, g_ref, w_ref, b_ref, lng_ref, lnb_ref, act_ref, st_ref, xx_ref, cv_ref, sh_ref,
                      *, tt):
    i = pl.program_id(1)

    @pl.when(i == 0)
    def _():
        xx_ref[0:CONV_HALO, :] = jnp.zeros((CONV_HALO, D_CONV), F32)

    xx_ref[CONV_HALO:CONV_HALO + tt, :] = a_ref[...].astype(F32) * jax.nn.sigmoid(g_ref[...].astype(F32))

    first = CONV_HALO - (CONV_K - 1)
    sh_rows = sh_ref.shape[1]
    for lc in range(D_CONV // CONV_LANE_CHUNK):
        ls = pl.ds(lc * CONV_LANE_CHUNK, CONV_LANE_CHUNK)
        for r in range(1, V7X_SUBLANES):
            sh_ref[r - 1] = xx_ref[r:r + sh_rows, ls]
        for r0 in range(0, tt, CONV_ROW_CHUNK):
            acc = jnp.broadcast_to(b_ref[:, ls], (CONV_ROW_CHUNK, CONV_LANE_CHUNK))
            for j in range(CONV_K):
                q, r = divmod(first + j, V7X_SUBLANES)
                a0 = r0 + V7X_SUBLANES * q
                if r == 0:
                    window = xx_ref[a0:a0 + CONV_ROW_CHUNK, ls]
                else:
                    window = sh_ref[r - 1, a0:a0 + CONV_ROW_CHUNK, :]
                acc = acc + w_ref[j:j + 1, ls] * window
            cv_ref[r0:r0 + CONV_ROW_CHUNK, ls] = acc
    act_ref[...] = _ln_silu(cv_ref[...], lng_ref[...], lnb_ref[...]).astype(BF16)

    @pl.when(i == pl.num_programs(1) - 1)
    def _():
        st_ref[0] = xx_ref[tt + first:tt + CONV_HALO, :]

    xx_ref[0:CONV_HALO, :] = xx_ref[tt:tt + CONV_HALO, :]


def _conv_prompt(proj, n, t, w, b, lng, lnb, l, tt=512):
    tps = t // tt
    vec = _layer_vec(l, D_CONV)
    blocks = [((tt, D_CONV), BF16)] * 3
    shifted = (V7X_SUBLANES - 1, tt + CONV_HALO - V7X_SUBLANES, CONV_LANE_CHUNK)
    return pl.pallas_call(
        functools.partial(_conv_prompt_body, tt=tt),
        grid=(n, tps),
        in_specs=[pl.BlockSpec((tt, D_CONV), lambda s, i: (s * tps + i, COL_GLU_A // D_CONV)),
                  pl.BlockSpec((tt, D_CONV), lambda s, i: (s * tps + i, COL_GLU_G // D_CONV)),
                  _layer_vec(l, D_CONV, CONV_K), vec, vec, vec],
        out_specs=[pl.BlockSpec((tt, D_CONV), lambda s, i: (s * tps + i, 0)),
                   pl.BlockSpec((1, CONV_K - 1, D_CONV), lambda s, i: (s, 0, 0))],
        out_shape=[jax.ShapeDtypeStruct((n * t, D_CONV), BF16),
                   jax.ShapeDtypeStruct((n, CONV_K - 1, D_CONV), F32)],
        scratch_shapes=[pltpu.VMEM((CONV_HALO + tt, D_CONV), F32), pltpu.VMEM((tt, D_CONV), F32),
                        pltpu.VMEM(shifted, F32)],
        compiler_params=_params(("parallel", "arbitrary"), blocks,
                                resident=[((CONV_HALO + tt, D_CONV), F32), ((tt, D_CONV), F32), (shifted, F32)],
                                temps=[((tt, D_CONV), F32)] * 4),
        name="conv_prompt",
    )(proj, proj, w, _as_rows(b), _as_rows(lng), _as_rows(lnb))


def _conv_sample_body(a_ref, g_ref, s_ref, w_ref, b_ref, lng_ref, lnb_ref, st_ref, act_ref, xx_ref, cv_ref,
                      *, nb, t):
    hist = CONV_K - 1
    xx_ref[:, 0:hist, :] = s_ref[...]
    u = a_ref[...].astype(F32) * jax.nn.sigmoid(g_ref[...].astype(F32))
    xx_ref[:, hist:hist + t, :] = u.reshape(nb, t, D_CONV)
    st_ref[...] = xx_ref[:, t:t + hist, :]
    for lc in range(D_CONV // CONV_LANE_CHUNK):
        ls = pl.ds(lc * CONV_LANE_CHUNK, CONV_LANE_CHUNK)
        acc = jnp.broadcast_to(b_ref[:, ls], (nb, t, CONV_LANE_CHUNK))
        for j in range(CONV_K):
            acc = acc + w_ref[j:j + 1, ls] * xx_ref[:, j:j + t, ls]
        cv_ref[:, ls] = acc.reshape(nb * t, CONV_LANE_CHUNK)
    act_ref[...] = _ln_silu(cv_ref[...], lng_ref[...], lnb_ref[...]).astype(BF16)


def _conv_sample(proj, state, prev, n, t, w, b, lng, lnb, l, nb=16):
    depth = state.shape[0]
    hist = CONV_K - 1
    rows = nb * t
    xx_rows = -(-(hist + t) // V7X_SUBLANES) * V7X_SUBLANES
    vec = _layer_vec(l, D_CONV)
    st = pl.BlockSpec((None, nb, hist, D_CONV), lambda s: (l, s, 0, 0))
    blocks = [((rows, D_CONV), BF16)] * 3 + [((nb, CONV_HALO, D_CONV), F32)] * 2
    return _stacked_call(
        functools.partial(_conv_sample_body, nb=nb, t=t), l, prev, 1,
        grid=(n // nb,),
        in_specs=[pl.BlockSpec((rows, D_CONV), lambda s: (s, COL_GLU_A // D_CONV)),
                  pl.BlockSpec((rows, D_CONV), lambda s: (s, COL_GLU_G // D_CONV)),
                  st, _layer_vec(l, D_CONV, CONV_K), vec, vec, vec],
        out_specs=[st, pl.BlockSpec((rows, D_CONV), lambda s: (s, 0))],
        out_shape=[jax.ShapeDtypeStruct((depth, n, hist, D_CONV), F32),
                   jax.ShapeDtypeStruct((n * t, D_CONV), BF16)],
        scratch_shapes=[pltpu.VMEM((nb, xx_rows, D_CONV), F32), pltpu.VMEM((rows, D_CONV), F32)],
        compiler_params=_params(("parallel",), blocks,
                                resident=[((nb, xx_rows, D_CONV), F32), ((rows, D_CONV), F32)],
                                temps=[((rows, D_CONV), F32)] * 3),
        name="conv_sample",
    )(proj, proj, state, w, _as_rows(b), _as_rows(lng), _as_rows(lnb))


def _softmax_terms(s, valid, distf, slope, sink):
    s = jnp.where(valid, s - slope * distf, NEG)
    m = jnp.maximum(jnp.max(s, axis=-1, keepdims=True), sink)
    p = jnp.exp(s - m)
    denom = jnp.sum(p, axis=-1, keepdims=True) + jnp.exp(sink - m)
    return p, denom


def _attn_prompt_body(sink_ref, q_ref, kc_ref, vc_ref, kp_ref, vp_ref, o_ref, *, l, nsub):
    i = pl.program_id(1)
    blk = WINDOW
    q_all = q_ref[...] * SCALE
    k_all = jnp.concatenate([kp_ref[...], kc_ref[...]], axis=0)
    v_all = jnp.concatenate([vp_ref[...], vc_ref[...]], axis=0)
    r = lax.broadcasted_iota(jnp.int32, (blk, 2 * blk), 0)
    c = lax.broadcasted_iota(jnp.int32, (blk, 2 * blk), 1)
    dist = blk + r - c
    in_window = (dist >= 0) & (dist <= WINDOW)
    distf = dist.astype(F32)
    for sb in range(nsub):
        q = q_all[sb * blk:(sb + 1) * blk]
        k = k_all[sb * blk:(sb + 2) * blk]
        v = v_all[sb * blk:(sb + 2) * blk]
        valid = in_window & ((c >= blk) | (i > 0)) if sb == 0 else in_window
        for kv in range(N_KV_HEADS):
            kh = k[:, kv * HEAD_DIM:(kv + 1) * HEAD_DIM]
            vh = v[:, kv * HEAD_DIM:(kv + 1) * HEAD_DIM]
            heads = [kv * GROUP + g for g in range(GROUP)]
            qs = jnp.concatenate([q[:, h * HEAD_DIM:(h + 1) * HEAD_DIM] for h in heads], axis=0)
            s = lax.dot_general(qs, kh, (((1,), (1,)), ((), ())), preferred_element_type=F32)
            ps, ds = [], []
            for g, h in enumerate(heads):
                p, d = _softmax_terms(s[g * blk:(g + 1) * blk], valid, distf, SLOPES[h], sink_ref[l, h])
                ps.append(p.astype(BF16))
                ds.append(d)
            o = _dot(jnp.concatenate(ps, axis=0), vh)
            for g, h in enumerate(heads):
                o_ref[sb * blk:(sb + 1) * blk, h * HEAD_DIM:(h + 1) * HEAD_DIM] = (
                    o[g * blk:(g + 1) * blk] / ds[g]).astype(BF16)


def _attn_prompt(proj, sinks, n, t, l, nsub=1):
    blk = WINDOW
    rows = nsub * blk
    steps = t // rows
    kcol, vcol = COL_K // D_KV, COL_V // D_KV
    prev = lambda s, i: s * (t // blk) + jnp.maximum(nsub * i - 1, 0)
    blocks = ([((rows, D_ATTN), BF16)] * 2 + [((rows, D_KV), BF16)] * 2 + [((blk, D_KV), BF16)] * 2)
    return pl.pallas_call(
        functools.partial(_attn_prompt_body, l=l, nsub=nsub),
        grid=(n, steps),
        in_specs=[pl.BlockSpec(memory_space=pltpu.SMEM),
                  pl.BlockSpec((rows, D_ATTN), lambda s, i: (s * steps + i, COL_Q // D_ATTN)),
                  pl.BlockSpec((rows, D_KV), lambda s, i: (s * steps + i, kcol)),
                  pl.BlockSpec((rows, D_KV), lambda s, i: (s * steps + i, vcol)),
                  pl.BlockSpec((blk, D_KV), lambda s, i: (prev(s, i), kcol)),
                  pl.BlockSpec((blk, D_KV), lambda s, i: (prev(s, i), vcol))],
        out_specs=pl.BlockSpec((rows, D_ATTN), lambda s, i: (s * steps + i, 0)),
        out_shape=jax.ShapeDtypeStruct((n * t, D_ATTN), BF16),
        compiler_params=_params(("parallel", "parallel"), blocks,
                                temps=[((GROUP * blk, 2 * blk), F32)] * 16),
        name="attn_prompt",
    )(sinks, proj, proj, proj, proj, proj)


def _attn_sample_body(sink_ref, q_ref, kn_ref, vn_ref, ck_ref, cv_ref, nk_ref, nv_ref, o_ref, kk_ref, vv_ref,
                      *, nb, t, l):
    keys = 2 * WINDOW
    for src_new, src_cache, dst, out in ((kn_ref, ck_ref, kk_ref, nk_ref), (vn_ref, cv_ref, vv_ref, nv_ref)):
        dst[:, 0:WINDOW, :] = src_cache[...]
        dst[:, WINDOW:WINDOW + t, :] = src_new[...].astype(F32).reshape(nb, t, D_KV)
        dst[:, WINDOW + t:keys, :] = jnp.zeros((nb, keys - WINDOW - t, D_KV), F32)
        out[...] = dst[:, t:t + WINDOW, :]
    q3 = (q_ref[...].astype(F32) * SCALE).reshape(nb, t, D_ATTN)
    k3 = kk_ref[...].astype(BF16)
    v3 = vv_ref[...].astype(BF16)
    rows = GROUP * t
    r = lax.broadcasted_iota(jnp.int32, (rows, keys), 0)
    c = lax.broadcasted_iota(jnp.int32, (rows, keys), 1)
    dist = WINDOW + r % t - c
    valid = (dist >= 0) & (dist <= WINDOW)
    distf = dist.astype(F32)
    gi = lax.broadcasted_iota(jnp.int32, (rows, 1), 0) // t
    for kv in range(N_KV_HEADS):
        heads = [kv * GROUP + g for g in range(GROUP)]
        slope = jnp.zeros((rows, 1), F32)
        sink = jnp.zeros((rows, 1), F32)
        for g, h in enumerate(heads):
            slope = jnp.where(gi == g, SLOPES[h], slope)
            sink = jnp.where(gi == g, sink_ref[l, h], sink)
        kh = k3[:, :, kv * HEAD_DIM:(kv + 1) * HEAD_DIM]
        vh = v3[:, :, kv * HEAD_DIM:(kv + 1) * HEAD_DIM]
        qs = jnp.concatenate([q3[:, :, h * HEAD_DIM:(h + 1) * HEAD_DIM] for h in heads], axis=1).astype(BF16)
        s = jnp.einsum("bqd,bkd->bqk", qs, kh, preferred_element_type=F32)
        p, d = _softmax_terms(s, valid, distf, slope, sink)
        o = jnp.einsum("bqk,bkd->bqd", p.astype(BF16), vh, preferred_element_type=F32) / d
        for g, h in enumerate(heads):
            o_ref[:, h * HEAD_DIM:(h + 1) * HEAD_DIM] = (
                o[:, g * t:(g + 1) * t, :].reshape(nb * t, HEAD_DIM).astype(BF16))


def _attn_sample(proj, cache_k, cache_v, prev, sinks, n, t, l, nb=8):
    depth = cache_k.shape[0]
    rows = nb * t
    keys = 2 * WINDOW
    kcol, vcol = COL_K // D_KV, COL_V // D_KV
    cache = pl.BlockSpec((None, nb, WINDOW, D_KV), lambda s: (l, s, 0, 0))
    stacked = jax.ShapeDtypeStruct((depth, n, WINDOW, D_KV), F32)
    blocks = [((rows, D_ATTN), BF16)] * 2 + [((rows, D_KV), BF16)] * 2 + [((nb, WINDOW, D_KV), F32)] * 4
    return _stacked_call(
        functools.partial(_attn_sample_body, nb=nb, t=t, l=l), l, prev, 2,
        grid=(n // nb,),
        in_specs=[pl.BlockSpec(memory_space=pltpu.SMEM),
                  pl.BlockSpec((rows, D_ATTN), lambda s: (s, COL_Q // D_ATTN)),
                  pl.BlockSpec((rows, D_KV), lambda s: (s, kcol)),
                  pl.BlockSpec((rows, D_KV), lambda s: (s, vcol)),
                  cache, cache],
        out_specs=[cache, cache, pl.BlockSpec((rows, D_ATTN), lambda s: (s, 0))],
        out_shape=[stacked, stacked, jax.ShapeDtypeStruct((n * t, D_ATTN), BF16)],
        scratch_shapes=[pltpu.VMEM((nb, keys, D_KV), F32)] * 2,
        compiler_params=_params(("parallel",), blocks,
                                resident=[((nb, keys, D_KV), F32)] * 2,
                                temps=[((nb, keys, D_KV), F32)] * 2 + [((nb, GROUP * t, keys), F32)] * 6),
        name="attn_sample",
    )(sinks, proj, proj, proj, cache_k, cache_v)


def _mix_body(c_ref, a_ref, gc_ref, ga_ref, x_ref, wco_ref, wao_ref, wo_ref, n2_ref, x1_ref, h2_ref):
    branch_c = _dot(c_ref[...], wco_ref[...])
    branch_a = _dot(a_ref[...], wao_ref[...])
    merged = (jax.nn.sigmoid(gc_ref[...].astype(F32)) * branch_c
              + jax.nn.sigmoid(ga_ref[...].astype(F32)) * branch_a)
    x1 = x_ref[...] + _dot(merged.astype(BF16), wo_ref[...])
    x1_ref[...] = x1
    h2_ref[...] = _rms(x1, n2_ref[...]).astype(BF16)


def _mix(c_act, a_act, proj, x, wco, wao, wo, n2, l, tm=256):
    m = x.shape[0]
    row = lambda width, col: pl.BlockSpec((tm, width), lambda i: (i, col))
    const = lambda k: pl.BlockSpec((None, k, D_MODEL), lambda i: (l, 0, 0), pipeline_mode=pl.Buffered(1))
    blocks = ([((tm, D_CONV), BF16), ((tm, D_ATTN), BF16)] + [((tm, D_MODEL), F32)] * 2 + [((tm, D_MODEL), BF16)] * 3)
    weights = [((D_CONV, D_MODEL), BF16), ((D_ATTN, D_MODEL), BF16), ((D_MODEL, D_MODEL), BF16)]
    return pl.pallas_call(
        _mix_body,
        grid=(m // tm,),
        in_specs=[row(D_CONV, 0), row(D_ATTN, 0),
                  row(D_MODEL, COL_GATE_C // D_MODEL), row(D_MODEL, COL_GATE_A // D_MODEL),
                  row(D_MODEL, 0),
                  const(D_CONV), const(D_ATTN), const(D_MODEL),
                  _layer_vec(l, D_MODEL)],
        out_specs=[row(D_MODEL, 0), row(D_MODEL, 0)],
        out_shape=[jax.ShapeDtypeStruct((m, D_MODEL), F32), jax.ShapeDtypeStruct((m, D_MODEL), BF16)],
        compiler_params=_params(("parallel",), blocks, resident=weights, temps=[((tm, D_MODEL), F32)] * 4),
        name="mix",
    )(c_act, a_act, proj, proj, x, wco, wao, wo, _as_rows(n2))


def _ffn_conv(e_ref, w_ref, b_ref, tm):
    first = FFN_HALO - (FFN_K - 1)
    out = b_ref[...]
    for j in range(FFN_K):
        out = out + w_ref[j:j + 1, :] * e_ref[first + j:first + j + tm, :]
    return out


def _up_prompt_body(h_ref, wg_ref, wv_ref, cwg_ref, cwv_ref, cbg_ref, cbv_ref, act_ref, sg_ref, sv_ref,
                    eg_ref, ev_ref, wgb_ref, wvb_ref, *, tm, tiles_per_seq):
    i = pl.program_id(1)
    pos = i % tiles_per_seq

    @pl.when(i == 0)
    def _():
        wgb_ref[...] = wg_ref[...].astype(BF16)
        wvb_ref[...] = wv_ref[...].astype(BF16)

    @pl.when(pos == 0)
    def _():
        eg_ref[0:FFN_HALO, :] = jnp.zeros((FFN_HALO, FFN_TILE), F32)
        ev_ref[0:FFN_HALO, :] = jnp.zeros((FFN_HALO, FFN_TILE), F32)

    h = h_ref[...]
    eg_ref[FFN_HALO:FFN_HALO + tm, :] = _dot(h, wgb_ref[...])
    ev_ref[FFN_HALO:FFN_HALO + tm, :] = _dot(h, wvb_ref[...])
    gate = _ffn_conv(eg_ref, cwg_ref, cbg_ref, tm)
    val = _ffn_conv(ev_ref, cwv_ref, cbv_ref, tm)
    act_ref[...] = (gate * jax.nn.sigmoid(gate) * val).astype(BF16)

    @pl.when(pos == tiles_per_seq - 1)
    def _():
        hist = FFN_K - 1
        sg_ref[0] = eg_ref[FFN_HALO + tm - hist:FFN_HALO + tm, :]
        sv_ref[0] = ev_ref[FFN_HALO + tm - hist:FFN_HALO + tm, :]

    eg_ref[0:FFN_HALO, :] = eg_ref[tm:tm + FFN_HALO, :]
    ev_ref[0:FFN_HALO, :] = ev_ref[tm:tm + FFN_HALO, :]


def _up_prompt(h2, n, t, w_up, cw, cb, l, tm=1024):
    m = n * t
    tps = t // tm
    nt = D_FF // FFN_TILE
    hist = FFN_K - 1
    col = lambda rows, off: pl.BlockSpec((None, rows, FFN_TILE), lambda j, i: (l, 0, j + off))
    blocks = ([((tm, D_MODEL), BF16)] + [((D_MODEL, FFN_TILE), F32)] * 2 + [((tm, FFN_TILE), BF16)])
    scratch = [((FFN_HALO + tm, FFN_TILE), F32)] * 2 + [((D_MODEL, FFN_TILE), BF16)] * 2
    state = pl.BlockSpec((1, hist, FFN_TILE), lambda j, i: (i // tps, 0, j))
    return pl.pallas_call(
        functools.partial(_up_prompt_body, tm=tm, tiles_per_seq=tps),
        grid=(nt, m // tm),
        in_specs=[pl.BlockSpec((tm, D_MODEL), lambda j, i: (i, 0)),
                  col(D_MODEL, 0), col(D_MODEL, nt), col(FFN_K, 0), col(FFN_K, nt), col(1, 0), col(1, nt)],
        out_specs=[pl.BlockSpec((tm, FFN_TILE), lambda j, i: (i, j)), state, state],
        out_shape=[jax.ShapeDtypeStruct((m, D_FF), BF16),
                   jax.ShapeDtypeStruct((n, hist, D_FF), F32), jax.ShapeDtypeStruct((n, hist, D_FF), F32)],
        scratch_shapes=[pltpu.VMEM(s, d) for s, d in scratch],
        compiler_params=_params(("parallel", "arbitrary"), blocks, resident=scratch,
                                temps=[((tm, FFN_TILE), F32)] * 4),
        name="up_prompt",
    )(h2, w_up, w_up, cw, cw, _as_rows(cb), _as_rows(cb))


def _ffn_conv_sample(u3, st_ref, w_ref, b_ref, tpos):
    st0 = st_ref[:, 0:1, :]
    st1 = st_ref[:, 1:2, :]
    prev1 = jnp.where(tpos == 0, st1, pltpu.roll(u3, 1, axis=1))
    prev2 = jnp.where(tpos == 0, st0, jnp.where(tpos == 1, st1, pltpu.roll(u3, 2, axis=1)))
    return b_ref[...] + w_ref[0:1, :] * prev2 + w_ref[1:2, :] * prev1 + w_ref[2:3, :] * u3


def _up_sample_body(h_ref, wg_ref, wv_ref, cwg_ref, cwv_ref, cbg_ref, cbv_ref, stg_ref, stv_ref,
                    act_ref, sg_ref, sv_ref, *, n, t):
    hist = FFN_K - 1
    h = h_ref[...]
    ug = _dot(h, wg_ref[...].astype(BF16)).reshape(n, t, FFN_TILE)
    uv = _dot(h, wv_ref[...].astype(BF16)).reshape(n, t, FFN_TILE)
    tpos = lax.broadcasted_iota(jnp.int32, (1, t, 1), 1)
    gate = _ffn_conv_sample(ug, stg_ref, cwg_ref, cbg_ref, tpos)
    val = _ffn_conv_sample(uv, stv_ref, cwv_ref, cbv_ref, tpos)
    act_ref[...] = (gate * jax.nn.sigmoid(gate) * val).reshape(n * t, FFN_TILE).astype(BF16)
    sg_ref[...] = ug[:, t - hist:t, :]
    sv_ref[...] = uv[:, t - hist:t, :]


def _up_sample(h2, state, n, t, w_up, cw, cb, l):
    assert t == V7X_SUBLANES and FFN_K == 3
    m = n * t
    nt = D_FF // FFN_TILE
    hist = FFN_K - 1
    col = lambda rows, off: pl.BlockSpec((None, rows, FFN_TILE), lambda j: (l, 0, j + off))
    st_in = lambda off: pl.BlockSpec((None, n, hist, FFN_TILE), lambda j: (l, 0, 0, j + off))
    st_out = pl.BlockSpec((n, hist, FFN_TILE), lambda j: (0, 0, j))
    blocks = ([((D_MODEL, FFN_TILE), F32)] * 2 + [((n, V7X_SUBLANES, FFN_TILE), F32)] * 4
              + [((m, FFN_TILE), BF16)])
    return pl.pallas_call(
        functools.partial(_up_sample_body, n=n, t=t),
        grid=(nt,),
        in_specs=[pl.BlockSpec((m, D_MODEL), lambda j: (0, 0)),
                  col(D_MODEL, 0), col(D_MODEL, nt), col(FFN_K, 0), col(FFN_K, nt), col(1, 0), col(1, nt),
                  st_in(0), st_in(nt)],
        out_specs=[pl.BlockSpec((m, FFN_TILE), lambda j: (0, j)), st_out, st_out],
        out_shape=[jax.ShapeDtypeStruct((m, D_FF), BF16),
                   jax.ShapeDtypeStruct((n, hist, D_FF), F32), jax.ShapeDtypeStruct((n, hist, D_FF), F32)],
        compiler_params=_params(("parallel",), blocks,
                                resident=[((m, D_MODEL), BF16)] * 2 + [((D_MODEL, FFN_TILE), BF16)] * 2,
                                temps=[((m, FFN_TILE), F32)] * 8),
        name="up_sample",
    )(h2, w_up, w_up, cw, cw, _as_rows(cb), _as_rows(cb), state, state)


def _down_body(act_ref, w_ref, x_ref, o_ref, wb_ref):
    @pl.when(pl.program_id(1) == 0)
    def _():
        wb_ref[...] = w_ref[...].astype(BF16)

    o_ref[...] = x_ref[...] + _dot(act_ref[...], wb_ref[...])


def _down(act, w_down, x1, l, tm=512):
    m = x1.shape[0]
    blocks = [((tm, D_FF), BF16), ((D_FF, DOWN_TILE), F32), ((tm, DOWN_TILE), F32), ((tm, DOWN_TILE), F32)]
    return pl.pallas_call(
        _down_body,
        grid=(D_MODEL // DOWN_TILE, m // tm),
        in_specs=[pl.BlockSpec((tm, D_FF), lambda j, i: (i, 0)),
                  pl.BlockSpec((None, D_FF, DOWN_TILE), lambda j, i: (l, 0, j)),
                  pl.BlockSpec((tm, DOWN_TILE), lambda j, i: (i, j))],
        out_specs=pl.BlockSpec((tm, DOWN_TILE), lambda j, i: (i, j)),
        out_shape=jax.ShapeDtypeStruct((m, D_MODEL), F32),
        scratch_shapes=[pltpu.VMEM((D_FF, DOWN_TILE), BF16)],
        compiler_params=_params(("parallel", "arbitrary"), blocks, resident=[((D_FF, DOWN_TILE), BF16)],
                                temps=[((tm, DOWN_TILE), F32)]),
        name="down",
    )(act, w_down, x1)


def _final_norm_body(x_ref, g_ref, o_ref):
    o_ref[...] = _rms(x_ref[...], g_ref[...])


def _final_norm(x, g, tm=512):
    m = x.shape[0]
    blocks = [((tm, D_MODEL), F32)] * 2
    return pl.pallas_call(
        _final_norm_body,
        grid=(m // tm,),
        in_specs=[pl.BlockSpec((tm, D_MODEL), lambda i: (i, 0)), pl.BlockSpec((1, D_MODEL), lambda i: (0, 0))],
        out_specs=pl.BlockSpec((tm, D_MODEL), lambda i: (i, 0)),
        out_shape=jax.ShapeDtypeStruct((m, D_MODEL), F32),
        compiler_params=_params(("parallel",), blocks, temps=[((tm, D_MODEL), F32)]),
        name="final_norm",
    )(x, g.reshape(1, D_MODEL))


def kernel(x_prompt, x_sample, cache_k, cache_v, state_conv, state_ffn_conv, norm1_g, w_in, conv_w, conv_b,
           conv_ln_g, conv_ln_b, w_conv_out, attn_sinks, w_attn_out, w_out, norm2_g, w_up, ffn_conv_w,
           ffn_conv_b, w_down, final_norm_g):
    depth = w_in.shape[0]
    n_p, t_p, _ = x_prompt.shape
    n_s, t_s, _ = x_sample.shape
    xp = x_prompt.reshape(n_p * t_p, D_MODEL)
    xs = x_sample.reshape(n_s * t_s, D_MODEL)
    w_co_b, w_ao_b, w_o_b = (w.astype(BF16) for w in (w_conv_out, w_attn_out, w_out))
    cache_k = cache_k.reshape(depth, n_s, WINDOW, D_KV)
    cache_v = cache_v.reshape(depth, n_s, WINDOW, D_KV)
    outs = {k: [] for k in ("kp", "vp", "cp", "fp", "fs")}
    c_states, kv_states = (), ()
    for l in range(depth):
        conv_args = (conv_w, conv_b, conv_ln_g, conv_ln_b, l)
        proj = _inproj(xp, norm1_g, w_in, l, tm=1024)
        c_act, c_state = _conv_prompt(proj, n_p, t_p, *conv_args)
        a_act = _attn_prompt(proj, attn_sinks, n_p, t_p, l)
        x1, h2 = _mix(c_act, a_act, proj, xp, w_co_b, w_ao_b, w_o_b, norm2_g, l)
        act, f_g, f_v = _up_prompt(h2, n_p, t_p, w_up, ffn_conv_w, ffn_conv_b, l)
        xp = _down(act, w_down, x1, l)
        kv = proj.reshape(n_p, t_p, IN_COLS)[:, t_p - WINDOW:, :]
        outs["kp"].append(kv[:, :, COL_K:COL_K + D_KV].astype(F32).reshape(n_p, WINDOW, N_KV_HEADS, HEAD_DIM))
        outs["vp"].append(kv[:, :, COL_V:COL_V + D_KV].astype(F32).reshape(n_p, WINDOW, N_KV_HEADS, HEAD_DIM))
        outs["cp"].append(c_state)
        outs["fp"].append(jnp.concatenate([f_g, f_v], axis=-1))
        proj = _inproj(xs, norm1_g, w_in, l, tm=n_s * t_s)
        c_stack, c_act = _conv_sample(proj, state_conv, c_states, n_s, t_s, *conv_args)
        k_stack, v_stack, a_act = _attn_sample(proj, cache_k, cache_v, kv_states, attn_sinks, n_s, t_s, l)
        c_states, kv_states = (c_stack,), (k_stack, v_stack)
        x1, h2 = _mix(c_act, a_act, proj, xs, w_co_b, w_ao_b, w_o_b, norm2_g, l)
        act, f_g, f_v = _up_sample(h2, state_ffn_conv, n_s, t_s, w_up, ffn_conv_w, ffn_conv_b, l)
        xs = _down(act, w_down, x1, l)
        outs["fs"].append(jnp.concatenate([f_g, f_v], axis=-1))
    y_prompt = _final_norm(xp, final_norm_g).reshape(n_p, t_p, D_MODEL)
    y_sample = _final_norm(xs, final_norm_g).reshape(n_s, t_s, D_MODEL)
    kv_shape = (depth, n_s, WINDOW, N_KV_HEADS, HEAD_DIM)
    return (y_prompt, y_sample, jnp.stack(outs["kp"]), jnp.stack(outs["vp"]), jnp.stack(outs["cp"]),
            jnp.stack(outs["fp"]), k_stack.reshape(kv_shape), v_stack.reshape(kv_shape), c_stack,
            jnp.stack(outs["fs"]))
```

```python
import functools

import jax
import jax.numpy as jnp
from jax import lax
from jax.experimental import pallas as pl
from jax.experimental.pallas import tpu as pltpu

F32 = jnp.float32
BF16 = jnp.bfloat16

D_MODEL = 2048
HEAD_DIM = 64
N_HEADS = 16
N_KV_HEADS = 4
GROUP = N_HEADS // N_KV_HEADS
D_ATTN = N_HEADS * HEAD_DIM
D_KV = N_KV_HEADS * HEAD_DIM
WINDOW = 128
D_CONV = D_MODEL // 2
CONV_K = 31
D_FF = 3 * D_MODEL
FFN_K = 3
EPS = 1e-6
IN_COLS = 2 * D_CONV + D_ATTN + 2 * D_KV + 2 * D_MODEL
NEG = -1e30
SCALE = HEAD_DIM ** -0.5
SLOPES = tuple(2.0 ** (-8.0 * (h + 1) / N_HEADS) for h in range(N_HEADS))

V7X_SUBLANES = 8
V7X_SCOPED_VMEM_CAP_BYTES = 60000 * 1024

IN_TILE = 1280
COL_GLU_A = 0
COL_GLU_G = D_CONV
COL_Q = 2 * D_CONV
COL_K = COL_Q + D_ATTN
COL_V = COL_K + D_KV
COL_GATE_C = COL_V + D_KV
COL_GATE_A = COL_GATE_C + D_MODEL
GATE_BLOCK = 512

CONV_HALO = 32
CONV_ROW_CHUNK = 64
CONV_LANE_CHUNK = 256
FFN_HALO = V7X_SUBLANES
FFN_TILE = 512
FFN_PROMPT_TILE = 512
DOWN_TILE = 512


def _nbytes(shape, dtype):
    n = 1
    for s in shape:
        n *= s
    return n * jnp.dtype(dtype).itemsize


def _params(semantics, pipelined, resident=(), temps=()):
    est = 2 * sum(_nbytes(s, d) for s, d in pipelined)
    est += sum(_nbytes(s, d) for s, d in resident)
    est += sum(_nbytes(s, d) for s, d in temps)
    limit = min(V7X_SCOPED_VMEM_CAP_BYTES, est + est // 4)
    return pltpu.CompilerParams(dimension_semantics=semantics, vmem_limit_bytes=limit)


def _layer_vec(l, width, rows=1):
    return pl.BlockSpec((None, rows, width), lambda *_: (l, 0, 0))


def _as_rows(p):
    return p.reshape(p.shape[0], 1, p.shape[1])


def _rms(x, g):
    return x * lax.rsqrt(jnp.mean(x * x, axis=-1, keepdims=True) + EPS) * g


def _dot(a, b):
    return jnp.dot(a, b, preferred_element_type=F32)


def _skip_first_ref(body):
    def wrapped(_, *refs):
        body(*refs)
    return wrapped


def _stacked_call(body, l, prev, n_alias_out, **kw):
    if l == 0:
        return pl.pallas_call(body, **kw)
    for _ in prev:
        body = _skip_first_ref(body)
    kw["in_specs"] = [pl.BlockSpec(memory_space=pl.ANY)] * len(prev) + list(kw["in_specs"])
    call = pl.pallas_call(body, input_output_aliases={i: i for i in range(n_alias_out)}, **kw)
    return lambda *args: call(*prev, *args)


def _inproj_body(x_ref, g_ref, w_ref, o_ref, h_ref):
    @pl.when(pl.program_id(1) == 0)
    def _():
        h_ref[...] = _rms(x_ref[...], g_ref[...]).astype(BF16)

    o_ref[...] = _dot(h_ref[...], w_ref[...].astype(BF16)).astype(BF16)


def _inproj(x, g, w, l, tm):
    m = x.shape[0]
    blocks = [((tm, D_MODEL), F32), ((D_MODEL, IN_TILE), F32), ((tm, IN_TILE), BF16)]
    return pl.pallas_call(
        _inproj_body,
        grid=(m // tm, IN_COLS // IN_TILE),
        in_specs=[pl.BlockSpec((tm, D_MODEL), lambda i, j: (i, 0)),
                  _layer_vec(l, D_MODEL),
                  pl.BlockSpec((None, D_MODEL, IN_TILE), lambda i, j: (l, 0, j))],
        out_specs=pl.BlockSpec((tm, IN_TILE), lambda i, j: (i, j)),
        out_shape=jax.ShapeDtypeStruct((m, IN_COLS), BF16),
        scratch_shapes=[pltpu.VMEM((tm, D_MODEL), BF16)],
        compiler_params=_params(("parallel", "arbitrary"), blocks,
                                resident=[((tm, D_MODEL), BF16), ((D_MODEL, IN_TILE), BF16)],
                                temps=[((tm, D_MODEL), F32)]),
        name="inproj",
    )(x, _as_rows(g), w)


def _ln_silu(c, g, b):
    mu = jnp.mean(c, axis=-1, keepdims=True)
    xc = c - mu
    y = xc * lax.rsqrt(jnp.mean(xc * xc, axis=-1, keepdims=True) + EPS) * g + b
    return y * jax.nn.sigmoid(y)


def _conv_prompt_body(a_ref, g_ref, w_ref, b_ref, lng_ref, lnb_ref, act_ref, st_ref, xx_ref, cv_ref, sh_ref,
                      *, tt):
    i = pl.program_id(1)

    @pl.when(i == 0)
    def _():
        xx_ref[0:CONV_HALO, :] = jnp.zeros((CONV_HALO, D_CONV), F32)

    xx_ref[CONV_HALO:CONV_HALO + tt, :] = a_ref[...].astype(F32) * jax.nn.sigmoid(g_ref[...].astype(F32))

    first = CONV_HALO - (CONV_K - 1)
    sh_rows = sh_ref.shape[1]
    for lc in range(D_CONV // CONV_LANE_CHUNK):
        ls = pl.ds(lc * CONV_LANE_CHUNK, CONV_LANE_CHUNK)
        for r in range(1, V7X_SUBLANES):
            sh_ref[r - 1] = xx_ref[r:r + sh_rows, ls]
        for r0 in range(0, tt, CONV_ROW_CHUNK):
            acc = jnp.broadcast_to(b_ref[:, ls], (CONV_ROW_CHUNK, CONV_LANE_CHUNK))
            for j in range(CONV_K):
                q, r = divmod(first + j, V7X_SUBLANES)
                a0 = r0 + V7X_SUBLANES * q
                if r == 0:
                    window = xx_ref[a0:a0 + CONV_ROW_CHUNK, ls]
                else:
                    window = sh_ref[r - 1, a0:a0 + CONV_ROW_CHUNK, :]
                acc = acc + w_ref[j:j + 1, ls] * window
            cv_ref[r0:r0 + CONV_ROW_CHUNK, ls] = acc
    act_ref[...] = _ln_silu(cv_ref[...], lng_ref[...], lnb_ref[...]).astype(BF16)

    @pl.when(i == pl.num_programs(1) - 1)
    def _():
        st_ref[0] = xx_ref[tt + first:tt + CONV_HALO, :]

    xx_ref[0:CONV_HALO, :] = xx_ref[tt:tt + CONV_HALO, :]


def _conv_prompt(proj, n, t, w, b, lng, lnb, l, tt=512):
    tps = t // tt
    vec = _layer_vec(l, D_CONV)
    blocks = [((tt, D_CONV), BF16)] * 3
    shifted = (V7X_SUBLANES - 1, tt + CONV_HALO - V7X_SUBLANES, CONV_LANE_CHUNK)
    return pl.pallas_call(
        functools.partial(_conv_prompt_body, tt=tt),
        grid=(n, tps),
        in_specs=[pl.BlockSpec((tt, D_CONV), lambda s, i: (s * tps + i, COL_GLU_A // D_CONV)),
                  pl.BlockSpec((tt, D_CONV), lambda s, i: (s * tps + i, COL_GLU_G // D_CONV)),
                  _layer_vec(l, D_CONV, CONV_K), vec, vec, vec],
        out_specs=[pl.BlockSpec((tt, D_CONV), lambda s, i: (s * tps + i, 0)),
                   pl.BlockSpec((1, CONV_K - 1, D_CONV), lambda s, i: (s, 0, 0))],
        out_shape=[jax.ShapeDtypeStruct((n * t, D_CONV), BF16),
                   jax.ShapeDtypeStruct((n, CONV_K - 1, D_CONV), F32)],
        scratch_shapes=[pltpu.VMEM((CONV_HALO + tt, D_CONV), F32), pltpu.VMEM((tt, D_CONV), F32),
                        pltpu.VMEM(shifted, F32)],
        compiler_params=_params(("parallel", "arbitrary"), blocks,
                                resident=[((CONV_HALO + tt, D_CONV), F32), ((tt, D_CONV), F32), (shifted, F32)],
                                temps=[((tt, D_CONV), F32)] * 4),
        name="conv_prompt",
    )(proj, proj, w, _as_rows(b), _as_rows(lng), _as_rows(lnb))


def _conv_sample_body(a_ref, g_ref, s_ref, w_ref, b_ref, lng_ref, lnb_ref, st_ref, act_ref, xx_ref, cv_ref,
                      *, nb, t):
    hist = CONV_K - 1
    xx_ref[:, 0:hist, :] = s_ref[...]
    u = a_ref[...].astype(F32) * jax.nn.sigmoid(g_ref[...].astype(F32))
    xx_ref[:, hist:hist + t, :] = u.reshape(nb, t, D_CONV)
    st_ref[...] = xx_ref[:, t:t + hist, :]
    for lc in range(D_CONV // CONV_LANE_CHUNK):
        ls = pl.ds(lc * CONV_LANE_CHUNK, CONV_LANE_CHUNK)
        acc = jnp.broadcast_to(b_ref[:, ls], (nb, t, CONV_LANE_CHUNK))
        for j in range(CONV_K):
            acc = acc + w_ref[j:j + 1, ls] * xx_ref[:, j:j + t, ls]
        cv_ref[:, ls] = acc.reshape(nb * t, CONV_LANE_CHUNK)
    act_ref[...] = _ln_silu(cv_ref[...], lng_ref[...], lnb_ref[...]).astype(BF16)


def _conv_sample(proj, state, prev, n, t, w, b, lng, lnb, l, nb=16):
    depth = state.shape[0]
    hist = CONV_K - 1
    rows = nb * t
    xx_rows = -(-(hist + t) // V7X_SUBLANES) * V7X_SUBLANES
    vec = _layer_vec(l, D_CONV)
    st = pl.BlockSpec((None, nb, hist, D_CONV), lambda s: (l, s, 0, 0))
    blocks = [((rows, D_CONV), BF16)] * 3 + [((nb, CONV_HALO, D_CONV), F32)] * 2
    return _stacked_call(
        functools.partial(_conv_sample_body, nb=nb, t=t), l, prev, 1,
        grid=(n // nb,),
        in_specs=[pl.BlockSpec((rows, D_CONV), lambda s: (s, COL_GLU_A // D_CONV)),
                  pl.BlockSpec((rows, D_CONV), lambda s: (s, COL_GLU_G // D_CONV)),
                  st, _layer_vec(l, D_CONV, CONV_K), vec, vec, vec],
        out_specs=[st, pl.BlockSpec((rows, D_CONV), lambda s: (s, 0))],
        out_shape=[jax.ShapeDtypeStruct((depth, n, hist, D_CONV), F32),
                   jax.ShapeDtypeStruct((n * t, D_CONV), BF16)],
        scratch_shapes=[pltpu.VMEM((nb, xx_rows, D_CONV), F32), pltpu.VMEM((rows, D_CONV), F32)],
        compiler_params=_params(("parallel",), blocks,
                                resident=[((nb, xx_rows, D_CONV), F32), ((rows, D_CONV), F32)],
                                temps=[((rows, D_CONV), F32)] * 3),
        name="conv_sample",
    )(proj, proj, state, w, _as_rows(b), _as_rows(lng), _as_rows(lnb))


def _softmax_terms(s, valid, distf, slope, sink):
    s = jnp.where(valid, s - slope * distf, NEG)
    m = jnp.maximum(jnp.max(s, axis=-1, keepdims=True), sink)
    p = jnp.exp(s - m)
    denom = jnp.sum(p, axis=-1, keepdims=True) + jnp.exp(sink - m)
    return p, denom


def _attn_prompt_body(sink_ref, q_ref, kc_ref, vc_ref, kp_ref, vp_ref, o_ref, *, l, nsub):
    i = pl.program_id(1)
    blk = WINDOW
    q_all = q_ref[...] * SCALE
    k_all = jnp.concatenate([kp_ref[...], kc_ref[...]], axis=0)
    v_all = jnp.concatenate([vp_ref[...], vc_ref[...]], axis=0)
    r = lax.broadcasted_iota(jnp.int32, (blk, 2 * blk), 0)
    c = lax.broadcasted_iota(jnp.int32, (blk, 2 * blk), 1)
    dist = blk + r - c
    in_window = (dist >= 0) & (dist <= WINDOW)
    distf = dist.astype(F32)
    for sb in range(nsub):
        q = q_all[sb * blk:(sb + 1) * blk]
        k = k_all[sb * blk:(sb + 2) * blk]
        v = v_all[sb * blk:(sb + 2) * blk]
        valid = in_window & ((c >= blk) | (i > 0)) if sb == 0 else in_window
        for kv in range(N_KV_HEADS):
            kh = k[:, kv * HEAD_DIM:(kv + 1) * HEAD_DIM]
            vh = v[:, kv * HEAD_DIM:(kv + 1) * HEAD_DIM]
            heads = [kv * GROUP + g for g in range(GROUP)]
            qs = jnp.concatenate([q[:, h * HEAD_DIM:(h + 1) * HEAD_DIM] for h in heads], axis=0)
            s = lax.dot_general(qs, kh, (((1,), (1,)), ((), ())), preferred_element_type=F32)
            ps, ds = [], []
            for g, h in enumerate(heads):
                p, d = _softmax_terms(s[g * blk:(g + 1) * blk], valid, distf, SLOPES[h], sink_ref[l, h])
                ps.append(p.astype(BF16))
                ds.append(d)
            o = _dot(jnp.concatenate(ps, axis=0), vh)
            for g, h in enumerate(heads):
                o_ref[sb * blk:(sb + 1) * blk, h * HEAD_DIM:(h + 1) * HEAD_DIM] = (
                    o[g * blk:(g + 1) * blk] / ds[g]).astype(BF16)


def _attn_prompt(proj, sinks, n, t, l, nsub=1):
    blk = WINDOW
    rows = nsub * blk
    steps = t // rows
    kcol, vcol = COL_K // D_KV, COL_V // D_KV
    prev = lambda s, i: s * (t // blk) + jnp.maximum(nsub * i - 1, 0)
    blocks = ([((rows, D_ATTN), BF16)] * 2 + [((rows, D_KV), BF16)] * 2 + [((blk, D_KV), BF16)] * 2)
    return pl.pallas_call(
        functools.partial(_attn_prompt_body, l=l, nsub=nsub),
        grid=(n, steps),
        in_specs=[pl.BlockSpec(memory_space=pltpu.SMEM),
                  pl.BlockSpec((rows, D_ATTN), lambda s, i: (s * steps + i, COL_Q // D_ATTN)),
                  pl.BlockSpec((rows, D_KV), lambda s, i: (s * steps + i, kcol)),
                  pl.BlockSpec((rows, D_KV), lambda s, i: (s * steps + i, vcol)),
                  pl.BlockSpec((blk, D_KV), lambda s, i: (prev(s, i), kcol)),
                  pl.BlockSpec((blk, D_KV), lambda s, i: (prev(s, i), vcol))],
        out_specs=pl.BlockSpec((rows, D_ATTN), lambda s, i: (s * steps + i, 0)),
        out_shape=jax.ShapeDtypeStruct((n * t, D_ATTN), BF16),
        compiler_params=_params(("parallel", "parallel"), blocks,
                                temps=[((GROUP * blk, 2 * blk), F32)] * 16),
        name="attn_prompt",
    )(sinks, proj, proj, proj, proj, proj)


def _attn_sample_body(sink_ref, q_ref, kn_ref, vn_ref, ck_ref, cv_ref, nk_ref, nv_ref, o_ref, *, nb, t, l):
    eye = (lax.broadcasted_iota(jnp.int32, (D_KV, D_KV), 0)
           == lax.broadcasted_iota(jnp.int32, (D_KV, D_KV), 1)).astype(BF16)
    lane = lax.broadcasted_iota(jnp.int32, (D_KV, WINDOW), 1)
    for new_ref, cache_ref, out_ref in ((kn_ref, ck_ref, nk_ref), (vn_ref, cv_ref, nv_ref)):
        tr = lax.dot_general(eye, new_ref[...], (((1,), (1,)), ((), ())), preferred_element_type=F32)
        for s in range(nb):
            kept = pltpu.roll(cache_ref[s], WINDOW - t, axis=1)
            fresh = pltpu.roll(tr, (WINDOW - t - s * t) % WINDOW, axis=1)
            out_ref[s] = jnp.where(lane >= WINDOW - t, fresh, kept)
    q3 = (q_ref[...].astype(F32) * SCALE).reshape(nb, t, D_ATTN)
    rows = GROUP * t
    tq = lax.broadcasted_iota(jnp.int32, (rows, WINDOW), 0) % t
    w = lax.broadcasted_iota(jnp.int32, (rows, WINDOW), 1)
    dist_old = WINDOW + tq - w
    valid_old = (w < t) & (w >= tq)
    dist_new = WINDOW - t + tq - w
    valid_new = dist_new >= 0
    gi = lax.broadcasted_iota(jnp.int32, (rows, 1), 0) // t
    for kv in range(N_KV_HEADS):
        heads = [kv * GROUP + g for g in range(GROUP)]
        slope = jnp.zeros((rows, 1), F32)
        sink = jnp.zeros((rows, 1), F32)
        for g, h in enumerate(heads):
            slope = jnp.where(gi == g, SLOPES[h], slope)
            sink = jnp.where(gi == g, sink_ref[l, h], sink)
        hs = slice(kv * HEAD_DIM, (kv + 1) * HEAD_DIM)
        qs = jnp.concatenate([q3[:, :, h * HEAD_DIM:(h + 1) * HEAD_DIM] for h in heads], axis=1).astype(BF16)
        s_old = jnp.einsum("bqd,bdk->bqk", qs, ck_ref[:, hs, :].astype(BF16), preferred_element_type=F32)
        s_new = jnp.einsum("bqd,bdk->bqk", qs, nk_ref[:, hs, :].astype(BF16), preferred_element_type=F32)
        s_old = jnp.where(valid_old, s_old - slope * dist_old.astype(F32), NEG)
        s_new = jnp.where(valid_new, s_new - slope * dist_new.astype(F32), NEG)
        m = jnp.maximum(jnp.maximum(jnp.max(s_old, axis=-1, keepdims=True),
                                    jnp.max(s_new, axis=-1, keepdims=True)), sink)
        p_old = jnp.exp(s_old - m)
        p_new = jnp.exp(s_new - m)
        denom = (jnp.sum(p_old, axis=-1, keepdims=True) + jnp.sum(p_new, axis=-1, keepdims=True)
                 + jnp.exp(sink - m))
        o = (jnp.einsum("bqk,bdk->bqd", p_old.astype(BF16), cv_ref[:, hs, :].astype(BF16),
                        preferred_element_type=F32)
             + jnp.einsum("bqk,bdk->bqd", p_new.astype(BF16), nv_ref[:, hs, :].astype(BF16),
                          preferred_element_type=F32)) / denom
        for g, h in enumerate(heads):
            o_ref[:, h * HEAD_DIM:(h + 1) * HEAD_DIM] = (
                o[:, g * t:(g + 1) * t, :].reshape(nb * t, HEAD_DIM).astype(BF16))


def _attn_sample(proj, cache_k, cache_v, prev, sinks, n, t, l):
    depth = cache_k.shape[0]
    nb = WINDOW // t
    rows = nb * t
    kcol, vcol = COL_K // D_KV, COL_V // D_KV
    cache = pl.BlockSpec((None, nb, D_KV, WINDOW), lambda s: (l, s, 0, 0))
    stacked = jax.ShapeDtypeStruct((depth, n, D_KV, WINDOW), F32)
    blocks = [((rows, D_ATTN), BF16)] * 2 + [((rows, D_KV), BF16)] * 2 + [((nb, D_KV, WINDOW), F32)] * 4
    return _stacked_call(
        functools.partial(_attn_sample_body, nb=nb, t=t, l=l), l, prev, 2,
        grid=(n // nb,),
        in_specs=[pl.BlockSpec(memory_space=pltpu.SMEM),
                  pl.BlockSpec((rows, D_ATTN), lambda s: (s, COL_Q // D_ATTN)),
                  pl.BlockSpec((rows, D_KV), lambda s: (s, kcol)),
                  pl.BlockSpec((rows, D_KV), lambda s: (s, vcol)),
                  cache, cache],
        out_specs=[cache, cache, pl.BlockSpec((rows, D_ATTN), lambda s: (s, 0))],
        out_shape=[stacked, stacked, jax.ShapeDtypeStruct((n * t, D_ATTN), BF16)],
        compiler_params=_params(("parallel",), blocks,
                                temps=[((nb, D_KV, WINDOW), F32)] * 4 + [((nb, GROUP * t, WINDOW), F32)] * 8),
        name="attn_sample",
    )(sinks, proj, proj, proj, cache_k, cache_v)


def _mix_body(c_ref, a_ref, *refs):
    n_gate = D_MODEL // GATE_BLOCK
    gc_refs, ga_refs = refs[:n_gate], refs[n_gate:2 * n_gate]
    x_ref, wco_ref, wao_ref, wo_ref, n2_ref, x1_ref, h2_ref = refs[2 * n_gate:]
    gate_c = jnp.concatenate([g[...] for g in gc_refs], axis=1).astype(F32)
    gate_a = jnp.concatenate([g[...] for g in ga_refs], axis=1).astype(F32)
    branch_c = _dot(c_ref[...], wco_ref[...])
    branch_a = _dot(a_ref[...], wao_ref[...])
    merged = jax.nn.sigmoid(gate_c) * branch_c + jax.nn.sigmoid(gate_a) * branch_a
    x1 = x_ref[...] + _dot(merged.astype(BF16), wo_ref[...])
    x1_ref[...] = x1
    h2_ref[...] = _rms(x1, n2_ref[...]).astype(BF16)


def _mix(c_act, a_act, proj, x, wco, wao, wo, n2, l, tm=256):
    m = x.shape[0]
    row = lambda width, col: pl.BlockSpec((tm, width), lambda i: (i, col))
    const = lambda k: pl.BlockSpec((None, k, D_MODEL), lambda i: (l, 0, 0), pipeline_mode=pl.Buffered(1))
    blocks = ([((tm, D_CONV), BF16), ((tm, D_ATTN), BF16)] + [((tm, D_MODEL), F32)] * 2 + [((tm, D_MODEL), BF16)] * 3)
    weights = [((D_CONV, D_MODEL), BF16), ((D_ATTN, D_MODEL), BF16), ((D_MODEL, D_MODEL), BF16)]
    return pl.pallas_call(
        _mix_body,
        grid=(m // tm,),
        in_specs=[row(D_CONV, 0), row(D_ATTN, 0)]
                 + [row(GATE_BLOCK, COL_GATE_C // GATE_BLOCK + b) for b in range(D_MODEL // GATE_BLOCK)]
                 + [row(GATE_BLOCK, COL_GATE_A // GATE_BLOCK + b) for b in range(D_MODEL // GATE_BLOCK)]
                 + [row(D_MODEL, 0),
                  const(D_CONV), const(D_ATTN), const(D_MODEL),
                  _layer_vec(l, D_MODEL)],
        out_specs=[row(D_MODEL, 0), row(D_MODEL, 0)],
        out_shape=[jax.ShapeDtypeStruct((m, D_MODEL), F32), jax.ShapeDtypeStruct((m, D_MODEL), BF16)],
        compiler_params=_params(("parallel",), blocks, resident=weights, temps=[((tm, D_MODEL), F32)] * 4),
        name="mix",
    )(c_act, a_act, *([proj] * (2 * (D_MODEL // GATE_BLOCK))), x, wco, wao, wo, _as_rows(n2))


def _ffn_conv(e_ref, w_ref, b_ref, tm):
    first = FFN_HALO - (FFN_K - 1)
    out = b_ref[...]
    for j in range(FFN_K):
        out = out + w_ref[j:j + 1, :] * e_ref[first + j:first + j + tm, :]
    return out


def _up_prompt_body(h_ref, wg_ref, wv_ref, cwg_ref, cwv_ref, cbg_ref, cbv_ref, act_ref, sg_ref, sv_ref,
                    eg_ref, ev_ref, wgb_ref, wvb_ref, *, tm, tiles_per_seq):
    i = pl.program_id(1)
    pos = i % tiles_per_seq

    @pl.when(i == 0)
    def _():
        wgb_ref[...] = wg_ref[...].astype(BF16)
        wvb_ref[...] = wv_ref[...].astype(BF16)

    @pl.when(pos == 0)
    def _():
        eg_ref[0:FFN_HALO, :] = jnp.zeros((FFN_HALO, eg_ref.shape[1]), F32)
        ev_ref[0:FFN_HALO, :] = jnp.zeros((FFN_HALO, ev_ref.shape[1]), F32)

    h = h_ref[...]
    eg_ref[FFN_HALO:FFN_HALO + tm, :] = _dot(h, wgb_ref[...])
    ev_ref[FFN_HALO:FFN_HALO + tm, :] = _dot(h, wvb_ref[...])
    gate = _ffn_conv(eg_ref, cwg_ref, cbg_ref, tm)
    val = _ffn_conv(ev_ref, cwv_ref, cbv_ref, tm)
    act_ref[...] = (gate * jax.nn.sigmoid(gate) * val).astype(BF16)

    @pl.when(pos == tiles_per_seq - 1)
    def _():
        hist = FFN_K - 1
        sg_ref[0] = eg_ref[FFN_HALO + tm - hist:FFN_HALO + tm, :]
        sv_ref[0] = ev_ref[FFN_HALO + tm - hist:FFN_HALO + tm, :]

    eg_ref[0:FFN_HALO, :] = eg_ref[tm:tm + FFN_HALO, :]
    ev_ref[0:FFN_HALO, :] = ev_ref[tm:tm + FFN_HALO, :]


def _up_prompt(h2, n, t, w_up, cw, cb, l, tm=1024, tile=FFN_PROMPT_TILE):
    m = n * t
    tps = t // tm
    nt = D_FF // tile
    hist = FFN_K - 1
    col = lambda rows, off: pl.BlockSpec((None, rows, tile), lambda j, i: (l, 0, j + off))
    blocks = ([((tm, D_MODEL), BF16)] + [((D_MODEL, tile), F32)] * 2 + [((tm, tile), BF16)])
    scratch = [((FFN_HALO + tm, tile), F32)] * 2 + [((D_MODEL, tile), BF16)] * 2
    state = pl.BlockSpec((1, hist, tile), lambda j, i: (i // tps, 0, j))
    return pl.pallas_call(
        functools.partial(_up_prompt_body, tm=tm, tiles_per_seq=tps),
        grid=(nt, m // tm),
        in_specs=[pl.BlockSpec((tm, D_MODEL), lambda j, i: (i, 0)),
                  col(D_MODEL, 0), col(D_MODEL, nt), col(FFN_K, 0), col(FFN_K, nt), col(1, 0), col(1, nt)],
        out_specs=[pl.BlockSpec((tm, tile), lambda j, i: (i, j)), state, state],
        out_shape=[jax.ShapeDtypeStruct((m, D_FF), BF16),
                   jax.ShapeDtypeStruct((n, hist, D_FF), F32), jax.ShapeDtypeStruct((n, hist, D_FF), F32)],
        scratch_shapes=[pltpu.VMEM(s, d) for s, d in scratch],
        compiler_params=_params(("parallel", "arbitrary"), blocks, resident=scratch,
                                temps=[((tm, tile), F32)] * 4),
        name="up_prompt",
    )(h2, w_up, w_up, cw, cw, _as_rows(cb), _as_rows(cb))


def _ffn_conv_sample(u3, st_ref, w_ref, b_ref, tpos):
    st0 = st_ref[:, 0:1, :]
    st1 = st_ref[:, 1:2, :]
    prev1 = jnp.where(tpos == 0, st1, pltpu.roll(u3, 1, axis=1))
    prev2 = jnp.where(tpos == 0, st0, jnp.where(tpos == 1, st1, pltpu.roll(u3, 2, axis=1)))
    return b_ref[...] + w_ref[0:1, :] * prev2 + w_ref[1:2, :] * prev1 + w_ref[2:3, :] * u3


def _up_sample_body(h_ref, wg_ref, wv_ref, cwg_ref, cwv_ref, cbg_ref, cbv_ref, stg_ref, stv_ref,
                    sg_ref, sv_ref, act_ref, *, n, t):
    hist = FFN_K - 1
    h = h_ref[...]
    ug = _dot(h, wg_ref[...].astype(BF16)).reshape(n, t, FFN_TILE)
    uv = _dot(h, wv_ref[...].astype(BF16)).reshape(n, t, FFN_TILE)
    tpos = lax.broadcasted_iota(jnp.int32, (1, t, 1), 1)
    gate = _ffn_conv_sample(ug, stg_ref, cwg_ref, cbg_ref, tpos)
    val = _ffn_conv_sample(uv, stv_ref, cwv_ref, cbv_ref, tpos)
    act_ref[...] = (gate * jax.nn.sigmoid(gate) * val).reshape(n * t, FFN_TILE).astype(BF16)
    sg_ref[...] = ug[:, t - hist:t, :]
    sv_ref[...] = uv[:, t - hist:t, :]


def _up_sample(h2, state, prev, n, t, w_up, cw, cb, l):
    assert t == V7X_SUBLANES and FFN_K == 3
    depth = state.shape[0]
    m = n * t
    nt = D_FF // FFN_TILE
    hist = FFN_K - 1
    col = lambda rows, off: pl.BlockSpec((None, rows, FFN_TILE), lambda j: (l, 0, j + off))
    st = lambda off: pl.BlockSpec((None, n, hist, FFN_TILE), lambda j: (l, 0, 0, j + off))
    half_state = jax.ShapeDtypeStruct((depth, n, hist, D_FF), F32)
    blocks = ([((D_MODEL, FFN_TILE), F32)] * 2 + [((n, V7X_SUBLANES, FFN_TILE), F32)] * 4
              + [((m, FFN_TILE), BF16)])
    return _stacked_call(
        functools.partial(_up_sample_body, n=n, t=t), l, prev, 2,
        grid=(nt,),
        in_specs=[pl.BlockSpec((m, D_MODEL), lambda j: (0, 0)),
                  col(D_MODEL, 0), col(D_MODEL, nt), col(FFN_K, 0), col(FFN_K, nt), col(1, 0), col(1, nt),
                  st(0), st(nt)],
        out_specs=[st(0), st(0), pl.BlockSpec((m, FFN_TILE), lambda j: (0, j))],
        out_shape=[half_state, half_state, jax.ShapeDtypeStruct((m, D_FF), BF16)],
        compiler_params=_params(("parallel",), blocks,
                                resident=[((m, D_MODEL), BF16)] * 2 + [((D_MODEL, FFN_TILE), BF16)] * 2,
                                temps=[((m, FFN_TILE), F32)] * 8),
        name="up_sample",
    )(h2, w_up, w_up, cw, cw, _as_rows(cb), _as_rows(cb), state, state)


def _down_body(act_ref, w_ref, x_ref, o_ref, wb_ref):
    @pl.when(pl.program_id(1) == 0)
    def _():
        wb_ref[...] = w_ref[...].astype(BF16)

    o_ref[...] = x_ref[...] + _dot(act_ref[...], wb_ref[...])


def _down(act, w_down, x1, l, tm=512):
    m = x1.shape[0]
    blocks = [((tm, D_FF), BF16), ((D_FF, DOWN_TILE), F32), ((tm, DOWN_TILE), F32), ((tm, DOWN_TILE), F32)]
    return pl.pallas_call(
        _down_body,
        grid=(D_MODEL // DOWN_TILE, m // tm),
        in_specs=[pl.BlockSpec((tm, D_FF), lambda j, i: (i, 0)),
                  pl.BlockSpec((None, D_FF, DOWN_TILE), lambda j, i: (l, 0, j)),
                  pl.BlockSpec((tm, DOWN_TILE), lambda j, i: (i, j))],
        out_specs=pl.BlockSpec((tm, DOWN_TILE), lambda j, i: (i, j)),
        out_shape=jax.ShapeDtypeStruct((m, D_MODEL), F32),
        scratch_shapes=[pltpu.VMEM((D_FF, DOWN_TILE), BF16)],
        compiler_params=_params(("parallel", "arbitrary"), blocks, resident=[((D_FF, DOWN_TILE), BF16)],
                                temps=[((tm, DOWN_TILE), F32)]),
        name="down",
    )(act, w_down, x1)


def _final_norm_body(x_ref, g_ref, o_ref):
    o_ref[...] = _rms(x_ref[...], g_ref[...])


def _final_norm(x, g, tm=512):
    m = x.shape[0]
    blocks = [((tm, D_MODEL), F32)] * 2
    return pl.pallas_call(
        _final_norm_body,
        grid=(m // tm,),
        in_specs=[pl.BlockSpec((tm, D_MODEL), lambda i: (i, 0)), pl.BlockSpec((1, D_MODEL), lambda i: (0, 0))],
        out_specs=pl.BlockSpec((tm, D_MODEL), lambda i: (i, 0)),
        out_shape=jax.ShapeDtypeStruct((m, D_MODEL), F32),
        compiler_params=_params(("parallel",), blocks, temps=[((tm, D_MODEL), F32)]),
        name="final_norm",
    )(x, g.reshape(1, D_MODEL))


def kernel(x_prompt, x_sample, cache_k, cache_v, state_conv, state_ffn_conv, norm1_g, w_in, conv_w, conv_b,
           conv_ln_g, conv_ln_b, w_conv_out, attn_sinks, w_attn_out, w_out, norm2_g, w_up, ffn_conv_w,
           ffn_conv_b, w_down, final_norm_g):
    depth = w_in.shape[0]
    n_p, t_p, _ = x_prompt.shape
    n_s, t_s, _ = x_sample.shape
    xp = x_prompt.reshape(n_p * t_p, D_MODEL)
    xs = x_sample.reshape(n_s * t_s, D_MODEL)
    w_co_b, w_ao_b, w_o_b = (w.astype(BF16) for w in (w_conv_out, w_attn_out, w_out))
    keys_on_lanes = lambda c: jnp.transpose(c, (0, 1, 3, 4, 2)).reshape(depth, n_s, D_KV, WINDOW)
    keys_on_rows = lambda c: jnp.transpose(c.reshape(depth, n_s, N_KV_HEADS, HEAD_DIM, WINDOW), (0, 1, 4, 2, 3))
    cache_k, cache_v = keys_on_lanes(cache_k), keys_on_lanes(cache_v)
    outs = {k: [] for k in ("kp", "vp", "cp", "fp")}
    c_states, kv_states, f_states = (), (), ()
    for l in range(depth):
        conv_args = (conv_w, conv_b, conv_ln_g, conv_ln_b, l)
        proj = _inproj(xp, norm1_g, w_in, l, tm=1024)
        c_act, c_state = _conv_prompt(proj, n_p, t_p, *conv_args)
        a_act = _attn_prompt(proj, attn_sinks, n_p, t_p, l)
        x1, h2 = _mix(c_act, a_act, proj, xp, w_co_b, w_ao_b, w_o_b, norm2_g, l)
        act, f_g, f_v = _up_prompt(h2, n_p, t_p, w_up, ffn_conv_w, ffn_conv_b, l)
        xp = _down(act, w_down, x1, l)
        kv = proj.reshape(n_p, t_p, IN_COLS)[:, t_p - WINDOW:, :]
        outs["kp"].append(kv[:, :, COL_K:COL_K + D_KV].astype(F32).reshape(n_p, WINDOW, N_KV_HEADS, HEAD_DIM))
        outs["vp"].append(kv[:, :, COL_V:COL_V + D_KV].astype(F32).reshape(n_p, WINDOW, N_KV_HEADS, HEAD_DIM))
        outs["cp"].append(c_state)
        outs["fp"].append(jnp.concatenate([f_g, f_v], axis=-1))
        proj = _inproj(xs, norm1_g, w_in, l, tm=n_s * t_s)
        c_stack, c_act = _conv_sample(proj, state_conv, c_states, n_s, t_s, *conv_args)
        k_stack, v_stack, a_act = _attn_sample(proj, cache_k, cache_v, kv_states, attn_sinks, n_s, t_s, l)
        c_states, kv_states = (c_stack,), (k_stack, v_stack)
        x1, h2 = _mix(c_act, a_act, proj, xs, w_co_b, w_ao_b, w_o_b, norm2_g, l)
        fg_stack, fv_stack, act = _up_sample(h2, state_ffn_conv, f_states, n_s, t_s, w_up, ffn_conv_w,
                                             ffn_conv_b, l)
        f_states = (fg_stack, fv_stack)
        xs = _down(act, w_down, x1, l)
    y_prompt = _final_norm(xp, final_norm_g).reshape(n_p, t_p, D_MODEL)
    y_sample = _final_norm(xs, final_norm_g).reshape(n_s, t_s, D_MODEL)
    return (y_prompt, y_sample, jnp.stack(outs["kp"]), jnp.stack(outs["vp"]), jnp.stack(outs["cp"]),
            jnp.stack(outs["fp"]), keys_on_rows(k_stack), keys_on_rows(v_stack), c_stack,
            jnp.concatenate([fg_stack, fv_stack], axis=-1))
```

```python
import functools

import jax
import jax.numpy as jnp
from jax import lax
from jax.experimental import pallas as pl
from jax.experimental.pallas import tpu as pltpu

F32 = jnp.float32
BF16 = jnp.bfloat16

D_MODEL = 2048
HEAD_DIM = 64
N_HEADS = 16
N_KV_HEADS = 4
GROUP = N_HEADS // N_KV_HEADS
D_ATTN = N_HEADS * HEAD_DIM
D_KV = N_KV_HEADS * HEAD_DIM
WINDOW = 128
D_CONV = D_MODEL // 2
CONV_K = 31
D_FF = 3 * D_MODEL
FFN_K = 3
EPS = 1e-6
IN_COLS = 2 * D_CONV + D_ATTN + 2 * D_KV + 2 * D_MODEL
NEG = -1e30
SCALE = HEAD_DIM ** -0.5
SLOPES = tuple(2.0 ** (-8.0 * (h + 1) / N_HEADS) for h in range(N_HEADS))

V7X_SUBLANES = 8
V7X_SCOPED_VMEM_CAP_BYTES = 60000 * 1024

IN_TILE = 1280
COL_GLU_A = 0
COL_GLU_G = D_CONV
COL_Q = 2 * D_CONV
COL_K = COL_Q + D_ATTN
COL_V = COL_K + D_KV
COL_GATE_C = COL_V + D_KV
COL_GATE_A = COL_GATE_C + D_MODEL
GATE_BLOCK = 512

CONV_HALO = 32
CONV_ROW_CHUNK = 64
CONV_LANE_CHUNK = 256
FFN_HALO = V7X_SUBLANES
FFN_TILE = 512
FFN_PROMPT_TILE = 512
DOWN_TILE = 512


def _nbytes(shape, dtype):
    n = 1
    for s in shape:
        n *= s
    return n * jnp.dtype(dtype).itemsize


def _params(semantics, pipelined, resident=(), temps=()):
    est = 2 * sum(_nbytes(s, d) for s, d in pipelined)
    est += sum(_nbytes(s, d) for s, d in resident)
    est += sum(_nbytes(s, d) for s, d in temps)
    limit = min(V7X_SCOPED_VMEM_CAP_BYTES, est + est // 4)
    return pltpu.CompilerParams(dimension_semantics=semantics, vmem_limit_bytes=limit)


def _layer_vec(l, width, rows=1):
    return pl.BlockSpec((None, rows, width), lambda *_: (l, 0, 0))


def _as_rows(p):
    return p.reshape(p.shape[0], 1, p.shape[1])


def _rms(x, g):
    return x * lax.rsqrt(jnp.mean(x * x, axis=-1, keepdims=True) + EPS) * g


def _dot(a, b):
    return jnp.dot(a, b, preferred_element_type=F32)


def _skip_first_ref(body):
    def wrapped(_, *refs):
        body(*refs)
    return wrapped


def _stacked_call(body, l, prev, n_alias_out, **kw):
    if l == 0:
        return pl.pallas_call(body, **kw)
    for _ in prev:
        body = _skip_first_ref(body)
    kw["in_specs"] = [pl.BlockSpec(memory_space=pl.ANY)] * len(prev) + list(kw["in_specs"])
    call = pl.pallas_call(body, input_output_aliases={i: i for i in range(n_alias_out)}, **kw)
    return lambda *args: call(*prev, *args)


def _inproj_body(x_ref, g_ref, w_ref, o_ref, h_ref):
    @pl.when(pl.program_id(1) == 0)
    def _():
        h_ref[...] = _rms(x_ref[...], g_ref[...]).astype(BF16)

    o_ref[...] = _dot(h_ref[...], w_ref[...].astype(BF16)).astype(BF16)


def _inproj(x, g, w, l, tm):
    m = x.shape[0]
    blocks = [((tm, D_MODEL), F32), ((D_MODEL, IN_TILE), F32), ((tm, IN_TILE), BF16)]
    return pl.pallas_call(
        _inproj_body,
        grid=(m // tm, IN_COLS // IN_TILE),
        in_specs=[pl.BlockSpec((tm, D_MODEL), lambda i, j: (i, 0)),
                  _layer_vec(l, D_MODEL),
                  pl.BlockSpec((None, D_MODEL, IN_TILE), lambda i, j: (l, 0, j))],
        out_specs=pl.BlockSpec((tm, IN_TILE), lambda i, j: (i, j)),
        out_shape=jax.ShapeDtypeStruct((m, IN_COLS), BF16),
        scratch_shapes=[pltpu.VMEM((tm, D_MODEL), BF16)],
        compiler_params=_params(("parallel", "arbitrary"), blocks,
                                resident=[((tm, D_MODEL), BF16), ((D_MODEL, IN_TILE), BF16)],
                                temps=[((tm, D_MODEL), F32)]),
        name="inproj",
    )(x, _as_rows(g), w)


def _ln_silu(c, g, b):
    mu = jnp.mean(c, axis=-1, keepdims=True)
    xc = c - mu
    y = xc * lax.rsqrt(jnp.mean(xc * xc, axis=-1, keepdims=True) + EPS) * g + b
    return y * jax.nn.sigmoid(y)


def _conv_prompt_body(a_ref, g_ref, w_ref, b_ref, lng_ref, lnb_ref, act_ref, st_ref, xx_ref, cv_ref, sh_ref,
                      *, tt):
    i = pl.program_id(1)

    @pl.when(i == 0)
    def _():
        xx_ref[0:CONV_HALO, :] = jnp.zeros((CONV_HALO, D_CONV), F32)

    xx_ref[CONV_HALO:CONV_HALO + tt, :] = a_ref[...].astype(F32) * jax.nn.sigmoid(g_ref[...].astype(F32))

    first = CONV_HALO - (CONV_K - 1)
    sh_rows = sh_ref.shape[1]
    for lc in range(D_CONV // CONV_LANE_CHUNK):
        ls = pl.ds(lc * CONV_LANE_CHUNK, CONV_LANE_CHUNK)
        for r in range(1, V7X_SUBLANES):
            sh_ref[r - 1] = xx_ref[r:r + sh_rows, ls]
        for r0 in range(0, tt, CONV_ROW_CHUNK):
            acc = jnp.broadcast_to(b_ref[:, ls], (CONV_ROW_CHUNK, CONV_LANE_CHUNK))
            for j in range(CONV_K):
                q, r = divmod(first + j, V7X_SUBLANES)
                a0 = r0 + V7X_SUBLANES * q
                if r == 0:
                    window = xx_ref[a0:a0 + CONV_ROW_CHUNK, ls]
                else:
                    window = sh_ref[r - 1, a0:a0 + CONV_ROW_CHUNK, :]
                acc = acc + w_ref[j:j + 1, ls] * window
            cv_ref[r0:r0 + CONV_ROW_CHUNK, ls] = acc
    act_ref[...] = _ln_silu(cv_ref[...], lng_ref[...], lnb_ref[...]).astype(BF16)

    @pl.when(i == pl.num_programs(1) - 1)
    def _():
        st_ref[0] = xx_ref[tt + first:tt + CONV_HALO, :]

    xx_ref[0:CONV_HALO, :] = xx_ref[tt:tt + CONV_HALO, :]


def _conv_prompt(proj, n, t, w, b, lng, lnb, l, tt=512):
    tps = t // tt
    vec = _layer_vec(l, D_CONV)
    blocks = [((tt, D_CONV), BF16)] * 3
    shifted = (V7X_SUBLANES - 1, tt + CONV_HALO - V7X_SUBLANES, CONV_LANE_CHUNK)
    return pl.pallas_call(
        functools.partial(_conv_prompt_body, tt=tt),
        grid=(n, tps),
        in_specs=[pl.BlockSpec((tt, D_CONV), lambda s, i: (s * tps + i, COL_GLU_A // D_CONV)),
                  pl.BlockSpec((tt, D_CONV), lambda s, i: (s * tps + i, COL_GLU_G // D_CONV)),
                  _layer_vec(l, D_CONV, CONV_K), vec, vec, vec],
        out_specs=[pl.BlockSpec((tt, D_CONV), lambda s, i: (s * tps + i, 0)),
                   pl.BlockSpec((1, CONV_K - 1, D_CONV), lambda s, i: (s, 0, 0))],
        out_shape=[jax.ShapeDtypeStruct((n * t, D_CONV), BF16),
                   jax.ShapeDtypeStruct((n, CONV_K - 1, D_CONV), F32)],
        scratch_shapes=[pltpu.VMEM((CONV_HALO + tt, D_CONV), F32), pltpu.VMEM((tt, D_CONV), F32),
                        pltpu.VMEM(shifted, F32)],
        compiler_params=_params(("parallel", "arbitrary"), blocks,
                                resident=[((CONV_HALO + tt, D_CONV), F32), ((tt, D_CONV), F32), (shifted, F32)],
                                temps=[((tt, D_CONV), F32)] * 4),
        name="conv_prompt",
    )(proj, proj, w, _as_rows(b), _as_rows(lng), _as_rows(lnb))


def _conv_sample_body(a_ref, g_ref, s_ref, w_ref, b_ref, lng_ref, lnb_ref, st_ref, act_ref, new_ref, y_ref,
                      *, nb, t):
    hist = CONV_K - 1
    u = (a_ref[...].astype(F32) * jax.nn.sigmoid(g_ref[...].astype(F32))).reshape(nb, t, D_CONV)
    for j in range(t):
        new_ref[j] = u[:, j, :]
    for r in range(hist):
        st_ref[r] = s_ref[r + t] if r + t < hist else new_ref[r + t - hist]
    for j in range(t):
        for lc in range(D_CONV // CONV_LANE_CHUNK):
            ls = pl.ds(lc * CONV_LANE_CHUNK, CONV_LANE_CHUNK)
            acc = jnp.broadcast_to(b_ref[:, ls], (nb, CONV_LANE_CHUNK))
            for k in range(CONV_K):
                r = j + k
                plane = s_ref[r, :, ls] if r < hist else new_ref[r - hist, :, ls]
                acc = acc + w_ref[k:k + 1, ls] * plane
            y_ref[j, :, ls] = acc
    y = _ln_silu(y_ref[...], lng_ref[...], lnb_ref[...])
    act_ref[...] = jnp.stack([y[j] for j in range(t)], axis=1).reshape(nb * t, D_CONV).astype(BF16)


def _conv_sample(proj, state, prev, n, t, w, b, lng, lnb, l, nb=32):
    depth = state.shape[0]
    hist = CONV_K - 1
    rows = nb * t
    vec = _layer_vec(l, D_CONV)
    st = pl.BlockSpec((None, hist, nb, D_CONV), lambda s: (l, 0, s, 0))
    blocks = [((rows, D_CONV), BF16)] * 3 + [((hist, nb, D_CONV), F32)] * 2
    scratch = [((t, nb, D_CONV), F32)] * 2
    return _stacked_call(
        functools.partial(_conv_sample_body, nb=nb, t=t), l, prev, 1,
        grid=(n // nb,),
        in_specs=[pl.BlockSpec((rows, D_CONV), lambda s: (s, COL_GLU_A // D_CONV)),
                  pl.BlockSpec((rows, D_CONV), lambda s: (s, COL_GLU_G // D_CONV)),
                  st, _layer_vec(l, D_CONV, CONV_K), vec, vec, vec],
        out_specs=[st, pl.BlockSpec((rows, D_CONV), lambda s: (s, 0))],
        out_shape=[jax.ShapeDtypeStruct((depth, hist, n, D_CONV), F32),
                   jax.ShapeDtypeStruct((n * t, D_CONV), BF16)],
        scratch_shapes=[pltpu.VMEM(s, d) for s, d in scratch],
        compiler_params=_params(("parallel",), blocks, resident=scratch,
                                temps=[((rows, D_CONV), F32)] * 4),
        name="conv_sample",
    )(proj, proj, state, w, _as_rows(b), _as_rows(lng), _as_rows(lnb))


def _softmax_terms(s, valid, distf, slope, sink):
    s = jnp.where(valid, s - slope * distf, NEG)
    m = jnp.maximum(jnp.max(s, axis=-1, keepdims=True), sink)
    p = jnp.exp(s - m)
    denom = jnp.sum(p, axis=-1, keepdims=True) + jnp.exp(sink - m)
    return p, denom


def _attn_prompt_body(sink_ref, q_ref, kc_ref, vc_ref, kp_ref, vp_ref, o_ref, *, l, nsub):
    i = pl.program_id(1)
    blk = WINDOW
    q_all = q_ref[...] * SCALE
    k_all = jnp.concatenate([kp_ref[...], kc_ref[...]], axis=0)
    v_all = jnp.concatenate([vp_ref[...], vc_ref[...]], axis=0)
    r = lax.broadcasted_iota(jnp.int32, (blk, 2 * blk), 0)
    c = lax.broadcasted_iota(jnp.int32, (blk, 2 * blk), 1)
    dist = blk + r - c
    in_window = (dist >= 0) & (dist <= WINDOW)
    distf = dist.astype(F32)
    for sb in range(nsub):
        q = q_all[sb * blk:(sb + 1) * blk]
        k = k_all[sb * blk:(sb + 2) * blk]
        v = v_all[sb * blk:(sb + 2) * blk]
        valid = in_window & ((c >= blk) | (i > 0)) if sb == 0 else in_window
        scores = []
        for kv in range(N_KV_HEADS):
            kh = k[:, kv * HEAD_DIM:(kv + 1) * HEAD_DIM]
            heads = [kv * GROUP + g for g in range(GROUP)]
            qs = jnp.concatenate([q[:, h * HEAD_DIM:(h + 1) * HEAD_DIM] for h in heads], axis=0)
            scores.append(lax.dot_general(qs, kh, (((1,), (1,)), ((), ())),
                                          preferred_element_type=F32))
        probs, denoms = [], []
        for kv in range(N_KV_HEADS):
            ps = []
            for g in range(GROUP):
                h = kv * GROUP + g
                p, d = _softmax_terms(scores[kv][g * blk:(g + 1) * blk], valid, distf, SLOPES[h], sink_ref[l, h])
                ps.append(p.astype(BF16))
                denoms.append(d)
            probs.append(jnp.concatenate(ps, axis=0))
        outs = [_dot(probs[kv], v[:, kv * HEAD_DIM:(kv + 1) * HEAD_DIM]) for kv in range(N_KV_HEADS)]
        for h in range(N_HEADS):
            kv, g = divmod(h, GROUP)
            o_ref[sb * blk:(sb + 1) * blk, h * HEAD_DIM:(h + 1) * HEAD_DIM] = (
                outs[kv][g * blk:(g + 1) * blk] / denoms[h]).astype(BF16)


def _attn_prompt(proj, sinks, n, t, l, nsub=1):
    blk = WINDOW
    rows = nsub * blk
    steps = t // rows
    kcol, vcol = COL_K // D_KV, COL_V // D_KV
    prev = lambda s, i: s * (t // blk) + jnp.maximum(nsub * i - 1, 0)
    blocks = ([((rows, D_ATTN), BF16)] * 2 + [((rows, D_KV), BF16)] * 2 + [((blk, D_KV), BF16)] * 2)
    return pl.pallas_call(
        functools.partial(_attn_prompt_body, l=l, nsub=nsub),
        grid=(n, steps),
        in_specs=[pl.BlockSpec(memory_space=pltpu.SMEM),
                  pl.BlockSpec((rows, D_ATTN), lambda s, i: (s * steps + i, COL_Q // D_ATTN)),
                  pl.BlockSpec((rows, D_KV), lambda s, i: (s * steps + i, kcol)),
                  pl.BlockSpec((rows, D_KV), lambda s, i: (s * steps + i, vcol)),
                  pl.BlockSpec((blk, D_KV), lambda s, i: (prev(s, i), kcol)),
                  pl.BlockSpec((blk, D_KV), lambda s, i: (prev(s, i), vcol))],
        out_specs=pl.BlockSpec((rows, D_ATTN), lambda s, i: (s * steps + i, 0)),
        out_shape=jax.ShapeDtypeStruct((n * t, D_ATTN), BF16),
        compiler_params=_params(("parallel", "parallel"), blocks,
                                temps=[((GROUP * blk, 2 * blk), F32)] * 16),
        name="attn_prompt",
    )(sinks, proj, proj, proj, proj, proj)


def _attn_sample_body(sink_ref, q_ref, kn_ref, vn_ref, ck_ref, cv_ref, nk_ref, nv_ref, o_ref, *, nb, t, l):
    eye = (lax.broadcasted_iota(jnp.int32, (D_KV, D_KV), 0)
           == lax.broadcasted_iota(jnp.int32, (D_KV, D_KV), 1)).astype(BF16)
    lane = lax.broadcasted_iota(jnp.int32, (D_KV, WINDOW), 1)
    for new_ref, cache_ref, out_ref in ((kn_ref, ck_ref, nk_ref), (vn_ref, cv_ref, nv_ref)):
        tr = lax.dot_general(eye, new_ref[...], (((1,), (1,)), ((), ())), preferred_element_type=F32)
        for s in range(nb):
            kept = pltpu.roll(cache_ref[s], WINDOW - t, axis=1)
            fresh = pltpu.roll(tr, (WINDOW - t - s * t) % WINDOW, axis=1)
            out_ref[s] = jnp.where(lane >= WINDOW - t, fresh, kept)
    q3 = (q_ref[...].astype(F32) * SCALE).reshape(nb, t, D_ATTN)
    rows = GROUP * t
    tq = lax.broadcasted_iota(jnp.int32, (rows, WINDOW), 0) % t
    w = lax.broadcasted_iota(jnp.int32, (rows, WINDOW), 1)
    dist_old = WINDOW + tq - w
    valid_old = (w < t) & (w >= tq)
    dist_new = WINDOW - t + tq - w
    valid_new = dist_new >= 0
    gi = lax.broadcasted_iota(jnp.int32, (rows, 1), 0) // t
    for kv in range(N_KV_HEADS):
        heads = [kv * GROUP + g for g in range(GROUP)]
        slope = jnp.zeros((rows, 1), F32)
        sink = jnp.zeros((rows, 1), F32)
        for g, h in enumerate(heads):
            slope = jnp.where(gi == g, SLOPES[h], slope)
            sink = jnp.where(gi == g, sink_ref[l, h], sink)
        hs = slice(kv * HEAD_DIM, (kv + 1) * HEAD_DIM)
        qs = jnp.concatenate([q3[:, :, h * HEAD_DIM:(h + 1) * HEAD_DIM] for h in heads], axis=1).astype(BF16)
        s_old = jnp.einsum("bqd,bdk->bqk", qs, ck_ref[:, hs, :].astype(BF16), preferred_element_type=F32)
        s_new = jnp.einsum("bqd,bdk->bqk", qs, nk_ref[:, hs, :].astype(BF16), preferred_element_type=F32)
        s_old = jnp.where(valid_old, s_old - slope * dist_old.astype(F32), NEG)
        s_new = jnp.where(valid_new, s_new - slope * dist_new.astype(F32), NEG)
        m = jnp.maximum(jnp.maximum(jnp.max(s_old, axis=-1, keepdims=True),
                                    jnp.max(s_new, axis=-1, keepdims=True)), sink)
        p_old = jnp.exp(s_old - m)
        p_new = jnp.exp(s_new - m)
        denom = (jnp.sum(p_old, axis=-1, keepdims=True) + jnp.sum(p_new, axis=-1, keepdims=True)
                 + jnp.exp(sink - m))
        o = (jnp.einsum("bqk,bdk->bqd", p_old.astype(BF16), cv_ref[:, hs, :].astype(BF16),
                        preferred_element_type=F32)
             + jnp.einsum("bqk,bdk->bqd", p_new.astype(BF16), nv_ref[:, hs, :].astype(BF16),
                          preferred_element_type=F32)) / denom
        for g, h in enumerate(heads):
            o_ref[:, h * HEAD_DIM:(h + 1) * HEAD_DIM] = (
                o[:, g * t:(g + 1) * t, :].reshape(nb * t, HEAD_DIM).astype(BF16))


def _attn_sample(proj, cache_k, cache_v, prev, sinks, n, t, l):
    depth = cache_k.shape[0]
    nb = WINDOW // t
    rows = nb * t
    kcol, vcol = COL_K // D_KV, COL_V // D_KV
    cache = pl.BlockSpec((None, nb, D_KV, WINDOW), lambda s: (l, s, 0, 0))
    stacked = jax.ShapeDtypeStruct((depth, n, D_KV, WINDOW), F32)
    blocks = [((rows, D_ATTN), BF16)] * 2 + [((rows, D_KV), BF16)] * 2 + [((nb, D_KV, WINDOW), F32)] * 4
    return _stacked_call(
        functools.partial(_attn_sample_body, nb=nb, t=t, l=l), l, prev, 2,
        grid=(n // nb,),
        in_specs=[pl.BlockSpec(memory_space=pltpu.SMEM),
                  pl.BlockSpec((rows, D_ATTN), lambda s: (s, COL_Q // D_ATTN)),
                  pl.BlockSpec((rows, D_KV), lambda s: (s, kcol)),
                  pl.BlockSpec((rows, D_KV), lambda s: (s, vcol)),
                  cache, cache],
        out_specs=[cache, cache, pl.BlockSpec((rows, D_ATTN), lambda s: (s, 0))],
        out_shape=[stacked, stacked, jax.ShapeDtypeStruct((n * t, D_ATTN), BF16)],
        compiler_params=_params(("parallel",), blocks,
                                temps=[((nb, D_KV, WINDOW), F32)] * 4 + [((nb, GROUP * t, WINDOW), F32)] * 8),
        name="attn_sample",
    )(sinks, proj, proj, proj, cache_k, cache_v)


def _mix_body(c_ref, a_ref, *refs):
    n_gate = D_MODEL // GATE_BLOCK
    gc_refs, ga_refs = refs[:n_gate], refs[n_gate:2 * n_gate]
    x_ref, wco_ref, wao_ref, wo_ref, n2_ref, x1_ref, h2_ref = refs[2 * n_gate:]
    gate_c = jnp.concatenate([g[...] for g in gc_refs], axis=1).astype(F32)
    gate_a = jnp.concatenate([g[...] for g in ga_refs], axis=1).astype(F32)
    branch_c = _dot(c_ref[...], wco_ref[...])
    branch_a = _dot(a_ref[...], wao_ref[...])
    merged = jax.nn.sigmoid(gate_c) * branch_c + jax.nn.sigmoid(gate_a) * branch_a
    x1 = x_ref[...] + _dot(merged.astype(BF16), wo_ref[...])
    x1_ref[...] = x1
    h2_ref[...] = _rms(x1, n2_ref[...]).astype(BF16)


def _mix(c_act, a_act, proj, x, wco, wao, wo, n2, l, tm=256):
    m = x.shape[0]
    row = lambda width, col: pl.BlockSpec((tm, width), lambda i: (i, col))
    const = lambda k: pl.BlockSpec((None, k, D_MODEL), lambda i: (l, 0, 0), pipeline_mode=pl.Buffered(1))
    blocks = ([((tm, D_CONV), BF16), ((tm, D_ATTN), BF16)] + [((tm, D_MODEL), F32)] * 2 + [((tm, D_MODEL), BF16)] * 3)
    weights = [((D_CONV, D_MODEL), BF16), ((D_ATTN, D_MODEL), BF16), ((D_MODEL, D_MODEL), BF16)]
    return pl.pallas_call(
        _mix_body,
        grid=(m // tm,),
        in_specs=[row(D_CONV, 0), row(D_ATTN, 0)]
                 + [row(GATE_BLOCK, COL_GATE_C // GATE_BLOCK + b) for b in range(D_MODEL // GATE_BLOCK)]
                 + [row(GATE_BLOCK, COL_GATE_A // GATE_BLOCK + b) for b in range(D_MODEL // GATE_BLOCK)]
                 + [row(D_MODEL, 0),
                  const(D_CONV), const(D_ATTN), const(D_MODEL),
                  _layer_vec(l, D_MODEL)],
        out_specs=[row(D_MODEL, 0), row(D_MODEL, 0)],
        out_shape=[jax.ShapeDtypeStruct((m, D_MODEL), F32), jax.ShapeDtypeStruct((m, D_MODEL), BF16)],
        compiler_params=_params(("parallel",), blocks, resident=weights, temps=[((tm, D_MODEL), F32)] * 4),
        name="mix",
    )(c_act, a_act, *([proj] * (2 * (D_MODEL // GATE_BLOCK))), x, wco, wao, wo, _as_rows(n2))


def _ffn_conv(e_ref, w_ref, b_ref, tm):
    first = FFN_HALO - (FFN_K - 1)
    out = b_ref[...]
    for j in range(FFN_K):
        out = out + w_ref[j:j + 1, :] * e_ref[first + j:first + j + tm, :]
    return out


def _up_prompt_body(h_ref, wg_ref, wv_ref, cwg_ref, cwv_ref, cbg_ref, cbv_ref, act_ref, sg_ref, sv_ref,
                    eg_ref, ev_ref, wgb_ref, wvb_ref, *, tm, tiles_per_seq):
    i = pl.program_id(1)
    pos = i % tiles_per_seq

    @pl.when(i == 0)
    def _():
        wgb_ref[...] = wg_ref[...].astype(BF16)
        wvb_ref[...] = wv_ref[...].astype(BF16)

    @pl.when(pos == 0)
    def _():
        eg_ref[0:FFN_HALO, :] = jnp.zeros((FFN_HALO, eg_ref.shape[1]), F32)
        ev_ref[0:FFN_HALO, :] = jnp.zeros((FFN_HALO, ev_ref.shape[1]), F32)

    h = h_ref[...]
    eg_ref[FFN_HALO:FFN_HALO + tm, :] = _dot(h, wgb_ref[...])
    ev_ref[FFN_HALO:FFN_HALO + tm, :] = _dot(h, wvb_ref[...])
    gate = _ffn_conv(eg_ref, cwg_ref, cbg_ref, tm)
    val = _ffn_conv(ev_ref, cwv_ref, cbv_ref, tm)
    act_ref[...] = (gate * jax.nn.sigmoid(gate) * val).astype(BF16)

    @pl.when(pos == tiles_per_seq - 1)
    def _():
        hist = FFN_K - 1
        sg_ref[0] = eg_ref[FFN_HALO + tm - hist:FFN_HALO + tm, :]
        sv_ref[0] = ev_ref[FFN_HALO + tm - hist:FFN_HALO + tm, :]

    eg_ref[0:FFN_HALO, :] = eg_ref[tm:tm + FFN_HALO, :]
    ev_ref[0:FFN_HALO, :] = ev_ref[tm:tm + FFN_HALO, :]


def _up_prompt(h2, n, t, w_up, cw, cb, l, tm=1024, tile=FFN_PROMPT_TILE):
    m = n * t
    tps = t // tm
    nt = D_FF // tile
    hist = FFN_K - 1
    col = lambda rows, off: pl.BlockSpec((None, rows, tile), lambda j, i: (l, 0, j + off))
    blocks = ([((tm, D_MODEL), BF16)] + [((D_MODEL, tile), F32)] * 2 + [((tm, tile), BF16)])
    scratch = [((FFN_HALO + tm, tile), F32)] * 2 + [((D_MODEL, tile), BF16)] * 2
    state = pl.BlockSpec((1, hist, tile), lambda j, i: (i // tps, 0, j))
    return pl.pallas_call(
        functools.partial(_up_prompt_body, tm=tm, tiles_per_seq=tps),
        grid=(nt, m // tm),
        in_specs=[pl.BlockSpec((tm, D_MODEL), lambda j, i: (i, 0)),
                  col(D_MODEL, 0), col(D_MODEL, nt), col(FFN_K, 0), col(FFN_K, nt), col(1, 0), col(1, nt)],
        out_specs=[pl.BlockSpec((tm, tile), lambda j, i: (i, j)), state, state],
        out_shape=[jax.ShapeDtypeStruct((m, D_FF), BF16),
                   jax.ShapeDtypeStruct((n, hist, D_FF), F32), jax.ShapeDtypeStruct((n, hist, D_FF), F32)],
        scratch_shapes=[pltpu.VMEM(s, d) for s, d in scratch],
        compiler_params=_params(("parallel", "arbitrary"), blocks, resident=scratch,
                                temps=[((tm, tile), F32)] * 4),
        name="up_prompt",
    )(h2, w_up, w_up, cw, cw, _as_rows(cb), _as_rows(cb))


def _ffn_conv_sample(u3, st_ref, w_ref, b_ref, tpos):
    st0 = st_ref[:, 0:1, :]
    st1 = st_ref[:, 1:2, :]
    prev1 = jnp.where(tpos == 0, st1, pltpu.roll(u3, 1, axis=1))
    prev2 = jnp.where(tpos == 0, st0, jnp.where(tpos == 1, st1, pltpu.roll(u3, 2, axis=1)))
    return b_ref[...] + w_ref[0:1, :] * prev2 + w_ref[1:2, :] * prev1 + w_ref[2:3, :] * u3


def _up_sample_body(h_ref, wg_ref, wv_ref, cwg_ref, cwv_ref, cbg_ref, cbv_ref, stg_ref, stv_ref,
                    s_ref, act_ref, *, n, t):
    hist = FFN_K - 1
    h = h_ref[...]
    ug = _dot(h, wg_ref[...].astype(BF16)).reshape(n, t, FFN_TILE)
    uv = _dot(h, wv_ref[...].astype(BF16)).reshape(n, t, FFN_TILE)
    tpos = lax.broadcasted_iota(jnp.int32, (1, t, 1), 1)
    gate = _ffn_conv_sample(ug, stg_ref, cwg_ref, cbg_ref, tpos)
    val = _ffn_conv_sample(uv, stv_ref, cwv_ref, cbv_ref, tpos)
    act_ref[...] = (gate * jax.nn.sigmoid(gate) * val).reshape(n * t, FFN_TILE).astype(BF16)
    s_ref[:, :, 0, :] = ug[:, t - hist:t, :]
    s_ref[:, :, 1, :] = uv[:, t - hist:t, :]


def _up_sample(h2, state, prev, n, t, w_up, cw, cb, l):
    assert t == V7X_SUBLANES and FFN_K == 3
    depth = state.shape[0]
    m = n * t
    nt = D_FF // FFN_TILE
    hist = FFN_K - 1
    col = lambda rows, off: pl.BlockSpec((None, rows, FFN_TILE), lambda j: (l, 0, j + off))
    st = lambda off: pl.BlockSpec((None, n, hist, FFN_TILE), lambda j: (l, 0, 0, j + off))
    new_state = pl.BlockSpec((None, n, hist, 2, FFN_TILE), lambda j: (l, 0, 0, 0, j))
    blocks = ([((D_MODEL, FFN_TILE), F32)] * 2 + [((n, V7X_SUBLANES, FFN_TILE), F32)] * 4
              + [((m, FFN_TILE), BF16)])
    return _stacked_call(
        functools.partial(_up_sample_body, n=n, t=t), l, prev, 1,
        grid=(nt,),
        in_specs=[pl.BlockSpec((m, D_MODEL), lambda j: (0, 0)),
                  col(D_MODEL, 0), col(D_MODEL, nt), col(FFN_K, 0), col(FFN_K, nt), col(1, 0), col(1, nt),
                  st(0), st(nt)],
        out_specs=[new_state, pl.BlockSpec((m, FFN_TILE), lambda j: (0, j))],
        out_shape=[jax.ShapeDtypeStruct((depth, n, hist, 2, D_FF), F32), jax.ShapeDtypeStruct((m, D_FF), BF16)],
        compiler_params=_params(("parallel",), blocks,
                                resident=[((m, D_MODEL), BF16)] * 2 + [((D_MODEL, FFN_TILE), BF16)] * 2,
                                temps=[((m, FFN_TILE), F32)] * 8),
        name="up_sample",
    )(h2, w_up, w_up, cw, cw, _as_rows(cb), _as_rows(cb), state, state)


def _down_body(act_ref, w_ref, x_ref, o_ref, wb_ref):
    @pl.when(pl.program_id(1) == 0)
    def _():
        wb_ref[...] = w_ref[...].astype(BF16)

    o_ref[...] = x_ref[...] + _dot(act_ref[...], wb_ref[...])


def _down(act, w_down, x1, l, tm=512):
    m = x1.shape[0]
    blocks = [((tm, D_FF), BF16), ((D_FF, DOWN_TILE), F32), ((tm, DOWN_TILE), F32), ((tm, DOWN_TILE), F32)]
    return pl.pallas_call(
        _down_body,
        grid=(D_MODEL // DOWN_TILE, m // tm),
        in_specs=[pl.BlockSpec((tm, D_FF), lambda j, i: (i, 0)),
                  pl.BlockSpec((None, D_FF, DOWN_TILE), lambda j, i: (l, 0, j)),
                  pl.BlockSpec((tm, DOWN_TILE), lambda j, i: (i, j))],
        out_specs=pl.BlockSpec((tm, DOWN_TILE), lambda j, i: (i, j)),
        out_shape=jax.ShapeDtypeStruct((m, D_MODEL), F32),
        scratch_shapes=[pltpu.VMEM((D_FF, DOWN_TILE), BF16)],
        compiler_params=_params(("parallel", "arbitrary"), blocks, resident=[((D_FF, DOWN_TILE), BF16)],
                                temps=[((tm, DOWN_TILE), F32)]),
        name="down",
    )(act, w_down, x1)


def _final_norm_body(x_ref, g_ref, o_ref):
    o_ref[...] = _rms(x_ref[...], g_ref[...])


def _final_norm(x, g, tm=512):
    m = x.shape[0]
    blocks = [((tm, D_MODEL), F32)] * 2
    return pl.pallas_call(
        _final_norm_body,
        grid=(m // tm,),
        in_specs=[pl.BlockSpec((tm, D_MODEL), lambda i: (i, 0)), pl.BlockSpec((1, D_MODEL), lambda i: (0, 0))],
        out_specs=pl.BlockSpec((tm, D_MODEL), lambda i: (i, 0)),
        out_shape=jax.ShapeDtypeStruct((m, D_MODEL), F32),
        compiler_params=_params(("parallel",), blocks, temps=[((tm, D_MODEL), F32)]),
        name="final_norm",
    )(x, g.reshape(1, D_MODEL))


def kernel(x_prompt, x_sample, cache_k, cache_v, state_conv, state_ffn_conv, norm1_g, w_in, conv_w, conv_b,
           conv_ln_g, conv_ln_b, w_conv_out, attn_sinks, w_attn_out, w_out, norm2_g, w_up, ffn_conv_w,
           ffn_conv_b, w_down, final_norm_g):
    depth = w_in.shape[0]
    n_p, t_p, _ = x_prompt.shape
    n_s, t_s, _ = x_sample.shape
    xp = x_prompt.reshape(n_p * t_p, D_MODEL)
    xs = x_sample.reshape(n_s * t_s, D_MODEL)
    w_co_b, w_ao_b, w_o_b = (w.astype(BF16) for w in (w_conv_out, w_attn_out, w_out))
    keys_on_lanes = lambda c: jnp.transpose(c, (0, 1, 3, 4, 2)).reshape(depth, n_s, D_KV, WINDOW)
    keys_on_rows = lambda c: jnp.transpose(c.reshape(depth, n_s, N_KV_HEADS, HEAD_DIM, WINDOW), (0, 1, 4, 2, 3))
    cache_k, cache_v = keys_on_lanes(cache_k), keys_on_lanes(cache_v)
    rows_first = lambda s: jnp.transpose(s, (0, 2, 1, 3))
    state_conv = rows_first(state_conv)
    outs = {k: [] for k in ("kp", "vp", "cp", "fp")}
    c_states, kv_states, f_states = (), (), ()
    for l in range(depth):
        conv_args = (conv_w, conv_b, conv_ln_g, conv_ln_b, l)
        proj = _inproj(xp, norm1_g, w_in, l, tm=1024)
        c_act, c_state = _conv_prompt(proj, n_p, t_p, *conv_args)
        a_act = _attn_prompt(proj, attn_sinks, n_p, t_p, l)
        x1, h2 = _mix(c_act, a_act, proj, xp, w_co_b, w_ao_b, w_o_b, norm2_g, l)
        act, f_g, f_v = _up_prompt(h2, n_p, t_p, w_up, ffn_conv_w, ffn_conv_b, l)
        xp = _down(act, w_down, x1, l)
        kv = proj.reshape(n_p, t_p, IN_COLS)[:, t_p - WINDOW:, :]
        outs["kp"].append(kv[:, :, COL_K:COL_K + D_KV].astype(F32).reshape(n_p, WINDOW, N_KV_HEADS, HEAD_DIM))
        outs["vp"].append(kv[:, :, COL_V:COL_V + D_KV].astype(F32).reshape(n_p, WINDOW, N_KV_HEADS, HEAD_DIM))
        outs["cp"].append(c_state)
        outs["fp"].append(jnp.concatenate([f_g, f_v], axis=-1))
        proj = _inproj(xs, norm1_g, w_in, l, tm=n_s * t_s)
        c_stack, c_act = _conv_sample(proj, state_conv, c_states, n_s, t_s, *conv_args)
        k_stack, v_stack, a_act = _attn_sample(proj, cache_k, cache_v, kv_states, attn_sinks, n_s, t_s, l)
        c_states, kv_states = (c_stack,), (k_stack, v_stack)
        x1, h2 = _mix(c_act, a_act, proj, xs, w_co_b, w_ao_b, w_o_b, norm2_g, l)
        f_stack, act = _up_sample(h2, state_ffn_conv, f_states, n_s, t_s, w_up, ffn_conv_w, ffn_conv_b, l)
        f_states = (f_stack,)
        xs = _down(act, w_down, x1, l)
    y_prompt = _final_norm(xp, final_norm_g).reshape(n_p, t_p, D_MODEL)
    y_sample = _final_norm(xs, final_norm_g).reshape(n_s, t_s, D_MODEL)
    return (y_prompt, y_sample, jnp.stack(outs["kp"]), jnp.stack(outs["vp"]), jnp.stack(outs["cp"]),
            jnp.stack(outs["fp"]), keys_on_rows(k_stack), keys_on_rows(v_stack), rows_first(c_stack),
            f_stack.reshape(depth, n_s, FFN_K - 1, 2 * D_FF))
```

```python
import functools

import jax
import jax.numpy as jnp
from jax import lax
from jax.experimental import pallas as pl
from jax.experimental.pallas import tpu as pltpu

F32 = jnp.float32
BF16 = jnp.bfloat16

D_MODEL = 2048
HEAD_DIM = 64
N_HEADS = 16
N_KV_HEADS = 4
GROUP = N_HEADS // N_KV_HEADS
D_ATTN = N_HEADS * HEAD_DIM
D_KV = N_KV_HEADS * HEAD_DIM
WINDOW = 128
D_CONV = D_MODEL // 2
CONV_K = 31
D_FF = 3 * D_MODEL
FFN_K = 3
EPS = 1e-6
IN_COLS = 2 * D_CONV + D_ATTN + 2 * D_KV + 2 * D_MODEL
NEG = -1e30
SCALE = HEAD_DIM ** -0.5
SLOPES = tuple(2.0 ** (-8.0 * (h + 1) / N_HEADS) for h in range(N_HEADS))

V7X_SUBLANES = 8
V7X_SCOPED_VMEM_CAP_BYTES = 60000 * 1024

IN_TILE = 1280
COL_GLU_A = 0
COL_GLU_G = D_CONV
COL_Q = 2 * D_CONV
COL_K = COL_Q + D_ATTN
COL_V = COL_K + D_KV
COL_GATE_C = COL_V + D_KV
COL_GATE_A = COL_GATE_C + D_MODEL
GATE_BLOCK = 512

CONV_HALO = 32
CONV_ROW_CHUNK = 64
CONV_LANE_CHUNK = 256
FFN_HALO = V7X_SUBLANES
FFN_TILE = 512
FFN_PROMPT_TILE = 512
DOWN_TILE = 512


def _nbytes(shape, dtype):
    n = 1
    for s in shape:
        n *= s
    return n * jnp.dtype(dtype).itemsize


def _params(semantics, pipelined, resident=(), temps=(), keep_operands_in_hbm=True):
    est = 2 * sum(_nbytes(s, d) for s, d in pipelined)
    est += sum(_nbytes(s, d) for s, d in resident)
    est += sum(_nbytes(s, d) for s, d in temps)
    limit = min(V7X_SCOPED_VMEM_CAP_BYTES, est + est // 4)
    if keep_operands_in_hbm:
        limit = V7X_SCOPED_VMEM_CAP_BYTES
    return pltpu.CompilerParams(dimension_semantics=semantics, vmem_limit_bytes=limit)


def _layer_vec(l, width, rows=1):
    return pl.BlockSpec((None, rows, width), lambda *_: (l, 0, 0))


def _as_rows(p):
    return p.reshape(p.shape[0], 1, p.shape[1])


def _rms(x, g):
    return x * lax.rsqrt(jnp.mean(x * x, axis=-1, keepdims=True) + EPS) * g


def _dot(a, b):
    return jnp.dot(a, b, preferred_element_type=F32)


def _skip_first_ref(body):
    def wrapped(_, *refs):
        body(*refs)
    return wrapped


def _stacked_call(body, l, prev, n_alias_out, **kw):
    if l == 0:
        return pl.pallas_call(body, **kw)
    for _ in prev:
        body = _skip_first_ref(body)
    kw["in_specs"] = [pl.BlockSpec(memory_space=pl.ANY)] * len(prev) + list(kw["in_specs"])
    call = pl.pallas_call(body, input_output_aliases={i: i for i in range(n_alias_out)}, **kw)
    return lambda *args: call(*prev, *args)


def _inproj_body(x_ref, g_ref, w_ref, o_ref, h_ref):
    @pl.when(pl.program_id(1) == 0)
    def _():
        h_ref[...] = _rms(x_ref[...], g_ref[...]).astype(BF16)

    o_ref[...] = _dot(h_ref[...], w_ref[...].astype(BF16)).astype(BF16)


def _inproj(x, g, w, l, tm):
    m = x.shape[0]
    blocks = [((tm, D_MODEL), F32), ((D_MODEL, IN_TILE), F32), ((tm, IN_TILE), BF16)]
    return pl.pallas_call(
        _inproj_body,
        grid=(m // tm, IN_COLS // IN_TILE),
        in_specs=[pl.BlockSpec((tm, D_MODEL), lambda i, j: (i, 0)),
                  _layer_vec(l, D_MODEL),
                  pl.BlockSpec((None, D_MODEL, IN_TILE), lambda i, j: (l, 0, j))],
        out_specs=pl.BlockSpec((tm, IN_TILE), lambda i, j: (i, j)),
        out_shape=jax.ShapeDtypeStruct((m, IN_COLS), BF16),
        scratch_shapes=[pltpu.VMEM((tm, D_MODEL), BF16)],
        compiler_params=_params(("parallel", "arbitrary"), blocks,
                                resident=[((tm, D_MODEL), BF16), ((D_MODEL, IN_TILE), BF16)],
                                temps=[((tm, D_MODEL), F32)]),
        name="inproj",
    )(x, _as_rows(g), w)


def _ln_silu(c, g, b):
    mu = jnp.mean(c, axis=-1, keepdims=True)
    xc = c - mu
    y = xc * lax.rsqrt(jnp.mean(xc * xc, axis=-1, keepdims=True) + EPS) * g + b
    return y * jax.nn.sigmoid(y)


def _conv_prompt_body(a_ref, g_ref, w_ref, b_ref, lng_ref, lnb_ref, act_ref, st_ref, xx_ref, cv_ref, sh_ref,
                      *, tt):
    i = pl.program_id(1)

    @pl.when(i == 0)
    def _():
        xx_ref[0:CONV_HALO, :] = jnp.zeros((CONV_HALO, D_CONV), F32)

    xx_ref[CONV_HALO:CONV_HALO + tt, :] = a_ref[...].astype(F32) * jax.nn.sigmoid(g_ref[...].astype(F32))

    first = CONV_HALO - (CONV_K - 1)
    sh_rows = sh_ref.shape[1]
    for lc in range(D_CONV // CONV_LANE_CHUNK):
        ls = pl.ds(lc * CONV_LANE_CHUNK, CONV_LANE_CHUNK)
        for r in range(1, V7X_SUBLANES):
            sh_ref[r - 1] = xx_ref[r:r + sh_rows, ls]
        for r0 in range(0, tt, CONV_ROW_CHUNK):
            acc = jnp.broadcast_to(b_ref[:, ls], (CONV_ROW_CHUNK, CONV_LANE_CHUNK))
            for j in range(CONV_K):
                q, r = divmod(first + j, V7X_SUBLANES)
                a0 = r0 + V7X_SUBLANES * q
                if r == 0:
                    window = xx_ref[a0:a0 + CONV_ROW_CHUNK, ls]
                else:
                    window = sh_ref[r - 1, a0:a0 + CONV_ROW_CHUNK, :]
                acc = acc + w_ref[j:j + 1, ls] * window
            cv_ref[r0:r0 + CONV_ROW_CHUNK, ls] = acc
    act_ref[...] = _ln_silu(cv_ref[...], lng_ref[...], lnb_ref[...]).astype(BF16)

    @pl.when(i == pl.num_programs(1) - 1)
    def _():
        st_ref[0] = xx_ref[tt + first:tt + CONV_HALO, :]

    xx_ref[0:CONV_HALO, :] = xx_ref[tt:tt + CONV_HALO, :]


def _conv_prompt(proj, n, t, w, b, lng, lnb, l, tt=512):
    tps = t // tt
    vec = _layer_vec(l, D_CONV)
    blocks = [((tt, D_CONV), BF16)] * 3
    shifted = (V7X_SUBLANES - 1, tt + CONV_HALO - V7X_SUBLANES, CONV_LANE_CHUNK)
    return pl.pallas_call(
        functools.partial(_conv_prompt_body, tt=tt),
        grid=(n, tps),
        in_specs=[pl.BlockSpec((tt, D_CONV), lambda s, i: (s * tps + i, COL_GLU_A // D_CONV)),
                  pl.BlockSpec((tt, D_CONV), lambda s, i: (s * tps + i, COL_GLU_G // D_CONV)),
                  _layer_vec(l, D_CONV, CONV_K), vec, vec, vec],
        out_specs=[pl.BlockSpec((tt, D_CONV), lambda s, i: (s * tps + i, 0)),
                   pl.BlockSpec((1, CONV_K - 1, D_CONV), lambda s, i: (s, 0, 0))],
        out_shape=[jax.ShapeDtypeStruct((n * t, D_CONV), BF16),
                   jax.ShapeDtypeStruct((n, CONV_K - 1, D_CONV), F32)],
        scratch_shapes=[pltpu.VMEM((CONV_HALO + tt, D_CONV), F32), pltpu.VMEM((tt, D_CONV), F32),
                        pltpu.VMEM(shifted, F32)],
        compiler_params=_params(("parallel", "arbitrary"), blocks,
                                resident=[((CONV_HALO + tt, D_CONV), F32), ((tt, D_CONV), F32), (shifted, F32)],
                                temps=[((tt, D_CONV), F32)] * 4),
        name="conv_prompt",
    )(proj, proj, w, _as_rows(b), _as_rows(lng), _as_rows(lnb))


def _conv_sample_body(a_ref, g_ref, s_ref, w_ref, b_ref, lng_ref, lnb_ref, st_ref, act_ref, new_ref, y_ref,
                      *, nb, t):
    hist = CONV_K - 1
    u = (a_ref[...].astype(F32) * jax.nn.sigmoid(g_ref[...].astype(F32))).reshape(nb, t, D_CONV)
    for j in range(t):
        new_ref[j] = u[:, j, :]
    for r in range(hist):
        st_ref[r] = s_ref[r + t] if r + t < hist else new_ref[r + t - hist]
    for j in range(t):
        for lc in range(D_CONV // CONV_LANE_CHUNK):
            ls = pl.ds(lc * CONV_LANE_CHUNK, CONV_LANE_CHUNK)
            acc = jnp.broadcast_to(b_ref[:, ls], (nb, CONV_LANE_CHUNK))
            for k in range(CONV_K):
                r = j + k
                plane = s_ref[r, :, ls] if r < hist else new_ref[r - hist, :, ls]
                acc = acc + w_ref[k:k + 1, ls] * plane
            y_ref[j, :, ls] = acc
    y = _ln_silu(y_ref[...], lng_ref[...], lnb_ref[...])
    act_ref[...] = jnp.stack([y[j] for j in range(t)], axis=1).reshape(nb * t, D_CONV).astype(BF16)


def _conv_sample(proj, state, prev, n, t, w, b, lng, lnb, l, nb=32):
    depth = state.shape[0]
    hist = CONV_K - 1
    rows = nb * t
    vec = _layer_vec(l, D_CONV)
    st = pl.BlockSpec((None, hist, nb, D_CONV), lambda s: (l, 0, s, 0))
    blocks = [((rows, D_CONV), BF16)] * 3 + [((hist, nb, D_CONV), F32)] * 2
    scratch = [((t, nb, D_CONV), F32)] * 2
    return _stacked_call(
        functools.partial(_conv_sample_body, nb=nb, t=t), l, prev, 1,
        grid=(n // nb,),
        in_specs=[pl.BlockSpec((rows, D_CONV), lambda s: (s, COL_GLU_A // D_CONV)),
                  pl.BlockSpec((rows, D_CONV), lambda s: (s, COL_GLU_G // D_CONV)),
                  st, _layer_vec(l, D_CONV, CONV_K), vec, vec, vec],
        out_specs=[st, pl.BlockSpec((rows, D_CONV), lambda s: (s, 0))],
        out_shape=[jax.ShapeDtypeStruct((depth, hist, n, D_CONV), F32),
                   jax.ShapeDtypeStruct((n * t, D_CONV), BF16)],
        scratch_shapes=[pltpu.VMEM(s, d) for s, d in scratch],
        compiler_params=_params(("parallel",), blocks, resident=scratch,
                                temps=[((rows, D_CONV), F32)] * 4, keep_operands_in_hbm=True),
        name="conv_sample",
    )(proj, proj, state, w, _as_rows(b), _as_rows(lng), _as_rows(lnb))


def _softmax_terms(s, valid, distf, slope, sink):
    s = jnp.where(valid, s - slope * distf, NEG)
    m = jnp.maximum(jnp.max(s, axis=-1, keepdims=True), sink)
    p = jnp.exp(s - m)
    denom = jnp.sum(p, axis=-1, keepdims=True) + jnp.exp(sink - m)
    return p, denom


def _attn_prompt_body(sink_ref, q_ref, kc_ref, vc_ref, kp_ref, vp_ref, o_ref, *, l, nsub):
    i = pl.program_id(1)
    blk = WINDOW
    q_all = q_ref[...] * SCALE
    k_all = jnp.concatenate([kp_ref[...], kc_ref[...]], axis=0)
    v_all = jnp.concatenate([vp_ref[...], vc_ref[...]], axis=0)
    r = lax.broadcasted_iota(jnp.int32, (blk, 2 * blk), 0)
    c = lax.broadcasted_iota(jnp.int32, (blk, 2 * blk), 1)
    dist = blk + r - c
    in_window = (dist >= 0) & (dist <= WINDOW)
    distf = dist.astype(F32)
    for sb in range(nsub):
        q = q_all[sb * blk:(sb + 1) * blk]
        k = k_all[sb * blk:(sb + 2) * blk]
        v = v_all[sb * blk:(sb + 2) * blk]
        valid = in_window & ((c >= blk) | (i > 0)) if sb == 0 else in_window
        scores = []
        for kv in range(N_KV_HEADS):
            kh = k[:, kv * HEAD_DIM:(kv + 1) * HEAD_DIM]
            heads = [kv * GROUP + g for g in range(GROUP)]
            qs = jnp.concatenate([q[:, h * HEAD_DIM:(h + 1) * HEAD_DIM] for h in heads], axis=0)
            scores.append(lax.dot_general(qs, kh, (((1,), (1,)), ((), ())),
                                          preferred_element_type=F32))
        probs, denoms = [], []
        for kv in range(N_KV_HEADS):
            ps = []
            for g in range(GROUP):
                h = kv * GROUP + g
                p, d = _softmax_terms(scores[kv][g * blk:(g + 1) * blk], valid, distf, SLOPES[h], sink_ref[l, h])
                ps.append(p.astype(BF16))
                denoms.append(d)
            probs.append(jnp.concatenate(ps, axis=0))
        outs = [_dot(probs[kv], v[:, kv * HEAD_DIM:(kv + 1) * HEAD_DIM]) for kv in range(N_KV_HEADS)]
        for h in range(N_HEADS):
            kv, g = divmod(h, GROUP)
            o_ref[sb * blk:(sb + 1) * blk, h * HEAD_DIM:(h + 1) * HEAD_DIM] = (
                outs[kv][g * blk:(g + 1) * blk] / denoms[h]).astype(BF16)


def _attn_prompt(proj, sinks, n, t, l, nsub=1):
    blk = WINDOW
    rows = nsub * blk
    steps = t // rows
    kcol, vcol = COL_K // D_KV, COL_V // D_KV
    prev = lambda s, i: s * (t // blk) + jnp.maximum(nsub * i - 1, 0)
    blocks = ([((rows, D_ATTN), BF16)] * 2 + [((rows, D_KV), BF16)] * 2 + [((blk, D_KV), BF16)] * 2)
    return pl.pallas_call(
        functools.partial(_attn_prompt_body, l=l, nsub=nsub),
        grid=(n, steps),
        in_specs=[pl.BlockSpec(memory_space=pltpu.SMEM),
                  pl.BlockSpec((rows, D_ATTN), lambda s, i: (s * steps + i, COL_Q // D_ATTN)),
                  pl.BlockSpec((rows, D_KV), lambda s, i: (s * steps + i, kcol)),
                  pl.BlockSpec((rows, D_KV), lambda s, i: (s * steps + i, vcol)),
                  pl.BlockSpec((blk, D_KV), lambda s, i: (prev(s, i), kcol)),
                  pl.BlockSpec((blk, D_KV), lambda s, i: (prev(s, i), vcol))],
        out_specs=pl.BlockSpec((rows, D_ATTN), lambda s, i: (s * steps + i, 0)),
        out_shape=jax.ShapeDtypeStruct((n * t, D_ATTN), BF16),
        compiler_params=_params(("parallel", "parallel"), blocks,
                                temps=[((GROUP * blk, 2 * blk), F32)] * 16),
        name="attn_prompt",
    )(sinks, proj, proj, proj, proj, proj)


def _attn_sample_body(sink_ref, q_ref, kn_ref, vn_ref, ck_ref, cv_ref, nk_ref, nv_ref, o_ref, *, nb, t, l):
    eye = (lax.broadcasted_iota(jnp.int32, (D_KV, D_KV), 0)
           == lax.broadcasted_iota(jnp.int32, (D_KV, D_KV), 1)).astype(BF16)
    lane = lax.broadcasted_iota(jnp.int32, (D_KV, WINDOW), 1)
    for new_ref, cache_ref, out_ref in ((kn_ref, ck_ref, nk_ref), (vn_ref, cv_ref, nv_ref)):
        tr = lax.dot_general(eye, new_ref[...], (((1,), (1,)), ((), ())), preferred_element_type=F32)
        for s in range(nb):
            kept = pltpu.roll(cache_ref[s], WINDOW - t, axis=1)
            fresh = pltpu.roll(tr, (WINDOW - t - s * t) % WINDOW, axis=1)
            out_ref[s] = jnp.where(lane >= WINDOW - t, fresh, kept)
    q3 = (q_ref[...].astype(F32) * SCALE).reshape(nb, t, D_ATTN)
    rows = GROUP * t
    tq = lax.broadcasted_iota(jnp.int32, (rows, WINDOW), 0) % t
    w = lax.broadcasted_iota(jnp.int32, (rows, WINDOW), 1)
    dist_old = WINDOW + tq - w
    valid_old = (w < t) & (w >= tq)
    dist_new = WINDOW - t + tq - w
    valid_new = dist_new >= 0
    gi = lax.broadcasted_iota(jnp.int32, (rows, 1), 0) // t
    for kv in range(N_KV_HEADS):
        heads = [kv * GROUP + g for g in range(GROUP)]
        slope = jnp.zeros((rows, 1), F32)
        sink = jnp.zeros((rows, 1), F32)
        for g, h in enumerate(heads):
            slope = jnp.where(gi == g, SLOPES[h], slope)
            sink = jnp.where(gi == g, sink_ref[l, h], sink)
        hs = slice(kv * HEAD_DIM, (kv + 1) * HEAD_DIM)
        qs = jnp.concatenate([q3[:, :, h * HEAD_DIM:(h + 1) * HEAD_DIM] for h in heads], axis=1).astype(BF16)
        s_old = jnp.einsum("bqd,bdk->bqk", qs, ck_ref[:, hs, :].astype(BF16), preferred_element_type=F32)
        s_new = jnp.einsum("bqd,bdk->bqk", qs, nk_ref[:, hs, :].astype(BF16), preferred_element_type=F32)
        s_old = jnp.where(valid_old, s_old - slope * dist_old.astype(F32), NEG)
        s_new = jnp.where(valid_new, s_new - slope * dist_new.astype(F32), NEG)
        m = jnp.maximum(jnp.maximum(jnp.max(s_old, axis=-1, keepdims=True),
                                    jnp.max(s_new, axis=-1, keepdims=True)), sink)
        p_old = jnp.exp(s_old - m)
        p_new = jnp.exp(s_new - m)
        denom = (jnp.sum(p_old, axis=-1, keepdims=True) + jnp.sum(p_new, axis=-1, keepdims=True)
                 + jnp.exp(sink - m))
        o = (jnp.einsum("bqk,bdk->bqd", p_old.astype(BF16), cv_ref[:, hs, :].astype(BF16),
                        preferred_element_type=F32)
             + jnp.einsum("bqk,bdk->bqd", p_new.astype(BF16), nv_ref[:, hs, :].astype(BF16),
                          preferred_element_type=F32)) / denom
        for g, h in enumerate(heads):
            o_ref[:, h * HEAD_DIM:(h + 1) * HEAD_DIM] = (
                o[:, g * t:(g + 1) * t, :].reshape(nb * t, HEAD_DIM).astype(BF16))


def _attn_sample(proj, cache_k, cache_v, prev, sinks, n, t, l):
    depth = cache_k.shape[0]
    nb = WINDOW // t
    rows = nb * t
    kcol, vcol = COL_K // D_KV, COL_V // D_KV
    cache = pl.BlockSpec((None, nb, D_KV, WINDOW), lambda s: (l, s, 0, 0))
    stacked = jax.ShapeDtypeStruct((depth, n, D_KV, WINDOW), F32)
    blocks = [((rows, D_ATTN), BF16)] * 2 + [((rows, D_KV), BF16)] * 2 + [((nb, D_KV, WINDOW), F32)] * 4
    return _stacked_call(
        functools.partial(_attn_sample_body, nb=nb, t=t, l=l), l, prev, 2,
        grid=(n // nb,),
        in_specs=[pl.BlockSpec(memory_space=pltpu.SMEM),
                  pl.BlockSpec((rows, D_ATTN), lambda s: (s, COL_Q // D_ATTN)),
                  pl.BlockSpec((rows, D_KV), lambda s: (s, kcol)),
                  pl.BlockSpec((rows, D_KV), lambda s: (s, vcol)),
                  cache, cache],
        out_specs=[cache, cache, pl.BlockSpec((rows, D_ATTN), lambda s: (s, 0))],
        out_shape=[stacked, stacked, jax.ShapeDtypeStruct((n * t, D_ATTN), BF16)],
        compiler_params=_params(("parallel",), blocks,
                                temps=[((nb, D_KV, WINDOW), F32)] * 4 + [((nb, GROUP * t, WINDOW), F32)] * 8,
                                keep_operands_in_hbm=True),
        name="attn_sample",
    )(sinks, proj, proj, proj, cache_k, cache_v)


def _mix_body(c_ref, a_ref, *refs):
    n_gate = D_MODEL // GATE_BLOCK
    gc_refs, ga_refs = refs[:n_gate], refs[n_gate:2 * n_gate]
    x_ref, wco_ref, wao_ref, wo_ref, n2_ref, x1_ref, h2_ref = refs[2 * n_gate:]
    gate_c = jnp.concatenate([g[...] for g in gc_refs], axis=1).astype(F32)
    gate_a = jnp.concatenate([g[...] for g in ga_refs], axis=1).astype(F32)
    branch_c = _dot(c_ref[...], wco_ref[...])
    branch_a = _dot(a_ref[...], wao_ref[...])
    merged = jax.nn.sigmoid(gate_c) * branch_c + jax.nn.sigmoid(gate_a) * branch_a
    x1 = x_ref[...] + _dot(merged.astype(BF16), wo_ref[...])
    x1_ref[...] = x1
    h2_ref[...] = _rms(x1, n2_ref[...]).astype(BF16)


def _mix(c_act, a_act, proj, x, wco, wao, wo, n2, l, tm=256):
    m = x.shape[0]
    row = lambda width, col: pl.BlockSpec((tm, width), lambda i: (i, col))
    const = lambda k: pl.BlockSpec((None, k, D_MODEL), lambda i: (l, 0, 0), pipeline_mode=pl.Buffered(1))
    blocks = ([((tm, D_CONV), BF16), ((tm, D_ATTN), BF16)] + [((tm, D_MODEL), F32)] * 2 + [((tm, D_MODEL), BF16)] * 3)
    weights = [((D_CONV, D_MODEL), BF16), ((D_ATTN, D_MODEL), BF16), ((D_MODEL, D_MODEL), BF16)]
    return pl.pallas_call(
        _mix_body,
        grid=(m // tm,),
        in_specs=[row(D_CONV, 0), row(D_ATTN, 0)]
                 + [row(GATE_BLOCK, COL_GATE_C // GATE_BLOCK + b) for b in range(D_MODEL // GATE_BLOCK)]
                 + [row(GATE_BLOCK, COL_GATE_A // GATE_BLOCK + b) for b in range(D_MODEL // GATE_BLOCK)]
                 + [row(D_MODEL, 0),
                  const(D_CONV), const(D_ATTN), const(D_MODEL),
                  _layer_vec(l, D_MODEL)],
        out_specs=[row(D_MODEL, 0), row(D_MODEL, 0)],
        out_shape=[jax.ShapeDtypeStruct((m, D_MODEL), F32), jax.ShapeDtypeStruct((m, D_MODEL), BF16)],
        compiler_params=_params(("parallel",), blocks, resident=weights, temps=[((tm, D_MODEL), F32)] * 4),
        name="mix",
    )(c_act, a_act, *([proj] * (2 * (D_MODEL // GATE_BLOCK))), x, wco, wao, wo, _as_rows(n2))


def _ffn_conv(e_ref, w_ref, b_ref, tm):
    first = FFN_HALO - (FFN_K - 1)
    out = b_ref[...]
    for j in range(FFN_K):
        out = out + w_ref[j:j + 1, :] * e_ref[first + j:first + j + tm, :]
    return out


def _up_prompt_body(h_ref, wg_ref, wv_ref, cwg_ref, cwv_ref, cbg_ref, cbv_ref, act_ref, sg_ref, sv_ref,
                    eg_ref, ev_ref, wgb_ref, wvb_ref, *, tm, tiles_per_seq):
    i = pl.program_id(1)
    pos = i % tiles_per_seq

    @pl.when(i == 0)
    def _():
        wgb_ref[...] = wg_ref[...].astype(BF16)
        wvb_ref[...] = wv_ref[...].astype(BF16)

    @pl.when(pos == 0)
    def _():
        eg_ref[0:FFN_HALO, :] = jnp.zeros((FFN_HALO, eg_ref.shape[1]), F32)
        ev_ref[0:FFN_HALO, :] = jnp.zeros((FFN_HALO, ev_ref.shape[1]), F32)

    h = h_ref[...]
    eg_ref[FFN_HALO:FFN_HALO + tm, :] = _dot(h, wgb_ref[...])
    ev_ref[FFN_HALO:FFN_HALO + tm, :] = _dot(h, wvb_ref[...])
    gate = _ffn_conv(eg_ref, cwg_ref, cbg_ref, tm)
    val = _ffn_conv(ev_ref, cwv_ref, cbv_ref, tm)
    act_ref[...] = (gate * jax.nn.sigmoid(gate) * val).astype(BF16)

    @pl.when(pos == tiles_per_seq - 1)
    def _():
        hist = FFN_K - 1
        sg_ref[0] = eg_ref[FFN_HALO + tm - hist:FFN_HALO + tm, :]
        sv_ref[0] = ev_ref[FFN_HALO + tm - hist:FFN_HALO + tm, :]

    eg_ref[0:FFN_HALO, :] = eg_ref[tm:tm + FFN_HALO, :]
    ev_ref[0:FFN_HALO, :] = ev_ref[tm:tm + FFN_HALO, :]


def _up_prompt(h2, n, t, w_up, cw, cb, l, tm=1024, tile=FFN_PROMPT_TILE):
    m = n * t
    tps = t // tm
    nt = D_FF // tile
    hist = FFN_K - 1
    col = lambda rows, off: pl.BlockSpec((None, rows, tile), lambda j, i: (l, 0, j + off))
    blocks = ([((tm, D_MODEL), BF16)] + [((D_MODEL, tile), F32)] * 2 + [((tm, tile), BF16)])
    scratch = [((FFN_HALO + tm, tile), F32)] * 2 + [((D_MODEL, tile), BF16)] * 2
    state = pl.BlockSpec((1, hist, tile), lambda j, i: (i // tps, 0, j))
    return pl.pallas_call(
        functools.partial(_up_prompt_body, tm=tm, tiles_per_seq=tps),
        grid=(nt, m // tm),
        in_specs=[pl.BlockSpec((tm, D_MODEL), lambda j, i: (i, 0)),
                  col(D_MODEL, 0), col(D_MODEL, nt), col(FFN_K, 0), col(FFN_K, nt), col(1, 0), col(1, nt)],
        out_specs=[pl.BlockSpec((tm, tile), lambda j, i: (i, j)), state, state],
        out_shape=[jax.ShapeDtypeStruct((m, D_FF), BF16),
                   jax.ShapeDtypeStruct((n, hist, D_FF), F32), jax.ShapeDtypeStruct((n, hist, D_FF), F32)],
        scratch_shapes=[pltpu.VMEM(s, d) for s, d in scratch],
        compiler_params=_params(("parallel", "arbitrary"), blocks, resident=scratch,
                                temps=[((tm, tile), F32)] * 4),
        name="up_prompt",
    )(h2, w_up, w_up, cw, cw, _as_rows(cb), _as_rows(cb))


def _ffn_conv_sample(u3, st_ref, w_ref, b_ref, tpos):
    st0 = st_ref[:, 0:1, :]
    st1 = st_ref[:, 1:2, :]
    prev1 = jnp.where(tpos == 0, st1, pltpu.roll(u3, 1, axis=1))
    prev2 = jnp.where(tpos == 0, st0, jnp.where(tpos == 1, st1, pltpu.roll(u3, 2, axis=1)))
    return b_ref[...] + w_ref[0:1, :] * prev2 + w_ref[1:2, :] * prev1 + w_ref[2:3, :] * u3


def _up_sample_body(h_ref, wg_ref, wv_ref, cwg_ref, cwv_ref, cbg_ref, cbv_ref, stg_ref, stv_ref,
                    s_ref, act_ref, *, n, t):
    hist = FFN_K - 1
    h = h_ref[...]
    ug = _dot(h, wg_ref[...].astype(BF16)).reshape(n, t, FFN_TILE)
    uv = _dot(h, wv_ref[...].astype(BF16)).reshape(n, t, FFN_TILE)
    tpos = lax.broadcasted_iota(jnp.int32, (1, t, 1), 1)
    gate = _ffn_conv_sample(ug, stg_ref, cwg_ref, cbg_ref, tpos)
    val = _ffn_conv_sample(uv, stv_ref, cwv_ref, cbv_ref, tpos)
    act_ref[...] = (gate * jax.nn.sigmoid(gate) * val).reshape(n * t, FFN_TILE).astype(BF16)
    s_ref[:, 0] = ug[:, t - hist:t, :]
    s_ref[:, 1] = uv[:, t - hist:t, :]


def _up_sample(h2, state, prev, n, t, w_up, cw, cb, l):
    assert t == V7X_SUBLANES and FFN_K == 3
    depth = state.shape[0]
    m = n * t
    nt = D_FF // FFN_TILE
    hist = FFN_K - 1
    col = lambda rows, off: pl.BlockSpec((None, rows, FFN_TILE), lambda j: (l, 0, j + off))
    st = lambda off: pl.BlockSpec((None, n, hist, FFN_TILE), lambda j: (l, 0, 0, j + off))
    new_state = pl.BlockSpec((None, n, 2, hist, FFN_TILE), lambda j: (l, 0, 0, 0, j))
    blocks = ([((D_MODEL, FFN_TILE), F32)] * 2 + [((n, V7X_SUBLANES, FFN_TILE), F32)] * 4
              + [((m, FFN_TILE), BF16)])
    return _stacked_call(
        functools.partial(_up_sample_body, n=n, t=t), l, prev, 1,
        grid=(nt,),
        in_specs=[pl.BlockSpec((m, D_MODEL), lambda j: (0, 0)),
                  col(D_MODEL, 0), col(D_MODEL, nt), col(FFN_K, 0), col(FFN_K, nt), col(1, 0), col(1, nt),
                  st(0), st(nt)],
        out_specs=[new_state, pl.BlockSpec((m, FFN_TILE), lambda j: (0, j))],
        out_shape=[jax.ShapeDtypeStruct((depth, n, 2, hist, D_FF), F32), jax.ShapeDtypeStruct((m, D_FF), BF16)],
        compiler_params=_params(("parallel",), blocks,
                                resident=[((m, D_MODEL), BF16)] * 2 + [((D_MODEL, FFN_TILE), BF16)] * 2,
                                temps=[((m, FFN_TILE), F32)] * 8),
        name="up_sample",
    )(h2, w_up, w_up, cw, cw, _as_rows(cb), _as_rows(cb), state, state)


def _down_body(act_ref, w_ref, x_ref, o_ref, wb_ref):
    @pl.when(pl.program_id(1) == 0)
    def _():
        wb_ref[...] = w_ref[...].astype(BF16)

    o_ref[...] = x_ref[...] + _dot(act_ref[...], wb_ref[...])


def _down(act, w_down, x1, l, tm=512):
    m = x1.shape[0]
    blocks = [((tm, D_FF), BF16), ((D_FF, DOWN_TILE), F32), ((tm, DOWN_TILE), F32), ((tm, DOWN_TILE), F32)]
    return pl.pallas_call(
        _down_body,
        grid=(D_MODEL // DOWN_TILE, m // tm),
        in_specs=[pl.BlockSpec((tm, D_FF), lambda j, i: (i, 0)),
                  pl.BlockSpec((None, D_FF, DOWN_TILE), lambda j, i: (l, 0, j)),
                  pl.BlockSpec((tm, DOWN_TILE), lambda j, i: (i, j))],
        out_specs=pl.BlockSpec((tm, DOWN_TILE), lambda j, i: (i, j)),
        out_shape=jax.ShapeDtypeStruct((m, D_MODEL), F32),
        scratch_shapes=[pltpu.VMEM((D_FF, DOWN_TILE), BF16)],
        compiler_params=_params(("parallel", "arbitrary"), blocks, resident=[((D_FF, DOWN_TILE), BF16)],
                                temps=[((tm, DOWN_TILE), F32)]),
        name="down",
    )(act, w_down, x1)


def _final_norm_body(x_ref, g_ref, o_ref):
    o_ref[...] = _rms(x_ref[...], g_ref[...])


def _final_norm(x, g, tm=512):
    m = x.shape[0]
    blocks = [((tm, D_MODEL), F32)] * 2
    return pl.pallas_call(
        _final_norm_body,
        grid=(m // tm,),
        in_specs=[pl.BlockSpec((tm, D_MODEL), lambda i: (i, 0)), pl.BlockSpec((1, D_MODEL), lambda i: (0, 0))],
        out_specs=pl.BlockSpec((tm, D_MODEL), lambda i: (i, 0)),
        out_shape=jax.ShapeDtypeStruct((m, D_MODEL), F32),
        compiler_params=_params(("parallel",), blocks, temps=[((tm, D_MODEL), F32)]),
        name="final_norm",
    )(x, g.reshape(1, D_MODEL))


def kernel(x_prompt, x_sample, cache_k, cache_v, state_conv, state_ffn_conv, norm1_g, w_in, conv_w, conv_b,
           conv_ln_g, conv_ln_b, w_conv_out, attn_sinks, w_attn_out, w_out, norm2_g, w_up, ffn_conv_w,
           ffn_conv_b, w_down, final_norm_g):
    depth = w_in.shape[0]
    n_p, t_p, _ = x_prompt.shape
    n_s, t_s, _ = x_sample.shape
    xp = x_prompt.reshape(n_p * t_p, D_MODEL)
    xs = x_sample.reshape(n_s * t_s, D_MODEL)
    w_co_b, w_ao_b, w_o_b = (w.astype(BF16) for w in (w_conv_out, w_attn_out, w_out))
    keys_on_lanes = lambda c: jnp.transpose(c, (0, 1, 3, 4, 2)).reshape(depth, n_s, D_KV, WINDOW)
    keys_on_rows = lambda c: jnp.transpose(c.reshape(depth, n_s, N_KV_HEADS, HEAD_DIM, WINDOW), (0, 1, 4, 2, 3))
    cache_k, cache_v = keys_on_lanes(cache_k), keys_on_lanes(cache_v)
    rows_first = lambda s: jnp.transpose(s, (0, 2, 1, 3))
    state_conv = rows_first(state_conv)
    outs = {k: [] for k in ("kp", "vp", "cp", "fp")}
    c_states, kv_states, f_states = (), (), ()
    for l in range(depth):
        conv_args = (conv_w, conv_b, conv_ln_g, conv_ln_b, l)
        proj = _inproj(xp, norm1_g, w_in, l, tm=1024)
        c_act, c_state = _conv_prompt(proj, n_p, t_p, *conv_args)
        a_act = _attn_prompt(proj, attn_sinks, n_p, t_p, l)
        x1, h2 = _mix(c_act, a_act, proj, xp, w_co_b, w_ao_b, w_o_b, norm2_g, l)
        act, f_g, f_v = _up_prompt(h2, n_p, t_p, w_up, ffn_conv_w, ffn_conv_b, l)
        xp = _down(act, w_down, x1, l)
        kv = proj.reshape(n_p, t_p, IN_COLS)[:, t_p - WINDOW:, :]
        outs["kp"].append(kv[:, :, COL_K:COL_K + D_KV].astype(F32).reshape(n_p, WINDOW, N_KV_HEADS, HEAD_DIM))
        outs["vp"].append(kv[:, :, COL_V:COL_V + D_KV].astype(F32).reshape(n_p, WINDOW, N_KV_HEADS, HEAD_DIM))
        outs["cp"].append(c_state)
        outs["fp"].append(jnp.concatenate([f_g, f_v], axis=-1))
        proj = _inproj(xs, norm1_g, w_in, l, tm=n_s * t_s)
        c_stack, c_act = _conv_sample(proj, state_conv, c_states, n_s, t_s, *conv_args)
        k_stack, v_stack, a_act = _attn_sample(proj, cache_k, cache_v, kv_states, attn_sinks, n_s, t_s, l)
        c_states, kv_states = (c_stack,), (k_stack, v_stack)
        x1, h2 = _mix(c_act, a_act, proj, xs, w_co_b, w_ao_b, w_o_b, norm2_g, l)
        f_stack, act = _up_sample(h2, state_ffn_conv, f_states, n_s, t_s, w_up, ffn_conv_w, ffn_conv_b, l)
        f_states = (f_stack,)
        xs = _down(act, w_down, x1, l)
    y_prompt = _final_norm(xp, final_norm_g).reshape(n_p, t_p, D_MODEL)
    y_sample = _final_norm(xs, final_norm_g).reshape(n_s, t_s, D_MODEL)
    return (y_prompt, y_sample, jnp.stack(outs["kp"]), jnp.stack(outs["vp"]), jnp.stack(outs["cp"]),
            jnp.stack(outs["fp"]), keys_on_rows(k_stack), keys_on_rows(v_stack), rows_first(c_stack),
            jnp.transpose(f_stack, (0, 1, 3, 2, 4)).reshape(depth, n_s, FFN_K - 1, 2 * D_FF))
```

```python
import functools

import jax
import jax.numpy as jnp
from jax import lax
from jax.experimental import pallas as pl
from jax.experimental.pallas import tpu as pltpu

F32 = jnp.float32
BF16 = jnp.bfloat16

D_MODEL = 2048
HEAD_DIM = 64
N_HEADS = 16
N_KV_HEADS = 4
GROUP = N_HEADS // N_KV_HEADS
D_ATTN = N_HEADS * HEAD_DIM
D_KV = N_KV_HEADS * HEAD_DIM
WINDOW = 128
D_CONV = D_MODEL // 2
CONV_K = 31
D_FF = 3 * D_MODEL
FFN_K = 3
EPS = 1e-6
IN_COLS = 2 * D_CONV + D_ATTN + 2 * D_KV + 2 * D_MODEL
NEG = -1e30
SCALE = HEAD_DIM ** -0.5
SLOPES = tuple(2.0 ** (-8.0 * (h + 1) / N_HEADS) for h in range(N_HEADS))

V7X_SUBLANES = 8
V7X_SCOPED_VMEM_CAP_BYTES = 60000 * 1024

IN_TILE = 1280
COL_GLU_A = 0
COL_GLU_G = D_CONV
COL_Q = 2 * D_CONV
COL_K = COL_Q + D_ATTN
COL_V = COL_K + D_KV
COL_GATE_C = COL_V + D_KV
COL_GATE_A = COL_GATE_C + D_MODEL
GATE_BLOCK = 512

CONV_HALO = 32
CONV_ROW_CHUNK = 64
CONV_LANE_CHUNK = 256
FFN_HALO = V7X_SUBLANES
FFN_TILE = 512
FFN_PROMPT_TILE = 512
DOWN_TILE = 512


def _nbytes(shape, dtype):
    n = 1
    for s in shape:
        n *= s
    return n * jnp.dtype(dtype).itemsize


def _params(semantics, pipelined, resident=(), temps=(), keep_operands_in_hbm=True):
    est = 2 * sum(_nbytes(s, d) for s, d in pipelined)
    est += sum(_nbytes(s, d) for s, d in resident)
    est += sum(_nbytes(s, d) for s, d in temps)
    limit = min(V7X_SCOPED_VMEM_CAP_BYTES, est + est // 4)
    if keep_operands_in_hbm:
        limit = V7X_SCOPED_VMEM_CAP_BYTES
    return pltpu.CompilerParams(dimension_semantics=semantics, vmem_limit_bytes=limit)


def _layer_vec(l, width, rows=1):
    return pl.BlockSpec((None, rows, width), lambda *_: (l, 0, 0))


def _as_rows(p):
    return p.reshape(p.shape[0], 1, p.shape[1])


def _rms(x, g):
    return x * lax.rsqrt(jnp.mean(x * x, axis=-1, keepdims=True) + EPS) * g


def _dot(a, b):
    return jnp.dot(a, b, preferred_element_type=F32)


def _skip_first_ref(body):
    def wrapped(_, *refs):
        body(*refs)
    return wrapped


def _stacked_call(body, l, prev, n_alias_out, **kw):
    if l == 0:
        return pl.pallas_call(body, **kw)
    for _ in prev:
        body = _skip_first_ref(body)
    kw["in_specs"] = [pl.BlockSpec(memory_space=pl.ANY)] * len(prev) + list(kw["in_specs"])
    call = pl.pallas_call(body, input_output_aliases={i: i for i in range(n_alias_out)}, **kw)
    return lambda *args: call(*prev, *args)


def _inproj_body(x_ref, g_ref, w_ref, o_ref, *refs, cast):
    h_ref = refs[-1]

    @pl.when(pl.program_id(1) == 0)
    def _():
        h_ref[...] = _rms(x_ref[...], g_ref[...]).astype(BF16)

    if cast:
        w16_ref = refs[0]
        w16_ref[...] = w_ref[...].astype(BF16)
        w_ref = w16_ref
    o_ref[...] = _dot(h_ref[...], w_ref[...]).astype(BF16)


def _inproj(x, g, w, l, tm):
    m = x.shape[0]
    cast = w.dtype == F32
    assert not cast or m == tm
    w_spec = (pl.BlockSpec((None, D_MODEL, IN_TILE), lambda i, j: (l, 0, j)) if cast
              else pl.BlockSpec((D_MODEL, IN_TILE), lambda i, j: (0, j)))
    out_specs = [pl.BlockSpec((tm, IN_TILE), lambda i, j: (i, j))]
    out_shape = [jax.ShapeDtypeStruct((m, IN_COLS), BF16)]
    blocks = [((tm, D_MODEL), F32), ((D_MODEL, IN_TILE), w.dtype), ((tm, IN_TILE), BF16)]
    if cast:
        out_specs.append(pl.BlockSpec((D_MODEL, IN_TILE), lambda i, j: (0, j)))
        out_shape.append(jax.ShapeDtypeStruct((D_MODEL, IN_COLS), BF16))
        blocks.append(((D_MODEL, IN_TILE), BF16))
    outs = pl.pallas_call(
        functools.partial(_inproj_body, cast=cast),
        grid=(m // tm, IN_COLS // IN_TILE),
        in_specs=[pl.BlockSpec((tm, D_MODEL), lambda i, j: (i, 0)), _layer_vec(l, D_MODEL), w_spec],
        out_specs=out_specs,
        out_shape=out_shape,
        scratch_shapes=[pltpu.VMEM((tm, D_MODEL), BF16)],
        compiler_params=_params(("parallel", "arbitrary"), blocks, resident=[((tm, D_MODEL), BF16)],
                                temps=[((tm, D_MODEL), F32)]),
        name="inproj",
    )(x, _as_rows(g), w)
    return outs if cast else outs[0]


def _ln_silu(c, g, b):
    mu = jnp.mean(c, axis=-1, keepdims=True)
    xc = c - mu
    y = xc * lax.rsqrt(jnp.mean(xc * xc, axis=-1, keepdims=True) + EPS) * g + b
    return y * jax.nn.sigmoid(y)


def _conv_prompt_body(a_ref, g_ref, w_ref, b_ref, lng_ref, lnb_ref, act_ref, st_ref, xx_ref, cv_ref, sh_ref,
                      *, tt):
    i = pl.program_id(1)

    @pl.when(i == 0)
    def _():
        xx_ref[0:CONV_HALO, :] = jnp.zeros((CONV_HALO, D_CONV), F32)

    xx_ref[CONV_HALO:CONV_HALO + tt, :] = a_ref[...].astype(F32) * jax.nn.sigmoid(g_ref[...].astype(F32))

    first = CONV_HALO - (CONV_K - 1)
    sh_rows = sh_ref.shape[1]
    for lc in range(D_CONV // CONV_LANE_CHUNK):
        ls = pl.ds(lc * CONV_LANE_CHUNK, CONV_LANE_CHUNK)
        for r in range(1, V7X_SUBLANES):
            sh_ref[r - 1] = xx_ref[r:r + sh_rows, ls]
        for r0 in range(0, tt, CONV_ROW_CHUNK):
            acc = jnp.broadcast_to(b_ref[:, ls], (CONV_ROW_CHUNK, CONV_LANE_CHUNK))
            for j in range(CONV_K):
                q, r = divmod(first + j, V7X_SUBLANES)
                a0 = r0 + V7X_SUBLANES * q
                if r == 0:
                    window = xx_ref[a0:a0 + CONV_ROW_CHUNK, ls]
                else:
                    window = sh_ref[r - 1, a0:a0 + CONV_ROW_CHUNK, :]
                acc = acc + w_ref[j:j + 1, ls] * window
            cv_ref[r0:r0 + CONV_ROW_CHUNK, ls] = acc
    act_ref[...] = _ln_silu(cv_ref[...], lng_ref[...], lnb_ref[...]).astype(BF16)

    @pl.when(i == pl.num_programs(1) - 1)
    def _():
        st_ref[0] = xx_ref[tt + first:tt + CONV_HALO, :]

    xx_ref[0:CONV_HALO, :] = xx_ref[tt:tt + CONV_HALO, :]


def _conv_prompt(proj, n, t, w, b, lng, lnb, l, tt=512):
    tps = t // tt
    vec = _layer_vec(l, D_CONV)
    blocks = [((tt, D_CONV), BF16)] * 3
    shifted = (V7X_SUBLANES - 1, tt + CONV_HALO - V7X_SUBLANES, CONV_LANE_CHUNK)
    return pl.pallas_call(
        functools.partial(_conv_prompt_body, tt=tt),
        grid=(n, tps),
        in_specs=[pl.BlockSpec((tt, D_CONV), lambda s, i: (s * tps + i, COL_GLU_A // D_CONV)),
                  pl.BlockSpec((tt, D_CONV), lambda s, i: (s * tps + i, COL_GLU_G // D_CONV)),
                  _layer_vec(l, D_CONV, CONV_K), vec, vec, vec],
        out_specs=[pl.BlockSpec((tt, D_CONV), lambda s, i: (s * tps + i, 0)),
                   pl.BlockSpec((1, CONV_K - 1, D_CONV), lambda s, i: (s, 0, 0))],
        out_shape=[jax.ShapeDtypeStruct((n * t, D_CONV), BF16),
                   jax.ShapeDtypeStruct((n, CONV_K - 1, D_CONV), F32)],
        scratch_shapes=[pltpu.VMEM((CONV_HALO + tt, D_CONV), F32), pltpu.VMEM((tt, D_CONV), F32),
                        pltpu.VMEM(shifted, F32)],
        compiler_params=_params(("parallel", "arbitrary"), blocks,
                                resident=[((CONV_HALO + tt, D_CONV), F32), ((tt, D_CONV), F32), (shifted, F32)],
                                temps=[((tt, D_CONV), F32)] * 4),
        name="conv_prompt",
    )(proj, proj, w, _as_rows(b), _as_rows(lng), _as_rows(lnb))


def _conv_sample_body(a_ref, g_ref, s_ref, w_ref, b_ref, lng_ref, lnb_ref, st_ref, act_ref, new_ref, y_ref,
                      *, nb, t):
    hist = CONV_K - 1
    u = (a_ref[...].astype(F32) * jax.nn.sigmoid(g_ref[...].astype(F32))).reshape(nb, t, D_CONV)
    for j in range(t):
        new_ref[j] = u[:, j, :]
    for r in range(hist):
        st_ref[r] = s_ref[r + t] if r + t < hist else new_ref[r + t - hist]
    for j in range(t):
        for lc in range(D_CONV // CONV_LANE_CHUNK):
            ls = pl.ds(lc * CONV_LANE_CHUNK, CONV_LANE_CHUNK)
            acc = jnp.broadcast_to(b_ref[:, ls], (nb, CONV_LANE_CHUNK))
            for k in range(CONV_K):
                r = j + k
                plane = s_ref[r, :, ls] if r < hist else new_ref[r - hist, :, ls]
                acc = acc + w_ref[k:k + 1, ls] * plane
            y_ref[j, :, ls] = acc
    y = _ln_silu(y_ref[...], lng_ref[...], lnb_ref[...])
    act_ref[...] = jnp.stack([y[j] for j in range(t)], axis=1).reshape(nb * t, D_CONV).astype(BF16)


def _conv_sample(proj, state, prev, n, t, w, b, lng, lnb, l, nb=32):
    depth = state.shape[0]
    hist = CONV_K - 1
    rows = nb * t
    vec = _layer_vec(l, D_CONV)
    st = pl.BlockSpec((None, hist, nb, D_CONV), lambda s: (l, 0, s, 0))
    blocks = [((rows, D_CONV), BF16)] * 3 + [((hist, nb, D_CONV), F32)] * 2
    scratch = [((t, nb, D_CONV), F32)] * 2
    return _stacked_call(
        functools.partial(_conv_sample_body, nb=nb, t=t), l, prev, 1,
        grid=(n // nb,),
        in_specs=[pl.BlockSpec((rows, D_CONV), lambda s: (s, COL_GLU_A // D_CONV)),
                  pl.BlockSpec((rows, D_CONV), lambda s: (s, COL_GLU_G // D_CONV)),
                  st, _layer_vec(l, D_CONV, CONV_K), vec, vec, vec],
        out_specs=[st, pl.BlockSpec((rows, D_CONV), lambda s: (s, 0))],
        out_shape=[jax.ShapeDtypeStruct((depth, hist, n, D_CONV), F32),
                   jax.ShapeDtypeStruct((n * t, D_CONV), BF16)],
        scratch_shapes=[pltpu.VMEM(s, d) for s, d in scratch],
        compiler_params=_params(("parallel",), blocks, resident=scratch,
                                temps=[((rows, D_CONV), F32)] * 4, keep_operands_in_hbm=True),
        name="conv_sample",
    )(proj, proj, state, w, _as_rows(b), _as_rows(lng), _as_rows(lnb))


def _softmax_terms(s, valid, distf, slope, sink):
    s = jnp.where(valid, s - slope * distf, NEG)
    m = jnp.maximum(jnp.max(s, axis=-1, keepdims=True), sink)
    p = jnp.exp(s - m)
    denom = jnp.sum(p, axis=-1, keepdims=True) + jnp.exp(sink - m)
    return p, denom


def _attn_prompt_body(sink_ref, q_ref, kc_ref, vc_ref, kp_ref, vp_ref, o_ref, *, l, nsub):
    i = pl.program_id(1)
    blk = WINDOW
    q_all = q_ref[...] * SCALE
    k_all = jnp.concatenate([kp_ref[...], kc_ref[...]], axis=0)
    v_all = jnp.concatenate([vp_ref[...], vc_ref[...]], axis=0)
    r = lax.broadcasted_iota(jnp.int32, (blk, 2 * blk), 0)
    c = lax.broadcasted_iota(jnp.int32, (blk, 2 * blk), 1)
    dist = blk + r - c
    in_window = (dist >= 0) & (dist <= WINDOW)
    distf = dist.astype(F32)
    for sb in range(nsub):
        q = q_all[sb * blk:(sb + 1) * blk]
        k = k_all[sb * blk:(sb + 2) * blk]
        v = v_all[sb * blk:(sb + 2) * blk]
        valid = in_window & ((c >= blk) | (i > 0)) if sb == 0 else in_window
        scores = []
        for kv in range(N_KV_HEADS):
            kh = k[:, kv * HEAD_DIM:(kv + 1) * HEAD_DIM]
            heads = [kv * GROUP + g for g in range(GROUP)]
            qs = jnp.concatenate([q[:, h * HEAD_DIM:(h + 1) * HEAD_DIM] for h in heads], axis=0)
            scores.append(lax.dot_general(qs, kh, (((1,), (1,)), ((), ())),
                                          preferred_element_type=F32))
        probs, denoms = [], []
        for kv in range(N_KV_HEADS):
            ps = []
            for g in range(GROUP):
                h = kv * GROUP + g
                p, d = _softmax_terms(scores[kv][g * blk:(g + 1) * blk], valid, distf, SLOPES[h], sink_ref[l, h])
                ps.append(p.astype(BF16))
                denoms.append(d)
            probs.append(jnp.concatenate(ps, axis=0))
        outs = [_dot(probs[kv], v[:, kv * HEAD_DIM:(kv + 1) * HEAD_DIM]) for kv in range(N_KV_HEADS)]
        for h in range(N_HEADS):
            kv, g = divmod(h, GROUP)
            o_ref[sb * blk:(sb + 1) * blk, h * HEAD_DIM:(h + 1) * HEAD_DIM] = (
                outs[kv][g * blk:(g + 1) * blk] / denoms[h]).astype(BF16)


def _attn_prompt(proj, sinks, n, t, l, nsub=1):
    blk = WINDOW
    rows = nsub * blk
    steps = t // rows
    kcol, vcol = COL_K // D_KV, COL_V // D_KV
    prev = lambda s, i: s * (t // blk) + jnp.maximum(nsub * i - 1, 0)
    blocks = ([((rows, D_ATTN), BF16)] * 2 + [((rows, D_KV), BF16)] * 2 + [((blk, D_KV), BF16)] * 2)
    return pl.pallas_call(
        functools.partial(_attn_prompt_body, l=l, nsub=nsub),
        grid=(n, steps),
        in_specs=[pl.BlockSpec(memory_space=pltpu.SMEM),
                  pl.BlockSpec((rows, D_ATTN), lambda s, i: (s * steps + i, COL_Q // D_ATTN)),
                  pl.BlockSpec((rows, D_KV), lambda s, i: (s * steps + i, kcol)),
                  pl.BlockSpec((rows, D_KV), lambda s, i: (s * steps + i, vcol)),
                  pl.BlockSpec((blk, D_KV), lambda s, i: (prev(s, i), kcol)),
                  pl.BlockSpec((blk, D_KV), lambda s, i: (prev(s, i), vcol))],
        out_specs=pl.BlockSpec((rows, D_ATTN), lambda s, i: (s * steps + i, 0)),
        out_shape=jax.ShapeDtypeStruct((n * t, D_ATTN), BF16),
        compiler_params=_params(("parallel", "parallel"), blocks,
                                temps=[((GROUP * blk, 2 * blk), F32)] * 16),
        name="attn_prompt",
    )(sinks, proj, proj, proj, proj, proj)


def _attn_sample_body(sink_ref, q_ref, kn_ref, vn_ref, ck_ref, cv_ref, nk_ref, nv_ref, o_ref, *, nb, t, l):
    eye = (lax.broadcasted_iota(jnp.int32, (D_KV, D_KV), 0)
           == lax.broadcasted_iota(jnp.int32, (D_KV, D_KV), 1)).astype(BF16)
    lane = lax.broadcasted_iota(jnp.int32, (D_KV, WINDOW), 1)
    for new_ref, cache_ref, out_ref in ((kn_ref, ck_ref, nk_ref), (vn_ref, cv_ref, nv_ref)):
        tr = lax.dot_general(eye, new_ref[...], (((1,), (1,)), ((), ())), preferred_element_type=F32)
        for s in range(nb):
            kept = pltpu.roll(cache_ref[s], WINDOW - t, axis=1)
            fresh = pltpu.roll(tr, (WINDOW - t - s * t) % WINDOW, axis=1)
            out_ref[s] = jnp.where(lane >= WINDOW - t, fresh, kept)
    q3 = (q_ref[...].astype(F32) * SCALE).reshape(nb, t, D_ATTN)
    rows = GROUP * t
    tq = lax.broadcasted_iota(jnp.int32, (rows, WINDOW), 0) % t
    w = lax.broadcasted_iota(jnp.int32, (rows, WINDOW), 1)
    dist_old = WINDOW + tq - w
    valid_old = (w < t) & (w >= tq)
    dist_new = WINDOW - t + tq - w
    valid_new = dist_new >= 0
    gi = lax.broadcasted_iota(jnp.int32, (rows, 1), 0) // t
    for kv in range(N_KV_HEADS):
        heads = [kv * GROUP + g for g in range(GROUP)]
        slope = jnp.zeros((rows, 1), F32)
        sink = jnp.zeros((rows, 1), F32)
        for g, h in enumerate(heads):
            slope = jnp.where(gi == g, SLOPES[h], slope)
            sink = jnp.where(gi == g, sink_ref[l, h], sink)
        hs = slice(kv * HEAD_DIM, (kv + 1) * HEAD_DIM)
        qs = jnp.concatenate([q3[:, :, h * HEAD_DIM:(h + 1) * HEAD_DIM] for h in heads], axis=1).astype(BF16)
        s_old = jnp.einsum("bqd,bdk->bqk", qs, ck_ref[:, hs, :].astype(BF16), preferred_element_type=F32)
        s_new = jnp.einsum("bqd,bdk->bqk", qs, nk_ref[:, hs, :].astype(BF16), preferred_element_type=F32)
        s_old = jnp.where(valid_old, s_old - slope * dist_old.astype(F32), NEG)
        s_new = jnp.where(valid_new, s_new - slope * dist_new.astype(F32), NEG)
        m = jnp.maximum(jnp.maximum(jnp.max(s_old, axis=-1, keepdims=True),
                                    jnp.max(s_new, axis=-1, keepdims=True)), sink)
        p_old = jnp.exp(s_old - m)
        p_new = jnp.exp(s_new - m)
        denom = (jnp.sum(p_old, axis=-1, keepdims=True) + jnp.sum(p_new, axis=-1, keepdims=True)
                 + jnp.exp(sink - m))
        o = (jnp.einsum("bqk,bdk->bqd", p_old.astype(BF16), cv_ref[:, hs, :].astype(BF16),
                        preferred_element_type=F32)
             + jnp.einsum("bqk,bdk->bqd", p_new.astype(BF16), nv_ref[:, hs, :].astype(BF16),
                          preferred_element_type=F32)) / denom
        for g, h in enumerate(heads):
            o_ref[:, h * HEAD_DIM:(h + 1) * HEAD_DIM] = (
                o[:, g * t:(g + 1) * t, :].reshape(nb * t, HEAD_DIM).astype(BF16))


def _attn_sample(proj, cache_k, cache_v, prev, sinks, n, t, l):
    depth = cache_k.shape[0]
    nb = WINDOW // t
    rows = nb * t
    kcol, vcol = COL_K // D_KV, COL_V // D_KV
    cache = pl.BlockSpec((None, nb, D_KV, WINDOW), lambda s: (l, s, 0, 0))
    stacked = jax.ShapeDtypeStruct((depth, n, D_KV, WINDOW), F32)
    blocks = [((rows, D_ATTN), BF16)] * 2 + [((rows, D_KV), BF16)] * 2 + [((nb, D_KV, WINDOW), F32)] * 4
    return _stacked_call(
        functools.partial(_attn_sample_body, nb=nb, t=t, l=l), l, prev, 2,
        grid=(n // nb,),
        in_specs=[pl.BlockSpec(memory_space=pltpu.SMEM),
                  pl.BlockSpec((rows, D_ATTN), lambda s: (s, COL_Q // D_ATTN)),
                  pl.BlockSpec((rows, D_KV), lambda s: (s, kcol)),
                  pl.BlockSpec((rows, D_KV), lambda s: (s, vcol)),
                  cache, cache],
        out_specs=[cache, cache, pl.BlockSpec((rows, D_ATTN), lambda s: (s, 0))],
        out_shape=[stacked, stacked, jax.ShapeDtypeStruct((n * t, D_ATTN), BF16)],
        compiler_params=_params(("parallel",), blocks,
                                temps=[((nb, D_KV, WINDOW), F32)] * 4 + [((nb, GROUP * t, WINDOW), F32)] * 8,
                                keep_operands_in_hbm=True),
        name="attn_sample",
    )(sinks, proj, proj, proj, cache_k, cache_v)


def _mix_body(c_ref, a_ref, *refs):
    n_gate = D_MODEL // GATE_BLOCK
    gc_refs, ga_refs = refs[:n_gate], refs[n_gate:2 * n_gate]
    x_ref, wco_ref, wao_ref, wo_ref, n2_ref, x1_ref, h2_ref = refs[2 * n_gate:]
    gate_c = jnp.concatenate([g[...] for g in gc_refs], axis=1).astype(F32)
    gate_a = jnp.concatenate([g[...] for g in ga_refs], axis=1).astype(F32)
    branch_c = _dot(c_ref[...], wco_ref[...])
    branch_a = _dot(a_ref[...], wao_ref[...])
    merged = jax.nn.sigmoid(gate_c) * branch_c + jax.nn.sigmoid(gate_a) * branch_a
    x1 = x_ref[...] + _dot(merged.astype(BF16), wo_ref[...])
    x1_ref[...] = x1
    h2_ref[...] = _rms(x1, n2_ref[...]).astype(BF16)


def _mix(c_act, a_act, proj, x, wco, wao, wo, n2, l, tm=256):
    m = x.shape[0]
    row = lambda width, col: pl.BlockSpec((tm, width), lambda i: (i, col))
    const = lambda k: pl.BlockSpec((None, k, D_MODEL), lambda i: (l, 0, 0), pipeline_mode=pl.Buffered(1))
    blocks = ([((tm, D_CONV), BF16), ((tm, D_ATTN), BF16)] + [((tm, D_MODEL), F32)] * 2 + [((tm, D_MODEL), BF16)] * 3)
    weights = [((D_CONV, D_MODEL), BF16), ((D_ATTN, D_MODEL), BF16), ((D_MODEL, D_MODEL), BF16)]
    return pl.pallas_call(
        _mix_body,
        grid=(m // tm,),
        in_specs=[row(D_CONV, 0), row(D_ATTN, 0)]
                 + [row(GATE_BLOCK, COL_GATE_C // GATE_BLOCK + b) for b in range(D_MODEL // GATE_BLOCK)]
                 + [row(GATE_BLOCK, COL_GATE_A // GATE_BLOCK + b) for b in range(D_MODEL // GATE_BLOCK)]
                 + [row(D_MODEL, 0),
                  const(D_CONV), const(D_ATTN), const(D_MODEL),
                  _layer_vec(l, D_MODEL)],
        out_specs=[row(D_MODEL, 0), row(D_MODEL, 0)],
        out_shape=[jax.ShapeDtypeStruct((m, D_MODEL), F32), jax.ShapeDtypeStruct((m, D_MODEL), BF16)],
        compiler_params=_params(("parallel",), blocks, resident=weights, temps=[((tm, D_MODEL), F32)] * 4),
        name="mix",
    )(c_act, a_act, *([proj] * (2 * (D_MODEL // GATE_BLOCK))), x, wco, wao, wo, _as_rows(n2))


def _ffn_conv(e_ref, w_ref, b_ref, tm):
    first = FFN_HALO - (FFN_K - 1)
    out = b_ref[...]
    for j in range(FFN_K):
        out = out + w_ref[j:j + 1, :] * e_ref[first + j:first + j + tm, :]
    return out


def _up_prompt_body(h_ref, wg_ref, wv_ref, cwg_ref, cwv_ref, cbg_ref, cbv_ref, act_ref, sg_ref, sv_ref,
                    eg_ref, ev_ref, *, tm, tiles_per_seq):
    pos = pl.program_id(1) % tiles_per_seq

    @pl.when(pos == 0)
    def _():
        eg_ref[0:FFN_HALO, :] = jnp.zeros((FFN_HALO, eg_ref.shape[1]), F32)
        ev_ref[0:FFN_HALO, :] = jnp.zeros((FFN_HALO, ev_ref.shape[1]), F32)

    h = h_ref[...]
    eg_ref[FFN_HALO:FFN_HALO + tm, :] = _dot(h, wg_ref[...])
    ev_ref[FFN_HALO:FFN_HALO + tm, :] = _dot(h, wv_ref[...])
    gate = _ffn_conv(eg_ref, cwg_ref, cbg_ref, tm)
    val = _ffn_conv(ev_ref, cwv_ref, cbv_ref, tm)
    act_ref[...] = (gate * jax.nn.sigmoid(gate) * val).astype(BF16)

    @pl.when(pos == tiles_per_seq - 1)
    def _():
        hist = FFN_K - 1
        sg_ref[0] = eg_ref[FFN_HALO + tm - hist:FFN_HALO + tm, :]
        sv_ref[0] = ev_ref[FFN_HALO + tm - hist:FFN_HALO + tm, :]

    eg_ref[0:FFN_HALO, :] = eg_ref[tm:tm + FFN_HALO, :]
    ev_ref[0:FFN_HALO, :] = ev_ref[tm:tm + FFN_HALO, :]


def _up_prompt(h2, n, t, wg16, wv16, cw, cb, l, tm=1024, tile=FFN_PROMPT_TILE):
    m = n * t
    tps = t // tm
    nt = D_FF // tile
    hist = FFN_K - 1
    col = lambda rows, off: pl.BlockSpec((None, rows, tile), lambda j, i: (l, 0, j + off))
    weight = pl.BlockSpec((D_MODEL, tile), lambda j, i: (0, j))
    blocks = ([((tm, D_MODEL), BF16)] + [((D_MODEL, tile), BF16)] * 2 + [((tm, tile), BF16)])
    scratch = [((FFN_HALO + tm, tile), F32)] * 2
    state = pl.BlockSpec((1, hist, tile), lambda j, i: (i // tps, 0, j))
    return pl.pallas_call(
        functools.partial(_up_prompt_body, tm=tm, tiles_per_seq=tps),
        grid=(nt, m // tm),
        in_specs=[pl.BlockSpec((tm, D_MODEL), lambda j, i: (i, 0)),
                  weight, weight, col(FFN_K, 0), col(FFN_K, nt), col(1, 0), col(1, nt)],
        out_specs=[pl.BlockSpec((tm, tile), lambda j, i: (i, j)), state, state],
        out_shape=[jax.ShapeDtypeStruct((m, D_FF), BF16),
                   jax.ShapeDtypeStruct((n, hist, D_FF), F32), jax.ShapeDtypeStruct((n, hist, D_FF), F32)],
        scratch_shapes=[pltpu.VMEM(s, d) for s, d in scratch],
        compiler_params=_params(("parallel", "arbitrary"), blocks, resident=scratch,
                                temps=[((tm, tile), F32)] * 4),
        name="up_prompt",
    )(h2, wg16, wv16, cw, cw, _as_rows(cb), _as_rows(cb))


def _ffn_conv_sample(u3, st_ref, w_ref, b_ref, tpos):
    st0 = st_ref[:, 0:1, :]
    st1 = st_ref[:, 1:2, :]
    prev1 = jnp.where(tpos == 0, st1, pltpu.roll(u3, 1, axis=1))
    prev2 = jnp.where(tpos == 0, st0, jnp.where(tpos == 1, st1, pltpu.roll(u3, 2, axis=1)))
    return b_ref[...] + w_ref[0:1, :] * prev2 + w_ref[1:2, :] * prev1 + w_ref[2:3, :] * u3


def _up_sample_body(h_ref, wg_ref, wv_ref, cwg_ref, cwv_ref, cbg_ref, cbv_ref, stg_ref, stv_ref,
                    s_ref, act_ref, wg16_ref, wv16_ref, *, n, t):
    hist = FFN_K - 1
    h = h_ref[...]
    wg16_ref[...] = wg_ref[...].astype(BF16)
    wv16_ref[...] = wv_ref[...].astype(BF16)
    ug = _dot(h, wg16_ref[...]).reshape(n, t, FFN_TILE)
    uv = _dot(h, wv16_ref[...]).reshape(n, t, FFN_TILE)
    tpos = lax.broadcasted_iota(jnp.int32, (1, t, 1), 1)
    gate = _ffn_conv_sample(ug, stg_ref, cwg_ref, cbg_ref, tpos)
    val = _ffn_conv_sample(uv, stv_ref, cwv_ref, cbv_ref, tpos)
    act_ref[...] = (gate * jax.nn.sigmoid(gate) * val).reshape(n * t, FFN_TILE).astype(BF16)
    s_ref[:, 0] = ug[:, t - hist:t, :]
    s_ref[:, 1] = uv[:, t - hist:t, :]


def _up_sample(h2, state, prev, n, t, w_up, cw, cb, l):
    assert t == V7X_SUBLANES and FFN_K == 3
    depth = state.shape[0]
    m = n * t
    nt = D_FF // FFN_TILE
    hist = FFN_K - 1
    col = lambda rows, off: pl.BlockSpec((None, rows, FFN_TILE), lambda j: (l, 0, j + off))
    st = lambda off: pl.BlockSpec((None, n, hist, FFN_TILE), lambda j: (l, 0, 0, j + off))
    new_state = pl.BlockSpec((None, n, 2, hist, FFN_TILE), lambda j: (l, 0, 0, 0, j))
    w16 = pl.BlockSpec((D_MODEL, FFN_TILE), lambda j: (0, j))
    w16_shape = jax.ShapeDtypeStruct((D_MODEL, D_FF), BF16)
    blocks = ([((D_MODEL, FFN_TILE), F32)] * 2 + [((n, V7X_SUBLANES, FFN_TILE), F32)] * 4
              + [((m, FFN_TILE), BF16)] + [((D_MODEL, FFN_TILE), BF16)] * 2)
    return _stacked_call(
        functools.partial(_up_sample_body, n=n, t=t), l, prev, 1,
        grid=(nt,),
        in_specs=[pl.BlockSpec((m, D_MODEL), lambda j: (0, 0)),
                  col(D_MODEL, 0), col(D_MODEL, nt), col(FFN_K, 0), col(FFN_K, nt), col(1, 0), col(1, nt),
                  st(0), st(nt)],
        out_specs=[new_state, pl.BlockSpec((m, FFN_TILE), lambda j: (0, j)), w16, w16],
        out_shape=[jax.ShapeDtypeStruct((depth, n, 2, hist, D_FF), F32), jax.ShapeDtypeStruct((m, D_FF), BF16),
                   w16_shape, w16_shape],
        compiler_params=_params(("parallel",), blocks,
                                resident=[((m, D_MODEL), BF16)] * 2,
                                temps=[((m, FFN_TILE), F32)] * 8),
        name="up_sample",
    )(h2, w_up, w_up, cw, cw, _as_rows(cb), _as_rows(cb), state, state)


def _down_body(act_ref, w_ref, x_ref, o_ref, *refs, cast):
    if cast:
        w16_ref, = refs

        @pl.when(pl.program_id(1) == 0)
        def _():
            w16_ref[...] = w_ref[...].astype(BF16)

        w_ref = w16_ref
    o_ref[...] = x_ref[...] + _dot(act_ref[...], w_ref[...])


def _down(act, w, x1, l, tm):
    m = x1.shape[0]
    cast = w.dtype == F32
    w_spec = (pl.BlockSpec((None, D_FF, DOWN_TILE), lambda j, i: (l, 0, j)) if cast
              else pl.BlockSpec((D_FF, DOWN_TILE), lambda j, i: (0, j)))
    out_specs = [pl.BlockSpec((tm, DOWN_TILE), lambda j, i: (i, j))]
    out_shape = [jax.ShapeDtypeStruct((m, D_MODEL), F32)]
    blocks = [((tm, D_FF), BF16), ((D_FF, DOWN_TILE), w.dtype), ((tm, DOWN_TILE), F32), ((tm, DOWN_TILE), F32)]
    if cast:
        out_specs.append(pl.BlockSpec((D_FF, DOWN_TILE), lambda j, i: (0, j)))
        out_shape.append(jax.ShapeDtypeStruct((D_FF, D_MODEL), BF16))
        blocks.append(((D_FF, DOWN_TILE), BF16))
    outs = pl.pallas_call(
        functools.partial(_down_body, cast=cast),
        grid=(D_MODEL // DOWN_TILE, m // tm),
        in_specs=[pl.BlockSpec((tm, D_FF), lambda j, i: (i, 0)), w_spec,
                  pl.BlockSpec((tm, DOWN_TILE), lambda j, i: (i, j))],
        out_specs=out_specs,
        out_shape=out_shape,
        compiler_params=_params(("parallel", "arbitrary"), blocks, temps=[((tm, DOWN_TILE), F32)]),
        name="down",
    )(act, w, x1)
    return outs if cast else outs[0]


def _final_norm_body(x_ref, g_ref, o_ref):
    o_ref[...] = _rms(x_ref[...], g_ref[...])


def _final_norm(x, g, tm=512):
    m = x.shape[0]
    blocks = [((tm, D_MODEL), F32)] * 2
    return pl.pallas_call(
        _final_norm_body,
        grid=(m // tm,),
        in_specs=[pl.BlockSpec((tm, D_MODEL), lambda i: (i, 0)), pl.BlockSpec((1, D_MODEL), lambda i: (0, 0))],
        out_specs=pl.BlockSpec((tm, D_MODEL), lambda i: (i, 0)),
        out_shape=jax.ShapeDtypeStruct((m, D_MODEL), F32),
        compiler_params=_params(("parallel",), blocks, temps=[((tm, D_MODEL), F32)]),
        name="final_norm",
    )(x, g.reshape(1, D_MODEL))


def kernel(x_prompt, x_sample, cache_k, cache_v, state_conv, state_ffn_conv, norm1_g, w_in, conv_w, conv_b,
           conv_ln_g, conv_ln_b, w_conv_out, attn_sinks, w_attn_out, w_out, norm2_g, w_up, ffn_conv_w,
           ffn_conv_b, w_down, final_norm_g):
    depth = w_in.shape[0]
    n_p, t_p, _ = x_prompt.shape
    n_s, t_s, _ = x_sample.shape
    xp = x_prompt.reshape(n_p * t_p, D_MODEL)
    xs = x_sample.reshape(n_s * t_s, D_MODEL)
    w_co_b, w_ao_b, w_o_b = (w.astype(BF16) for w in (w_conv_out, w_attn_out, w_out))
    keys_on_lanes = lambda c: jnp.transpose(c, (0, 1, 3, 4, 2)).reshape(depth, n_s, D_KV, WINDOW)
    keys_on_rows = lambda c: jnp.transpose(c.reshape(depth, n_s, N_KV_HEADS, HEAD_DIM, WINDOW), (0, 1, 4, 2, 3))
    cache_k, cache_v = keys_on_lanes(cache_k), keys_on_lanes(cache_v)
    rows_first = lambda s: jnp.transpose(s, (0, 2, 1, 3))
    state_conv = rows_first(state_conv)
    outs = {k: [] for k in ("kp", "vp", "cp", "fp")}
    c_states, kv_states, f_states = (), (), ()
    for l in range(depth):
        conv_args = (conv_w, conv_b, conv_ln_g, conv_ln_b, l)
        proj_s, w_in16 = _inproj(xs, norm1_g, w_in, l, tm=n_s * t_s)
        proj = _inproj(xp, norm1_g, w_in16, l, tm=1024)
        c_act, c_state = _conv_prompt(proj, n_p, t_p, *conv_args)
        a_act = _attn_prompt(proj, attn_sinks, n_p, t_p, l)
        x1_p, h2_p = _mix(c_act, a_act, proj, xp, w_co_b, w_ao_b, w_o_b, norm2_g, l)
        kv = proj.reshape(n_p, t_p, IN_COLS)[:, t_p - WINDOW:, :]
        outs["kp"].append(kv[:, :, COL_K:COL_K + D_KV].astype(F32).reshape(n_p, WINDOW, N_KV_HEADS, HEAD_DIM))
        outs["vp"].append(kv[:, :, COL_V:COL_V + D_KV].astype(F32).reshape(n_p, WINDOW, N_KV_HEADS, HEAD_DIM))
        outs["cp"].append(c_state)
        proj = proj_s
        c_stack, c_act = _conv_sample(proj, state_conv, c_states, n_s, t_s, *conv_args)
        k_stack, v_stack, a_act = _attn_sample(proj, cache_k, cache_v, kv_states, attn_sinks, n_s, t_s, l)
        c_states, kv_states = (c_stack,), (k_stack, v_stack)
        x1, h2 = _mix(c_act, a_act, proj, xs, w_co_b, w_ao_b, w_o_b, norm2_g, l)
        f_stack, act, wg16, wv16 = _up_sample(h2, state_ffn_conv, f_states, n_s, t_s, w_up, ffn_conv_w,
                                              ffn_conv_b, l)
        f_states = (f_stack,)
        xs, w_down16 = _down(act, w_down, x1, l, tm=512)
        act, f_g, f_v = _up_prompt(h2_p, n_p, t_p, wg16, wv16, ffn_conv_w, ffn_conv_b, l)
        xp = _down(act, w_down16, x1_p, l, tm=1024)
        outs["fp"].append(jnp.concatenate([f_g, f_v], axis=-1))
    y_prompt = _final_norm(xp, final_norm_g).reshape(n_p, t_p, D_MODEL)
    y_sample = _final_norm(xs, final_norm_g).reshape(n_s, t_s, D_MODEL)
    return (y_prompt, y_sample, jnp.stack(outs["kp"]), jnp.stack(outs["vp"]), jnp.stack(outs["cp"]),
            jnp.stack(outs["fp"]), keys_on_rows(k_stack), keys_on_rows(v_stack), rows_first(c_stack),
            jnp.transpose(f_stack, (0, 1, 3, 2, 4)).reshape(depth, n_s, FFN_K - 1, 2 * D_FF))
```

```python
import functools

import jax
import jax.numpy as jnp
from jax import lax
from jax.experimental import pallas as pl
from jax.experimental.pallas import tpu as pltpu

F32 = jnp.float32
BF16 = jnp.bfloat16

D_MODEL = 2048
HEAD_DIM = 64
N_HEADS = 16
N_KV_HEADS = 4
GROUP = N_HEADS // N_KV_HEADS
D_ATTN = N_HEADS * HEAD_DIM
D_KV = N_KV_HEADS * HEAD_DIM
WINDOW = 128
D_CONV = D_MODEL // 2
CONV_K = 31
D_FF = 3 * D_MODEL
FFN_K = 3
EPS = 1e-6
IN_COLS = 2 * D_CONV + D_ATTN + 2 * D_KV + 2 * D_MODEL
NEG = -1e30
SCALE = HEAD_DIM ** -0.5
SLOPES = tuple(2.0 ** (-8.0 * (h + 1) / N_HEADS) for h in range(N_HEADS))

V7X_SUBLANES = 8
V7X_SCOPED_VMEM_CAP_BYTES = 60000 * 1024

IN_TILE = 1280
COL_GLU_A = 0
COL_GLU_G = D_CONV
COL_Q = 2 * D_CONV
COL_K = COL_Q + D_ATTN
COL_V = COL_K + D_KV
COL_GATE_C = COL_V + D_KV
COL_GATE_A = COL_GATE_C + D_MODEL
GATE_BLOCK = 512

CONV_HALO = 32
CONV_ROW_CHUNK = 64
CONV_LANE_CHUNK = 256
FFN_HALO = V7X_SUBLANES
FFN_TILE = 512
FFN_PROMPT_TILE = 1024
DOWN_TILE = 512


def _nbytes(shape, dtype):
    n = 1
    for s in shape:
        n *= s
    return n * jnp.dtype(dtype).itemsize


def _params(semantics, pipelined, resident=(), temps=(), keep_operands_in_hbm=True):
    est = 2 * sum(_nbytes(s, d) for s, d in pipelined)
    est += sum(_nbytes(s, d) for s, d in resident)
    est += sum(_nbytes(s, d) for s, d in temps)
    limit = min(V7X_SCOPED_VMEM_CAP_BYTES, est + est // 4)
    if keep_operands_in_hbm:
        limit = V7X_SCOPED_VMEM_CAP_BYTES
    return pltpu.CompilerParams(dimension_semantics=semantics, vmem_limit_bytes=limit)


def _layer_vec(l, width, rows=1):
    return pl.BlockSpec((None, rows, width), lambda *_: (l, 0, 0))


def _as_rows(p):
    return p.reshape(p.shape[0], 1, p.shape[1])


def _rms(x, g):
    return x * lax.rsqrt(jnp.mean(x * x, axis=-1, keepdims=True) + EPS) * g


def _dot(a, b):
    return jnp.dot(a, b, preferred_element_type=F32)


def _skip_first_ref(body):
    def wrapped(_, *refs):
        body(*refs)
    return wrapped


def _stacked_call(body, l, prev, n_alias_out, **kw):
    if l == 0:
        return pl.pallas_call(body, **kw)
    for _ in prev:
        body = _skip_first_ref(body)
    kw["in_specs"] = [pl.BlockSpec(memory_space=pl.ANY)] * len(prev) + list(kw["in_specs"])
    call = pl.pallas_call(body, input_output_aliases={i: i for i in range(n_alias_out)}, **kw)
    return lambda *args: call(*prev, *args)


def _inproj_body(x_ref, g_ref, w_ref, o_ref, *refs, cast):
    h_ref = refs[-1]

    @pl.when(pl.program_id(1) == 0)
    def _():
        h_ref[...] = _rms(x_ref[...], g_ref[...]).astype(BF16)

    if cast:
        w16_ref = refs[0]
        w16_ref[...] = w_ref[...].astype(BF16)
        w_ref = w16_ref
    o_ref[...] = _dot(h_ref[...], w_ref[...]).astype(BF16)


def _inproj(x, g, w, l, tm):
    m = x.shape[0]
    cast = w.dtype == F32
    assert not cast or m == tm
    w_spec = (pl.BlockSpec((None, D_MODEL, IN_TILE), lambda i, j: (l, 0, j)) if cast
              else pl.BlockSpec((D_MODEL, IN_TILE), lambda i, j: (0, j)))
    out_specs = [pl.BlockSpec((tm, IN_TILE), lambda i, j: (i, j))]
    out_shape = [jax.ShapeDtypeStruct((m, IN_COLS), BF16)]
    blocks = [((tm, D_MODEL), F32), ((D_MODEL, IN_TILE), w.dtype), ((tm, IN_TILE), BF16)]
    if cast:
        out_specs.append(pl.BlockSpec((D_MODEL, IN_TILE), lambda i, j: (0, j)))
        out_shape.append(jax.ShapeDtypeStruct((D_MODEL, IN_COLS), BF16))
        blocks.append(((D_MODEL, IN_TILE), BF16))
    outs = pl.pallas_call(
        functools.partial(_inproj_body, cast=cast),
        grid=(m // tm, IN_COLS // IN_TILE),
        in_specs=[pl.BlockSpec((tm, D_MODEL), lambda i, j: (i, 0)), _layer_vec(l, D_MODEL), w_spec],
        out_specs=out_specs,
        out_shape=out_shape,
        scratch_shapes=[pltpu.VMEM((tm, D_MODEL), BF16)],
        compiler_params=_params(("parallel", "arbitrary"), blocks, resident=[((tm, D_MODEL), BF16)],
                                temps=[((tm, D_MODEL), F32)]),
        name="inproj",
    )(x, _as_rows(g), w)
    return outs if cast else outs[0]


def _ln_silu(c, g, b):
    mu = jnp.mean(c, axis=-1, keepdims=True)
    xc = c - mu
    y = xc * lax.rsqrt(jnp.mean(xc * xc, axis=-1, keepdims=True) + EPS) * g + b
    return y * jax.nn.sigmoid(y)


def _conv_prompt_body(a_ref, g_ref, w_ref, b_ref, lng_ref, lnb_ref, act_ref, st_ref, xx_ref, cv_ref, sh_ref,
                      *, tt):
    i = pl.program_id(1)

    @pl.when(i == 0)
    def _():
        xx_ref[0:CONV_HALO, :] = jnp.zeros((CONV_HALO, D_CONV), F32)

    xx_ref[CONV_HALO:CONV_HALO + tt, :] = a_ref[...].astype(F32) * jax.nn.sigmoid(g_ref[...].astype(F32))

    first = CONV_HALO - (CONV_K - 1)
    sh_rows = sh_ref.shape[1]
    for lc in range(D_CONV // CONV_LANE_CHUNK):
        ls = pl.ds(lc * CONV_LANE_CHUNK, CONV_LANE_CHUNK)
        for r in range(1, V7X_SUBLANES):
            sh_ref[r - 1] = xx_ref[r:r + sh_rows, ls]
        for r0 in range(0, tt, CONV_ROW_CHUNK):
            acc = jnp.broadcast_to(b_ref[:, ls], (CONV_ROW_CHUNK, CONV_LANE_CHUNK))
            for j in range(CONV_K):
                q, r = divmod(first + j, V7X_SUBLANES)
                a0 = r0 + V7X_SUBLANES * q
                if r == 0:
                    window = xx_ref[a0:a0 + CONV_ROW_CHUNK, ls]
                else:
                    window = sh_ref[r - 1, a0:a0 + CONV_ROW_CHUNK, :]
                acc = acc + w_ref[j:j + 1, ls] * window
            cv_ref[r0:r0 + CONV_ROW_CHUNK, ls] = acc
    act_ref[...] = _ln_silu(cv_ref[...], lng_ref[...], lnb_ref[...]).astype(BF16)

    @pl.when(i == pl.num_programs(1) - 1)
    def _():
        st_ref[0] = xx_ref[tt + first:tt + CONV_HALO, :]

    xx_ref[0:CONV_HALO, :] = xx_ref[tt:tt + CONV_HALO, :]


def _conv_prompt(proj, n, t, w, b, lng, lnb, l, tt=512):
    tps = t // tt
    vec = _layer_vec(l, D_CONV)
    blocks = [((tt, D_CONV), BF16)] * 3
    shifted = (V7X_SUBLANES - 1, tt + CONV_HALO - V7X_SUBLANES, CONV_LANE_CHUNK)
    return pl.pallas_call(
        functools.partial(_conv_prompt_body, tt=tt),
        grid=(n, tps),
        in_specs=[pl.BlockSpec((tt, D_CONV), lambda s, i: (s * tps + i, COL_GLU_A // D_CONV)),
                  pl.BlockSpec((tt, D_CONV), lambda s, i: (s * tps + i, COL_GLU_G // D_CONV)),
                  _layer_vec(l, D_CONV, CONV_K), vec, vec, vec],
        out_specs=[pl.BlockSpec((tt, D_CONV), lambda s, i: (s * tps + i, 0)),
                   pl.BlockSpec((1, CONV_K - 1, D_CONV), lambda s, i: (s, 0, 0))],
        out_shape=[jax.ShapeDtypeStruct((n * t, D_CONV), BF16),
                   jax.ShapeDtypeStruct((n, CONV_K - 1, D_CONV), F32)],
        scratch_shapes=[pltpu.VMEM((CONV_HALO + tt, D_CONV), F32), pltpu.VMEM((tt, D_CONV), F32),
                        pltpu.VMEM(shifted, F32)],
        compiler_params=_params(("parallel", "arbitrary"), blocks,
                                resident=[((CONV_HALO + tt, D_CONV), F32), ((tt, D_CONV), F32), (shifted, F32)],
                                temps=[((tt, D_CONV), F32)] * 4),
        name="conv_prompt",
    )(proj, proj, w, _as_rows(b), _as_rows(lng), _as_rows(lnb))


def _conv_sample_body(a_ref, g_ref, s_ref, w_ref, b_ref, lng_ref, lnb_ref, st_ref, act_ref, new_ref, y_ref,
                      *, nb, t):
    hist = CONV_K - 1
    u = (a_ref[...].astype(F32) * jax.nn.sigmoid(g_ref[...].astype(F32))).reshape(nb, t, D_CONV)
    for j in range(t):
        new_ref[j] = u[:, j, :]
    for r in range(hist):
        st_ref[r] = s_ref[r + t] if r + t < hist else new_ref[r + t - hist]
    for j in range(t):
        for lc in range(D_CONV // CONV_LANE_CHUNK):
            ls = pl.ds(lc * CONV_LANE_CHUNK, CONV_LANE_CHUNK)
            acc = jnp.broadcast_to(b_ref[:, ls], (nb, CONV_LANE_CHUNK))
            for k in range(CONV_K):
                r = j + k
                plane = s_ref[r, :, ls] if r < hist else new_ref[r - hist, :, ls]
                acc = acc + w_ref[k:k + 1, ls] * plane
            y_ref[j, :, ls] = acc
    y = _ln_silu(y_ref[...], lng_ref[...], lnb_ref[...])
    act_ref[...] = jnp.stack([y[j] for j in range(t)], axis=1).reshape(nb * t, D_CONV).astype(BF16)


def _conv_sample(proj, state, prev, n, t, w, b, lng, lnb, l, nb=32):
    depth = state.shape[0]
    hist = CONV_K - 1
    rows = nb * t
    vec = _layer_vec(l, D_CONV)
    st = pl.BlockSpec((None, hist, nb, D_CONV), lambda s: (l, 0, s, 0))
    blocks = [((rows, D_CONV), BF16)] * 3 + [((hist, nb, D_CONV), F32)] * 2
    scratch = [((t, nb, D_CONV), F32)] * 2
    return _stacked_call(
        functools.partial(_conv_sample_body, nb=nb, t=t), l, prev, 1,
        grid=(n // nb,),
        in_specs=[pl.BlockSpec((rows, D_CONV), lambda s: (s, COL_GLU_A // D_CONV)),
                  pl.BlockSpec((rows, D_CONV), lambda s: (s, COL_GLU_G // D_CONV)),
                  st, _layer_vec(l, D_CONV, CONV_K), vec, vec, vec],
        out_specs=[st, pl.BlockSpec((rows, D_CONV), lambda s: (s, 0))],
        out_shape=[jax.ShapeDtypeStruct((depth, hist, n, D_CONV), F32),
                   jax.ShapeDtypeStruct((n * t, D_CONV), BF16)],
        scratch_shapes=[pltpu.VMEM(s, d) for s, d in scratch],
        compiler_params=_params(("parallel",), blocks, resident=scratch,
                                temps=[((rows, D_CONV), F32)] * 4, keep_operands_in_hbm=True),
        name="conv_sample",
    )(proj, proj, state, w, _as_rows(b), _as_rows(lng), _as_rows(lnb))


def _softmax_terms(s, valid, distf, slope, sink):
    s = jnp.where(valid, s - slope * distf, NEG)
    m = jnp.maximum(jnp.max(s, axis=-1, keepdims=True), sink)
    p = jnp.exp(s - m)
    denom = jnp.sum(p, axis=-1, keepdims=True) + jnp.exp(sink - m)
    return p, denom


def _attn_prompt_body(sink_ref, q_ref, kc_ref, vc_ref, kp_ref, vp_ref, o_ref, *, l, nsub):
    i = pl.program_id(1)
    blk = WINDOW
    q_all = q_ref[...] * SCALE
    k_all = jnp.concatenate([kp_ref[...], kc_ref[...]], axis=0)
    v_all = jnp.concatenate([vp_ref[...], vc_ref[...]], axis=0)
    r = lax.broadcasted_iota(jnp.int32, (blk, 2 * blk), 0)
    c = lax.broadcasted_iota(jnp.int32, (blk, 2 * blk), 1)
    dist = blk + r - c
    in_window = (dist >= 0) & (dist <= WINDOW)
    distf = dist.astype(F32)
    for sb in range(nsub):
        q = q_all[sb * blk:(sb + 1) * blk]
        k = k_all[sb * blk:(sb + 2) * blk]
        v = v_all[sb * blk:(sb + 2) * blk]
        valid = in_window & ((c >= blk) | (i > 0)) if sb == 0 else in_window
        scores = []
        for kv in range(N_KV_HEADS):
            kh = k[:, kv * HEAD_DIM:(kv + 1) * HEAD_DIM]
            heads = [kv * GROUP + g for g in range(GROUP)]
            qs = jnp.concatenate([q[:, h * HEAD_DIM:(h + 1) * HEAD_DIM] for h in heads], axis=0)
            scores.append(lax.dot_general(qs, kh, (((1,), (1,)), ((), ())),
                                          preferred_element_type=F32))
        probs, denoms = [], []
        for kv in range(N_KV_HEADS):
            ps = []
            for g in range(GROUP):
                h = kv * GROUP + g
                p, d = _softmax_terms(scores[kv][g * blk:(g + 1) * blk], valid, distf, SLOPES[h], sink_ref[l, h])
                ps.append(p.astype(BF16))
                denoms.append(d)
            probs.append(jnp.concatenate(ps, axis=0))
        outs = [_dot(probs[kv], v[:, kv * HEAD_DIM:(kv + 1) * HEAD_DIM]) for kv in range(N_KV_HEADS)]
        for h in range(N_HEADS):
            kv, g = divmod(h, GROUP)
            o_ref[sb * blk:(sb + 1) * blk, h * HEAD_DIM:(h + 1) * HEAD_DIM] = (
                outs[kv][g * blk:(g + 1) * blk] / denoms[h]).astype(BF16)


def _attn_prompt(proj, sinks, n, t, l, nsub=1):
    blk = WINDOW
    rows = nsub * blk
    steps = t // rows
    kcol, vcol = COL_K // D_KV, COL_V // D_KV
    prev = lambda s, i: s * (t // blk) + jnp.maximum(nsub * i - 1, 0)
    blocks = ([((rows, D_ATTN), BF16)] * 2 + [((rows, D_KV), BF16)] * 2 + [((blk, D_KV), BF16)] * 2)
    return pl.pallas_call(
        functools.partial(_attn_prompt_body, l=l, nsub=nsub),
        grid=(n, steps),
        in_specs=[pl.BlockSpec(memory_space=pltpu.SMEM),
                  pl.BlockSpec((rows, D_ATTN), lambda s, i: (s * steps + i, COL_Q // D_ATTN)),
                  pl.BlockSpec((rows, D_KV), lambda s, i: (s * steps + i, kcol)),
                  pl.BlockSpec((rows, D_KV), lambda s, i: (s * steps + i, vcol)),
                  pl.BlockSpec((blk, D_KV), lambda s, i: (prev(s, i), kcol)),
                  pl.BlockSpec((blk, D_KV), lambda s, i: (prev(s, i), vcol))],
        out_specs=pl.BlockSpec((rows, D_ATTN), lambda s, i: (s * steps + i, 0)),
        out_shape=jax.ShapeDtypeStruct((n * t, D_ATTN), BF16),
        compiler_params=_params(("parallel", "parallel"), blocks,
                                temps=[((GROUP * blk, 2 * blk), F32)] * 16),
        name="attn_prompt",
    )(sinks, proj, proj, proj, proj, proj)


def _attn_sample_body(sink_ref, q_ref, kn_ref, vn_ref, ck_ref, cv_ref, nk_ref, nv_ref, o_ref, *, nb, t, l):
    eye = (lax.broadcasted_iota(jnp.int32, (D_KV, D_KV), 0)
           == lax.broadcasted_iota(jnp.int32, (D_KV, D_KV), 1)).astype(BF16)
    lane = lax.broadcasted_iota(jnp.int32, (D_KV, WINDOW), 1)
    for new_ref, cache_ref, out_ref in ((kn_ref, ck_ref, nk_ref), (vn_ref, cv_ref, nv_ref)):
        tr = lax.dot_general(eye, new_ref[...], (((1,), (1,)), ((), ())), preferred_element_type=F32)
        for s in range(nb):
            kept = pltpu.roll(cache_ref[s], WINDOW - t, axis=1)
            fresh = pltpu.roll(tr, (WINDOW - t - s * t) % WINDOW, axis=1)
            out_ref[s] = jnp.where(lane >= WINDOW - t, fresh, kept)
    q3 = (q_ref[...].astype(F32) * SCALE).reshape(nb, t, D_ATTN)
    rows = GROUP * t
    tq = lax.broadcasted_iota(jnp.int32, (rows, WINDOW), 0) % t
    w = lax.broadcasted_iota(jnp.int32, (rows, WINDOW), 1)
    dist_old = WINDOW + tq - w
    valid_old = (w < t) & (w >= tq)
    dist_new = WINDOW - t + tq - w
    valid_new = dist_new >= 0
    gi = lax.broadcasted_iota(jnp.int32, (rows, 1), 0) // t
    for kv in range(N_KV_HEADS):
        heads = [kv * GROUP + g for g in range(GROUP)]
        slope = jnp.zeros((rows, 1), F32)
        sink = jnp.zeros((rows, 1), F32)
        for g, h in enumerate(heads):
            slope = jnp.where(gi == g, SLOPES[h], slope)
            sink = jnp.where(gi == g, sink_ref[l, h], sink)
        hs = slice(kv * HEAD_DIM, (kv + 1) * HEAD_DIM)
        qs = jnp.concatenate([q3[:, :, h * HEAD_DIM:(h + 1) * HEAD_DIM] for h in heads], axis=1).astype(BF16)
        s_old = jnp.einsum("bqd,bdk->bqk", qs, ck_ref[:, hs, :].astype(BF16), preferred_element_type=F32)
        s_new = jnp.einsum("bqd,bdk->bqk", qs, nk_ref[:, hs, :].astype(BF16), preferred_element_type=F32)
        s_old = jnp.where(valid_old, s_old - slope * dist_old.astype(F32), NEG)
        s_new = jnp.where(valid_new, s_new - slope * dist_new.astype(F32), NEG)
        m = jnp.maximum(jnp.maximum(jnp.max(s_old, axis=-1, keepdims=True),
                                    jnp.max(s_new, axis=-1, keepdims=True)), sink)
        p_old = jnp.exp(s_old - m)
        p_new = jnp.exp(s_new - m)
        denom = (jnp.sum(p_old, axis=-1, keepdims=True) + jnp.sum(p_new, axis=-1, keepdims=True)
                 + jnp.exp(sink - m))
        o = (jnp.einsum("bqk,bdk->bqd", p_old.astype(BF16), cv_ref[:, hs, :].astype(BF16),
                        preferred_element_type=F32)
             + jnp.einsum("bqk,bdk->bqd", p_new.astype(BF16), nv_ref[:, hs, :].astype(BF16),
                          preferred_element_type=F32)) / denom
        for g, h in enumerate(heads):
            o_ref[:, h * HEAD_DIM:(h + 1) * HEAD_DIM] = (
                o[:, g * t:(g + 1) * t, :].reshape(nb * t, HEAD_DIM).astype(BF16))


def _attn_sample(proj, cache_k, cache_v, prev, sinks, n, t, l):
    depth = cache_k.shape[0]
    nb = WINDOW // t
    rows = nb * t
    kcol, vcol = COL_K // D_KV, COL_V // D_KV
    cache = pl.BlockSpec((None, nb, D_KV, WINDOW), lambda s: (l, s, 0, 0))
    stacked = jax.ShapeDtypeStruct((depth, n, D_KV, WINDOW), F32)
    blocks = [((rows, D_ATTN), BF16)] * 2 + [((rows, D_KV), BF16)] * 2 + [((nb, D_KV, WINDOW), F32)] * 4
    return _stacked_call(
        functools.partial(_attn_sample_body, nb=nb, t=t, l=l), l, prev, 2,
        grid=(n // nb,),
        in_specs=[pl.BlockSpec(memory_space=pltpu.SMEM),
                  pl.BlockSpec((rows, D_ATTN), lambda s: (s, COL_Q // D_ATTN)),
                  pl.BlockSpec((rows, D_KV), lambda s: (s, kcol)),
                  pl.BlockSpec((rows, D_KV), lambda s: (s, vcol)),
                  cache, cache],
        out_specs=[cache, cache, pl.BlockSpec((rows, D_ATTN), lambda s: (s, 0))],
        out_shape=[stacked, stacked, jax.ShapeDtypeStruct((n * t, D_ATTN), BF16)],
        compiler_params=_params(("parallel",), blocks,
                                temps=[((nb, D_KV, WINDOW), F32)] * 4 + [((nb, GROUP * t, WINDOW), F32)] * 8,
                                keep_operands_in_hbm=True),
        name="attn_sample",
    )(sinks, proj, proj, proj, cache_k, cache_v)


def _mix_body(c_ref, a_ref, *refs):
    n_gate = D_MODEL // GATE_BLOCK
    gc_refs, ga_refs = refs[:n_gate], refs[n_gate:2 * n_gate]
    x_ref, wco_ref, wao_ref, wo_ref, n2_ref, x1_ref, h2_ref = refs[2 * n_gate:]
    gate_c = jnp.concatenate([g[...] for g in gc_refs], axis=1).astype(F32)
    gate_a = jnp.concatenate([g[...] for g in ga_refs], axis=1).astype(F32)
    branch_c = _dot(c_ref[...], wco_ref[...])
    branch_a = _dot(a_ref[...], wao_ref[...])
    merged = jax.nn.sigmoid(gate_c) * branch_c + jax.nn.sigmoid(gate_a) * branch_a
    x1 = x_ref[...] + _dot(merged.astype(BF16), wo_ref[...])
    x1_ref[...] = x1
    h2_ref[...] = _rms(x1, n2_ref[...]).astype(BF16)


def _mix(c_act, a_act, proj, x, wco, wao, wo, n2, l, tm=256):
    m = x.shape[0]
    row = lambda width, col: pl.BlockSpec((tm, width), lambda i: (i, col))
    const = lambda k: pl.BlockSpec((None, k, D_MODEL), lambda i: (l, 0, 0), pipeline_mode=pl.Buffered(1))
    blocks = ([((tm, D_CONV), BF16), ((tm, D_ATTN), BF16)] + [((tm, D_MODEL), F32)] * 2 + [((tm, D_MODEL), BF16)] * 3)
    weights = [((D_CONV, D_MODEL), BF16), ((D_ATTN, D_MODEL), BF16), ((D_MODEL, D_MODEL), BF16)]
    return pl.pallas_call(
        _mix_body,
        grid=(m // tm,),
        in_specs=[row(D_CONV, 0), row(D_ATTN, 0)]
                 + [row(GATE_BLOCK, COL_GATE_C // GATE_BLOCK + b) for b in range(D_MODEL // GATE_BLOCK)]
                 + [row(GATE_BLOCK, COL_GATE_A // GATE_BLOCK + b) for b in range(D_MODEL // GATE_BLOCK)]
                 + [row(D_MODEL, 0),
                  const(D_CONV), const(D_ATTN), const(D_MODEL),
                  _layer_vec(l, D_MODEL)],
        out_specs=[row(D_MODEL, 0), row(D_MODEL, 0)],
        out_shape=[jax.ShapeDtypeStruct((m, D_MODEL), F32), jax.ShapeDtypeStruct((m, D_MODEL), BF16)],
        compiler_params=_params(("parallel",), blocks, resident=weights, temps=[((tm, D_MODEL), F32)] * 4),
        name="mix",
    )(c_act, a_act, *([proj] * (2 * (D_MODEL // GATE_BLOCK))), x, wco, wao, wo, _as_rows(n2))


def _ffn_conv(e_ref, w_ref, b_ref, tm):
    first = FFN_HALO - (FFN_K - 1)
    out = b_ref[...]
    for j in range(FFN_K):
        out = out + w_ref[j:j + 1, :] * e_ref[first + j:first + j + tm, :]
    return out


def _up_prompt_body(h_ref, wg_ref, wv_ref, cwg_ref, cwv_ref, cbg_ref, cbv_ref, act_ref, sg_ref, sv_ref,
                    eg_ref, ev_ref, *, tm, tiles_per_seq):
    pos = pl.program_id(1) % tiles_per_seq

    @pl.when(pos == 0)
    def _():
        eg_ref[0:FFN_HALO, :] = jnp.zeros((FFN_HALO, eg_ref.shape[1]), F32)
        ev_ref[0:FFN_HALO, :] = jnp.zeros((FFN_HALO, ev_ref.shape[1]), F32)

    h = h_ref[...]
    eg_ref[FFN_HALO:FFN_HALO + tm, :] = _dot(h, wg_ref[...])
    ev_ref[FFN_HALO:FFN_HALO + tm, :] = _dot(h, wv_ref[...])
    gate = _ffn_conv(eg_ref, cwg_ref, cbg_ref, tm)
    val = _ffn_conv(ev_ref, cwv_ref, cbv_ref, tm)
    act_ref[...] = (gate * jax.nn.sigmoid(gate) * val).astype(BF16)

    @pl.when(pos == tiles_per_seq - 1)
    def _():
        hist = FFN_K - 1
        sg_ref[0] = eg_ref[FFN_HALO + tm - hist:FFN_HALO + tm, :]
        sv_ref[0] = ev_ref[FFN_HALO + tm - hist:FFN_HALO + tm, :]

    eg_ref[0:FFN_HALO, :] = eg_ref[tm:tm + FFN_HALO, :]
    ev_ref[0:FFN_HALO, :] = ev_ref[tm:tm + FFN_HALO, :]


def _up_prompt(h2, n, t, wg16, wv16, cw, cb, l, tm=1024, tile=FFN_PROMPT_TILE):
    m = n * t
    tps = t // tm
    nt = D_FF // tile
    hist = FFN_K - 1
    col = lambda rows, off: pl.BlockSpec((None, rows, tile), lambda j, i: (l, 0, j + off))
    weight = pl.BlockSpec((D_MODEL, tile), lambda j, i: (0, j))
    blocks = ([((tm, D_MODEL), BF16)] + [((D_MODEL, tile), BF16)] * 2 + [((tm, tile), BF16)])
    scratch = [((FFN_HALO + tm, tile), F32)] * 2
    state = pl.BlockSpec((1, hist, tile), lambda j, i: (i // tps, 0, j))
    return pl.pallas_call(
        functools.partial(_up_prompt_body, tm=tm, tiles_per_seq=tps),
        grid=(nt, m // tm),
        in_specs=[pl.BlockSpec((tm, D_MODEL), lambda j, i: (i, 0)),
                  weight, weight, col(FFN_K, 0), col(FFN_K, nt), col(1, 0), col(1, nt)],
        out_specs=[pl.BlockSpec((tm, tile), lambda j, i: (i, j)), state, state],
        out_shape=[jax.ShapeDtypeStruct((m, D_FF), BF16),
                   jax.ShapeDtypeStruct((n, hist, D_FF), F32), jax.ShapeDtypeStruct((n, hist, D_FF), F32)],
        scratch_shapes=[pltpu.VMEM(s, d) for s, d in scratch],
        compiler_params=_params(("parallel", "arbitrary"), blocks, resident=scratch,
                                temps=[((tm, tile), F32)] * 4),
        name="up_prompt",
    )(h2, wg16, wv16, cw, cw, _as_rows(cb), _as_rows(cb))


def _ffn_conv_sample(u3, st_ref, w_ref, b_ref, tpos):
    st0 = st_ref[:, 0:1, :]
    st1 = st_ref[:, 1:2, :]
    prev1 = jnp.where(tpos == 0, st1, pltpu.roll(u3, 1, axis=1))
    prev2 = jnp.where(tpos == 0, st0, jnp.where(tpos == 1, st1, pltpu.roll(u3, 2, axis=1)))
    return b_ref[...] + w_ref[0:1, :] * prev2 + w_ref[1:2, :] * prev1 + w_ref[2:3, :] * u3


def _up_sample_body(h_ref, wg_ref, wv_ref, cwg_ref, cwv_ref, cbg_ref, cbv_ref, stg_ref, stv_ref,
                    s_ref, act_ref, wg16_ref, wv16_ref, *, n, t):
    hist = FFN_K - 1
    h = h_ref[...]
    wg16_ref[...] = wg_ref[...].astype(BF16)
    wv16_ref[...] = wv_ref[...].astype(BF16)
    ug = _dot(h, wg16_ref[...]).reshape(n, t, FFN_TILE)
    uv = _dot(h, wv16_ref[...]).reshape(n, t, FFN_TILE)
    tpos = lax.broadcasted_iota(jnp.int32, (1, t, 1), 1)
    gate = _ffn_conv_sample(ug, stg_ref, cwg_ref, cbg_ref, tpos)
    val = _ffn_conv_sample(uv, stv_ref, cwv_ref, cbv_ref, tpos)
    act_ref[...] = (gate * jax.nn.sigmoid(gate) * val).reshape(n * t, FFN_TILE).astype(BF16)
    s_ref[:, 0] = ug[:, t - hist:t, :]
    s_ref[:, 1] = uv[:, t - hist:t, :]


def _up_sample(h2, state, prev, n, t, w_up, cw, cb, l):
    assert t == V7X_SUBLANES and FFN_K == 3
    depth = state.shape[0]
    m = n * t
    nt = D_FF // FFN_TILE
    hist = FFN_K - 1
    col = lambda rows, off: pl.BlockSpec((None, rows, FFN_TILE), lambda j: (l, 0, j + off))
    st = lambda off: pl.BlockSpec((None, n, hist, FFN_TILE), lambda j: (l, 0, 0, j + off))
    new_state = pl.BlockSpec((None, n, 2, hist, FFN_TILE), lambda j: (l, 0, 0, 0, j))
    w16 = pl.BlockSpec((D_MODEL, FFN_TILE), lambda j: (0, j))
    w16_shape = jax.ShapeDtypeStruct((D_MODEL, D_FF), BF16)
    blocks = ([((D_MODEL, FFN_TILE), F32)] * 2 + [((n, V7X_SUBLANES, FFN_TILE), F32)] * 4
              + [((m, FFN_TILE), BF16)] + [((D_MODEL, FFN_TILE), BF16)] * 2)
    return _stacked_call(
        functools.partial(_up_sample_body, n=n, t=t), l, prev, 1,
        grid=(nt,),
        in_specs=[pl.BlockSpec((m, D_MODEL), lambda j: (0, 0)),
                  col(D_MODEL, 0), col(D_MODEL, nt), col(FFN_K, 0), col(FFN_K, nt), col(1, 0), col(1, nt),
                  st(0), st(nt)],
        out_specs=[new_state, pl.BlockSpec((m, FFN_TILE), lambda j: (0, j)), w16, w16],
        out_shape=[jax.ShapeDtypeStruct((depth, n, 2, hist, D_FF), F32), jax.ShapeDtypeStruct((m, D_FF), BF16),
                   w16_shape, w16_shape],
        compiler_params=_params(("parallel",), blocks,
                                resident=[((m, D_MODEL), BF16)] * 2,
                                temps=[((m, FFN_TILE), F32)] * 8),
        name="up_sample",
    )(h2, w_up, w_up, cw, cw, _as_rows(cb), _as_rows(cb), state, state)


def _down_body(act_ref, w_ref, x_ref, o_ref, *refs, cast):
    if cast:
        w16_ref, = refs

        @pl.when(pl.program_id(1) == 0)
        def _():
            w16_ref[...] = w_ref[...].astype(BF16)

        w_ref = w16_ref
    o_ref[...] = x_ref[...] + _dot(act_ref[...], w_ref[...])


def _down(act, w, x1, l, tm):
    m = x1.shape[0]
    cast = w.dtype == F32
    w_spec = (pl.BlockSpec((None, D_FF, DOWN_TILE), lambda j, i: (l, 0, j)) if cast
              else pl.BlockSpec((D_FF, DOWN_TILE), lambda j, i: (0, j)))
    out_specs = [pl.BlockSpec((tm, DOWN_TILE), lambda j, i: (i, j))]
    out_shape = [jax.ShapeDtypeStruct((m, D_MODEL), F32)]
    blocks = [((tm, D_FF), BF16), ((D_FF, DOWN_TILE), w.dtype), ((tm, DOWN_TILE), F32), ((tm, DOWN_TILE), F32)]
    if cast:
        out_specs.append(pl.BlockSpec((D_FF, DOWN_TILE), lambda j, i: (0, j)))
        out_shape.append(jax.ShapeDtypeStruct((D_FF, D_MODEL), BF16))
        blocks.append(((D_FF, DOWN_TILE), BF16))
    outs = pl.pallas_call(
        functools.partial(_down_body, cast=cast),
        grid=(D_MODEL // DOWN_TILE, m // tm),
        in_specs=[pl.BlockSpec((tm, D_FF), lambda j, i: (i, 0)), w_spec,
                  pl.BlockSpec((tm, DOWN_TILE), lambda j, i: (i, j))],
        out_specs=out_specs,
        out_shape=out_shape,
        compiler_params=_params(("parallel", "arbitrary"), blocks, temps=[((tm, DOWN_TILE), F32)]),
        name="down",
    )(act, w, x1)
    return outs if cast else outs[0]


def _final_norm_body(x_ref, g_ref, o_ref):
    o_ref[...] = _rms(x_ref[...], g_ref[...])


def _final_norm(x, g, tm=512):
    m = x.shape[0]
    blocks = [((tm, D_MODEL), F32)] * 2
    return pl.pallas_call(
        _final_norm_body,
        grid=(m // tm,),
        in_specs=[pl.BlockSpec((tm, D_MODEL), lambda i: (i, 0)), pl.BlockSpec((1, D_MODEL), lambda i: (0, 0))],
        out_specs=pl.BlockSpec((tm, D_MODEL), lambda i: (i, 0)),
        out_shape=jax.ShapeDtypeStruct((m, D_MODEL), F32),
        compiler_params=_params(("parallel",), blocks, temps=[((tm, D_MODEL), F32)]),
        name="final_norm",
    )(x, g.reshape(1, D_MODEL))


def kernel(x_prompt, x_sample, cache_k, cache_v, state_conv, state_ffn_conv, norm1_g, w_in, conv_w, conv_b,
           conv_ln_g, conv_ln_b, w_conv_out, attn_sinks, w_attn_out, w_out, norm2_g, w_up, ffn_conv_w,
           ffn_conv_b, w_down, final_norm_g):
    depth = w_in.shape[0]
    n_p, t_p, _ = x_prompt.shape
    n_s, t_s, _ = x_sample.shape
    xp = x_prompt.reshape(n_p * t_p, D_MODEL)
    xs = x_sample.reshape(n_s * t_s, D_MODEL)
    w_co_b, w_ao_b, w_o_b = (w.astype(BF16) for w in (w_conv_out, w_attn_out, w_out))
    keys_on_lanes = lambda c: jnp.transpose(c, (0, 1, 3, 4, 2)).reshape(depth, n_s, D_KV, WINDOW)
    keys_on_rows = lambda c: jnp.transpose(c.reshape(depth, n_s, N_KV_HEADS, HEAD_DIM, WINDOW), (0, 1, 4, 2, 3))
    cache_k, cache_v = keys_on_lanes(cache_k), keys_on_lanes(cache_v)
    rows_first = lambda s: jnp.transpose(s, (0, 2, 1, 3))
    state_conv = rows_first(state_conv)
    outs = {k: [] for k in ("kp", "vp", "cp", "fp")}
    c_states, kv_states, f_states = (), (), ()
    for l in range(depth):
        conv_args = (conv_w, conv_b, conv_ln_g, conv_ln_b, l)
        proj_s, w_in16 = _inproj(xs, norm1_g, w_in, l, tm=n_s * t_s)
        proj = _inproj(xp, norm1_g, w_in16, l, tm=1024)
        c_act, c_state = _conv_prompt(proj, n_p, t_p, *conv_args)
        a_act = _attn_prompt(proj, attn_sinks, n_p, t_p, l)
        x1_p, h2_p = _mix(c_act, a_act, proj, xp, w_co_b, w_ao_b, w_o_b, norm2_g, l)
        kv = proj.reshape(n_p, t_p, IN_COLS)[:, t_p - WINDOW:, :]
        outs["kp"].append(kv[:, :, COL_K:COL_K + D_KV].astype(F32).reshape(n_p, WINDOW, N_KV_HEADS, HEAD_DIM))
        outs["vp"].append(kv[:, :, COL_V:COL_V + D_KV].astype(F32).reshape(n_p, WINDOW, N_KV_HEADS, HEAD_DIM))
        outs["cp"].append(c_state)
        proj = proj_s
        c_stack, c_act = _conv_sample(proj, state_conv, c_states, n_s, t_s, *conv_args)
        k_stack, v_stack, a_act = _attn_sample(proj, cache_k, cache_v, kv_states, attn_sinks, n_s, t_s, l)
        c_states, kv_states = (c_stack,), (k_stack, v_stack)
        x1, h2 = _mix(c_act, a_act, proj, xs, w_co_b, w_ao_b, w_o_b, norm2_g, l)
        f_stack, act, wg16, wv16 = _up_sample(h2, state_ffn_conv, f_states, n_s, t_s, w_up, ffn_conv_w,
                                              ffn_conv_b, l)
        f_states = (f_stack,)
        xs, w_down16 = _down(act, w_down, x1, l, tm=512)
        act, f_g, f_v = _up_prompt(h2_p, n_p, t_p, wg16, wv16, ffn_conv_w, ffn_conv_b, l)
        xp = _down(act, w_down16, x1_p, l, tm=1024)
        outs["fp"].append(jnp.concatenate([f_g, f_v], axis=-1))
    y_prompt = _final_norm(xp, final_norm_g).reshape(n_p, t_p, D_MODEL)
    y_sample = _final_norm(xs, final_norm_g).reshape(n_s, t_s, D_MODEL)
    return (y_prompt, y_sample, jnp.stack(outs["kp"]), jnp.stack(outs["vp"]), jnp.stack(outs["cp"]),
            jnp.stack(outs["fp"]), keys_on_rows(k_stack), keys_on_rows(v_stack), rows_first(c_stack),
            jnp.transpose(f_stack, (0, 1, 3, 2, 4)).reshape(depth, n_s, FFN_K - 1, 2 * D_FF))
```

```python
import functools

import jax
import jax.numpy as jnp
from jax import lax
from jax.experimental import pallas as pl
from jax.experimental.pallas import tpu as pltpu

F32 = jnp.float32
BF16 = jnp.bfloat16

D_MODEL = 2048
HEAD_DIM = 64
N_HEADS = 16
N_KV_HEADS = 4
GROUP = N_HEADS // N_KV_HEADS
D_ATTN = N_HEADS * HEAD_DIM
D_KV = N_KV_HEADS * HEAD_DIM
WINDOW = 128
D_CONV = D_MODEL // 2
CONV_K = 31
D_FF = 3 * D_MODEL
FFN_K = 3
EPS = 1e-6
IN_COLS = 2 * D_CONV + D_ATTN + 2 * D_KV + 2 * D_MODEL
NEG = -1e30
SCALE = HEAD_DIM ** -0.5
SLOPES = tuple(2.0 ** (-8.0 * (h + 1) / N_HEADS) for h in range(N_HEADS))

V7X_SUBLANES = 8
V7X_SCOPED_VMEM_CAP_BYTES = 60000 * 1024

IN_TILE = 1280
COL_GLU_A = 0
COL_GLU_G = D_CONV
COL_Q = 2 * D_CONV
COL_K = COL_Q + D_ATTN
COL_V = COL_K + D_KV
COL_GATE_C = COL_V + D_KV
COL_GATE_A = COL_GATE_C + D_MODEL
GATE_BLOCK = 512

CONV_HALO = 32
CONV_ROW_CHUNK = 64
CONV_LANE_CHUNK = 256
FFN_HALO = V7X_SUBLANES
FFN_TILE = 512
FFN_PROMPT_TILE = 512
DOWN_TILE = 512


def _nbytes(shape, dtype):
    n = 1
    for s in shape:
        n *= s
    return n * jnp.dtype(dtype).itemsize


def _params(semantics, pipelined, resident=(), temps=(), keep_operands_in_hbm=True):
    est = 2 * sum(_nbytes(s, d) for s, d in pipelined)
    est += sum(_nbytes(s, d) for s, d in resident)
    est += sum(_nbytes(s, d) for s, d in temps)
    limit = min(V7X_SCOPED_VMEM_CAP_BYTES, est + est // 4)
    if keep_operands_in_hbm:
        limit = V7X_SCOPED_VMEM_CAP_BYTES
    return pltpu.CompilerParams(dimension_semantics=semantics, vmem_limit_bytes=limit)


def _layer_vec(l, width, rows=1):
    return pl.BlockSpec((None, rows, width), lambda *_: (l, 0, 0))


def _as_rows(p):
    return p.reshape(p.shape[0], 1, p.shape[1])


def _rms(x, g):
    return x * lax.rsqrt(jnp.mean(x * x, axis=-1, keepdims=True) + EPS) * g


def _dot(a, b):
    return jnp.dot(a, b, preferred_element_type=F32)


def _skip_first_ref(body):
    def wrapped(_, *refs):
        body(*refs)
    return wrapped


def _stacked_call(body, l, prev, n_alias_out, **kw):
    if l == 0:
        return pl.pallas_call(body, **kw)
    for _ in prev:
        body = _skip_first_ref(body)
    kw["in_specs"] = [pl.BlockSpec(memory_space=pl.ANY)] * len(prev) + list(kw["in_specs"])
    call = pl.pallas_call(body, input_output_aliases={i: i for i in range(n_alias_out)}, **kw)
    return lambda *args: call(*prev, *args)


def _inproj_body(x_ref, g_ref, w_ref, o_ref, *refs, cast):
    h_ref = refs[-1]

    @pl.when(pl.program_id(1) == 0)
    def _():
        h_ref[...] = _rms(x_ref[...], g_ref[...]).astype(BF16)

    if cast:
        w16_ref = refs[0]
        w16_ref[...] = w_ref[...].astype(BF16)
        w_ref = w16_ref
    o_ref[...] = _dot(h_ref[...], w_ref[...]).astype(BF16)


def _inproj(x, g, w, l, tm):
    m = x.shape[0]
    cast = w.dtype == F32
    assert not cast or m == tm
    w_spec = (pl.BlockSpec((None, D_MODEL, IN_TILE), lambda i, j: (l, 0, j)) if cast
              else pl.BlockSpec((D_MODEL, IN_TILE), lambda i, j: (0, j)))
    out_specs = [pl.BlockSpec((tm, IN_TILE), lambda i, j: (i, j))]
    out_shape = [jax.ShapeDtypeStruct((m, IN_COLS), BF16)]
    blocks = [((tm, D_MODEL), F32), ((D_MODEL, IN_TILE), w.dtype), ((tm, IN_TILE), BF16)]
    if cast:
        out_specs.append(pl.BlockSpec((D_MODEL, IN_TILE), lambda i, j: (0, j)))
        out_shape.append(jax.ShapeDtypeStruct((D_MODEL, IN_COLS), BF16))
        blocks.append(((D_MODEL, IN_TILE), BF16))
    outs = pl.pallas_call(
        functools.partial(_inproj_body, cast=cast),
        grid=(m // tm, IN_COLS // IN_TILE),
        in_specs=[pl.BlockSpec((tm, D_MODEL), lambda i, j: (i, 0)), _layer_vec(l, D_MODEL), w_spec],
        out_specs=out_specs,
        out_shape=out_shape,
        scratch_shapes=[pltpu.VMEM((tm, D_MODEL), BF16)],
        compiler_params=_params(("parallel", "arbitrary"), blocks, resident=[((tm, D_MODEL), BF16)],
                                temps=[((tm, D_MODEL), F32)]),
        name="inproj",
    )(x, _as_rows(g), w)
    return outs if cast else outs[0]


def _ln_silu(c, g, b):
    mu = jnp.mean(c, axis=-1, keepdims=True)
    xc = c - mu
    y = xc * lax.rsqrt(jnp.mean(xc * xc, axis=-1, keepdims=True) + EPS) * g + b
    return y * jax.nn.sigmoid(y)


def _conv_prompt_body(a_ref, g_ref, w_ref, b_ref, lng_ref, lnb_ref, act_ref, st_ref, xx_ref, cv_ref, sh_ref,
                      *, tt):
    i = pl.program_id(1)

    @pl.when(i == 0)
    def _():
        xx_ref[0:CONV_HALO, :] = jnp.zeros((CONV_HALO, D_CONV), F32)

    xx_ref[CONV_HALO:CONV_HALO + tt, :] = a_ref[...].astype(F32) * jax.nn.sigmoid(g_ref[...].astype(F32))

    first = CONV_HALO - (CONV_K - 1)
    sh_rows = sh_ref.shape[1]
    for lc in range(D_CONV // CONV_LANE_CHUNK):
        ls = pl.ds(lc * CONV_LANE_CHUNK, CONV_LANE_CHUNK)
        for r in range(1, V7X_SUBLANES):
            sh_ref[r - 1] = xx_ref[r:r + sh_rows, ls]
        for r0 in range(0, tt, CONV_ROW_CHUNK):
            acc = jnp.broadcast_to(b_ref[:, ls], (CONV_ROW_CHUNK, CONV_LANE_CHUNK))
            for j in range(CONV_K):
                q, r = divmod(first + j, V7X_SUBLANES)
                a0 = r0 + V7X_SUBLANES * q
                if r == 0:
                    window = xx_ref[a0:a0 + CONV_ROW_CHUNK, ls]
                else:
                    window = sh_ref[r - 1, a0:a0 + CONV_ROW_CHUNK, :]
                acc = acc + w_ref[j:j + 1, ls] * window
            cv_ref[r0:r0 + CONV_ROW_CHUNK, ls] = acc
    act_ref[...] = _ln_silu(cv_ref[...], lng_ref[...], lnb_ref[...]).astype(BF16)

    @pl.when(i == pl.num_programs(1) - 1)
    def _():
        st_ref[0] = xx_ref[tt + first:tt + CONV_HALO, :]

    xx_ref[0:CONV_HALO, :] = xx_ref[tt:tt + CONV_HALO, :]


def _conv_prompt(proj, n, t, w, b, lng, lnb, l, tt=512):
    tps = t // tt
    vec = _layer_vec(l, D_CONV)
    blocks = [((tt, D_CONV), BF16)] * 3
    shifted = (V7X_SUBLANES - 1, tt + CONV_HALO - V7X_SUBLANES, CONV_LANE_CHUNK)
    return pl.pallas_call(
        functools.partial(_conv_prompt_body, tt=tt),
        grid=(n, tps),
        in_specs=[pl.BlockSpec((tt, D_CONV), lambda s, i: (s * tps + i, COL_GLU_A // D_CONV)),
                  pl.BlockSpec((tt, D_CONV), lambda s, i: (s * tps + i, COL_GLU_G // D_CONV)),
                  _layer_vec(l, D_CONV, CONV_K), vec, vec, vec],
        out_specs=[pl.BlockSpec((tt, D_CONV), lambda s, i: (s * tps + i, 0)),
                   pl.BlockSpec((1, CONV_K - 1, D_CONV), lambda s, i: (s, 0, 0))],
        out_shape=[jax.ShapeDtypeStruct((n * t, D_CONV), BF16),
                   jax.ShapeDtypeStruct((n, CONV_K - 1, D_CONV), F32)],
        scratch_shapes=[pltpu.VMEM((CONV_HALO + tt, D_CONV), F32), pltpu.VMEM((tt, D_CONV), F32),
                        pltpu.VMEM(shifted, F32)],
        compiler_params=_params(("parallel", "arbitrary"), blocks,
                                resident=[((CONV_HALO + tt, D_CONV), F32), ((tt, D_CONV), F32), (shifted, F32)],
                                temps=[((tt, D_CONV), F32)] * 4),
        name="conv_prompt",
    )(proj, proj, w, _as_rows(b), _as_rows(lng), _as_rows(lnb))


def _conv_sample_body(a_ref, g_ref, s_ref, w_ref, b_ref, lng_ref, lnb_ref, st_ref, act_ref, new_ref, y_ref,
                      *, nb, t):
    hist = CONV_K - 1
    u = (a_ref[...].astype(F32) * jax.nn.sigmoid(g_ref[...].astype(F32))).reshape(nb, t, D_CONV)
    for j in range(t):
        new_ref[j] = u[:, j, :]
    for r in range(hist):
        st_ref[r] = s_ref[r + t] if r + t < hist else new_ref[r + t - hist]
    for j in range(t):
        for lc in range(D_CONV // CONV_LANE_CHUNK):
            ls = pl.ds(lc * CONV_LANE_CHUNK, CONV_LANE_CHUNK)
            acc = jnp.broadcast_to(b_ref[:, ls], (nb, CONV_LANE_CHUNK))
            for k in range(CONV_K):
                r = j + k
                plane = s_ref[r, :, ls] if r < hist else new_ref[r - hist, :, ls]
                acc = acc + w_ref[k:k + 1, ls] * plane
            y_ref[j, :, ls] = acc
    y = _ln_silu(y_ref[...], lng_ref[...], lnb_ref[...])
    act_ref[...] = jnp.stack([y[j] for j in range(t)], axis=1).reshape(nb * t, D_CONV).astype(BF16)


def _conv_sample(proj, state, prev, n, t, w, b, lng, lnb, l, nb=32):
    depth = state.shape[0]
    hist = CONV_K - 1
    rows = nb * t
    vec = _layer_vec(l, D_CONV)
    st = pl.BlockSpec((None, hist, nb, D_CONV), lambda s: (l, 0, s, 0))
    blocks = [((rows, D_CONV), BF16)] * 3 + [((hist, nb, D_CONV), F32)] * 2
    scratch = [((t, nb, D_CONV), F32)] * 2
    return _stacked_call(
        functools.partial(_conv_sample_body, nb=nb, t=t), l, prev, 1,
        grid=(n // nb,),
        in_specs=[pl.BlockSpec((rows, D_CONV), lambda s: (s, COL_GLU_A // D_CONV)),
                  pl.BlockSpec((rows, D_CONV), lambda s: (s, COL_GLU_G // D_CONV)),
                  st, _layer_vec(l, D_CONV, CONV_K), vec, vec, vec],
        out_specs=[st, pl.BlockSpec((rows, D_CONV), lambda s: (s, 0))],
        out_shape=[jax.ShapeDtypeStruct((depth, hist, n, D_CONV), F32),
                   jax.ShapeDtypeStruct((n * t, D_CONV), BF16)],
        scratch_shapes=[pltpu.VMEM(s, d) for s, d in scratch],
        compiler_params=_params(("parallel",), blocks, resident=scratch,
                                temps=[((rows, D_CONV), F32)] * 4, keep_operands_in_hbm=True),
        name="conv_sample",
    )(proj, proj, state, w, _as_rows(b), _as_rows(lng), _as_rows(lnb))


def _softmax_terms(s, valid, distf, slope, sink):
    s = jnp.where(valid, s - slope * distf, NEG)
    m = jnp.maximum(jnp.max(s, axis=-1, keepdims=True), sink)
    p = jnp.exp(s - m)
    denom = jnp.sum(p, axis=-1, keepdims=True) + jnp.exp(sink - m)
    return p, denom


def _attn_prompt_body(sink_ref, q_ref, kc_ref, vc_ref, kp_ref, vp_ref, o_ref, *, l, nsub):
    i = pl.program_id(1)
    blk = WINDOW
    q_all = q_ref[...] * SCALE
    k_all = jnp.concatenate([kp_ref[...], kc_ref[...]], axis=0)
    v_all = jnp.concatenate([vp_ref[...], vc_ref[...]], axis=0)
    r = lax.broadcasted_iota(jnp.int32, (blk, 2 * blk), 0)
    c = lax.broadcasted_iota(jnp.int32, (blk, 2 * blk), 1)
    dist = blk + r - c
    in_window = (dist >= 0) & (dist <= WINDOW)
    distf = dist.astype(F32)
    for sb in range(nsub):
        q = q_all[sb * blk:(sb + 1) * blk]
        k = k_all[sb * blk:(sb + 2) * blk]
        v = v_all[sb * blk:(sb + 2) * blk]
        valid = in_window & ((c >= blk) | (i > 0)) if sb == 0 else in_window
        scores = []
        for kv in range(N_KV_HEADS):
            kh = k[:, kv * HEAD_DIM:(kv + 1) * HEAD_DIM]
            heads = [kv * GROUP + g for g in range(GROUP)]
            qs = jnp.concatenate([q[:, h * HEAD_DIM:(h + 1) * HEAD_DIM] for h in heads], axis=0)
            scores.append(lax.dot_general(qs, kh, (((1,), (1,)), ((), ())),
                                          preferred_element_type=F32))
        probs, denoms = [], []
        for kv in range(N_KV_HEADS):
            ps = []
            for g in range(GROUP):
                h = kv * GROUP + g
                p, d = _softmax_terms(scores[kv][g * blk:(g + 1) * blk], valid, distf, SLOPES[h], sink_ref[l, h])
                ps.append(p.astype(BF16))
                denoms.append(d)
            probs.append(jnp.concatenate(ps, axis=0))
        outs = [_dot(probs[kv], v[:, kv * HEAD_DIM:(kv + 1) * HEAD_DIM]) for kv in range(N_KV_HEADS)]
        for h in range(N_HEADS):
            kv, g = divmod(h, GROUP)
            o_ref[sb * blk:(sb + 1) * blk, h * HEAD_DIM:(h + 1) * HEAD_DIM] = (
                outs[kv][g * blk:(g + 1) * blk] / denoms[h]).astype(BF16)


def _attn_prompt(proj, sinks, n, t, l, nsub=1):
    blk = WINDOW
    rows = nsub * blk
    steps = t // rows
    kcol, vcol = COL_K // D_KV, COL_V // D_KV
    prev = lambda s, i: s * (t // blk) + jnp.maximum(nsub * i - 1, 0)
    blocks = ([((rows, D_ATTN), BF16)] * 2 + [((rows, D_KV), BF16)] * 2 + [((blk, D_KV), BF16)] * 2)
    return pl.pallas_call(
        functools.partial(_attn_prompt_body, l=l, nsub=nsub),
        grid=(n, steps),
        in_specs=[pl.BlockSpec(memory_space=pltpu.SMEM),
                  pl.BlockSpec((rows, D_ATTN), lambda s, i: (s * steps + i, COL_Q // D_ATTN)),
                  pl.BlockSpec((rows, D_KV), lambda s, i: (s * steps + i, kcol)),
                  pl.BlockSpec((rows, D_KV), lambda s, i: (s * steps + i, vcol)),
                  pl.BlockSpec((blk, D_KV), lambda s, i: (prev(s, i), kcol)),
                  pl.BlockSpec((blk, D_KV), lambda s, i: (prev(s, i), vcol))],
        out_specs=pl.BlockSpec((rows, D_ATTN), lambda s, i: (s * steps + i, 0)),
        out_shape=jax.ShapeDtypeStruct((n * t, D_ATTN), BF16),
        compiler_params=_params(("parallel", "parallel"), blocks,
                                temps=[((GROUP * blk, 2 * blk), F32)] * 16),
        name="attn_prompt",
    )(sinks, proj, proj, proj, proj, proj)


def _attn_sample_body(sink_ref, q_ref, kn_ref, vn_ref, ck_ref, cv_ref, nk_ref, nv_ref, o_ref, *, nb, t, l):
    eye = (lax.broadcasted_iota(jnp.int32, (D_KV, D_KV), 0)
           == lax.broadcasted_iota(jnp.int32, (D_KV, D_KV), 1)).astype(BF16)
    lane = lax.broadcasted_iota(jnp.int32, (D_KV, WINDOW), 1)
    for new_ref, cache_ref, out_ref in ((kn_ref, ck_ref, nk_ref), (vn_ref, cv_ref, nv_ref)):
        tr = lax.dot_general(eye, new_ref[...], (((1,), (1,)), ((), ())), preferred_element_type=F32)
        for s in range(nb):
            kept = pltpu.roll(cache_ref[s], WINDOW - t, axis=1)
            fresh = pltpu.roll(tr, (WINDOW - t - s * t) % WINDOW, axis=1)
            out_ref[s] = jnp.where(lane >= WINDOW - t, fresh, kept)
    q3 = (q_ref[...].astype(F32) * SCALE).reshape(nb, t, D_ATTN)
    rows = GROUP * t
    tq = lax.broadcasted_iota(jnp.int32, (rows, WINDOW), 0) % t
    w = lax.broadcasted_iota(jnp.int32, (rows, WINDOW), 1)
    dist_old = WINDOW + tq - w
    valid_old = (w < t) & (w >= tq)
    dist_new = WINDOW - t + tq - w
    valid_new = dist_new >= 0
    gi = lax.broadcasted_iota(jnp.int32, (rows, 1), 0) // t
    for kv in range(N_KV_HEADS):
        heads = [kv * GROUP + g for g in range(GROUP)]
        slope = jnp.zeros((rows, 1), F32)
        sink = jnp.zeros((rows, 1), F32)
        for g, h in enumerate(heads):
            slope = jnp.where(gi == g, SLOPES[h], slope)
            sink = jnp.where(gi == g, sink_ref[l, h], sink)
        hs = slice(kv * HEAD_DIM, (kv + 1) * HEAD_DIM)
        qs = jnp.concatenate([q3[:, :, h * HEAD_DIM:(h + 1) * HEAD_DIM] for h in heads], axis=1).astype(BF16)
        s_old = jnp.einsum("bqd,bdk->bqk", qs, ck_ref[:, hs, :].astype(BF16), preferred_element_type=F32)
        s_new = jnp.einsum("bqd,bdk->bqk", qs, nk_ref[:, hs, :].astype(BF16), preferred_element_type=F32)
        s_old = jnp.where(valid_old, s_old - slope * dist_old.astype(F32), NEG)
        s_new = jnp.where(valid_new, s_new - slope * dist_new.astype(F32), NEG)
        m = jnp.maximum(jnp.maximum(jnp.max(s_old, axis=-1, keepdims=True),
                                    jnp.max(s_new, axis=-1, keepdims=True)), sink)
        p_old = jnp.exp(s_old - m)
        p_new = jnp.exp(s_new - m)
        denom = (jnp.sum(p_old, axis=-1, keepdims=True) + jnp.sum(p_new, axis=-1, keepdims=True)
                 + jnp.exp(sink - m))
        o = (jnp.einsum("bqk,bdk->bqd", p_old.astype(BF16), cv_ref[:, hs, :].astype(BF16),
                        preferred_element_type=F32)
             + jnp.einsum("bqk,bdk->bqd", p_new.astype(BF16), nv_ref[:, hs, :].astype(BF16),
                          preferred_element_type=F32)) / denom
        for g, h in enumerate(heads):
            o_ref[:, h * HEAD_DIM:(h + 1) * HEAD_DIM] = (
                o[:, g * t:(g + 1) * t, :].reshape(nb * t, HEAD_DIM).astype(BF16))


def _attn_sample(proj, cache_k, cache_v, prev, sinks, n, t, l):
    depth = cache_k.shape[0]
    nb = WINDOW // t
    rows = nb * t
    kcol, vcol = COL_K // D_KV, COL_V // D_KV
    cache = pl.BlockSpec((None, nb, D_KV, WINDOW), lambda s: (l, s, 0, 0))
    stacked = jax.ShapeDtypeStruct((depth, n, D_KV, WINDOW), F32)
    blocks = [((rows, D_ATTN), BF16)] * 2 + [((rows, D_KV), BF16)] * 2 + [((nb, D_KV, WINDOW), F32)] * 4
    return _stacked_call(
        functools.partial(_attn_sample_body, nb=nb, t=t, l=l), l, prev, 2,
        grid=(n // nb,),
        in_specs=[pl.BlockSpec(memory_space=pltpu.SMEM),
                  pl.BlockSpec((rows, D_ATTN), lambda s: (s, COL_Q // D_ATTN)),
                  pl.BlockSpec((rows, D_KV), lambda s: (s, kcol)),
                  pl.BlockSpec((rows, D_KV), lambda s: (s, vcol)),
                  cache, cache],
        out_specs=[cache, cache, pl.BlockSpec((rows, D_ATTN), lambda s: (s, 0))],
        out_shape=[stacked, stacked, jax.ShapeDtypeStruct((n * t, D_ATTN), BF16)],
        compiler_params=_params(("parallel",), blocks,
                                temps=[((nb, D_KV, WINDOW), F32)] * 4 + [((nb, GROUP * t, WINDOW), F32)] * 8,
                                keep_operands_in_hbm=True),
        name="attn_sample",
    )(sinks, proj, proj, proj, cache_k, cache_v)


def _mix_body(c_ref, a_ref, *refs):
    n_gate = D_MODEL // GATE_BLOCK
    gc_refs, ga_refs = refs[:n_gate], refs[n_gate:2 * n_gate]
    x_ref, wco_ref, wao_ref, wo_ref, n2_ref, x1_ref, h2_ref = refs[2 * n_gate:]
    gate_c = jnp.concatenate([g[...] for g in gc_refs], axis=1).astype(F32)
    gate_a = jnp.concatenate([g[...] for g in ga_refs], axis=1).astype(F32)
    branch_c = _dot(c_ref[...], wco_ref[...])
    branch_a = _dot(a_ref[...], wao_ref[...])
    merged = jax.nn.sigmoid(gate_c) * branch_c + jax.nn.sigmoid(gate_a) * branch_a
    x1 = x_ref[...] + _dot(merged.astype(BF16), wo_ref[...])
    x1_ref[...] = x1
    h2_ref[...] = _rms(x1, n2_ref[...]).astype(BF16)


def _mix(c_act, a_act, proj, x, wco, wao, wo, n2, l, tm=512):
    m = x.shape[0]
    row = lambda width, col: pl.BlockSpec((tm, width), lambda i: (i, col))
    const = lambda k: pl.BlockSpec((None, k, D_MODEL), lambda i: (l, 0, 0), pipeline_mode=pl.Buffered(1))
    blocks = ([((tm, D_CONV), BF16), ((tm, D_ATTN), BF16)] + [((tm, D_MODEL), F32)] * 2 + [((tm, D_MODEL), BF16)] * 3)
    weights = [((D_CONV, D_MODEL), BF16), ((D_ATTN, D_MODEL), BF16), ((D_MODEL, D_MODEL), BF16)]
    return pl.pallas_call(
        _mix_body,
        grid=(m // tm,),
        in_specs=[row(D_CONV, 0), row(D_ATTN, 0)]
                 + [row(GATE_BLOCK, COL_GATE_C // GATE_BLOCK + b) for b in range(D_MODEL // GATE_BLOCK)]
                 + [row(GATE_BLOCK, COL_GATE_A // GATE_BLOCK + b) for b in range(D_MODEL // GATE_BLOCK)]
                 + [row(D_MODEL, 0),
                  const(D_CONV), const(D_ATTN), const(D_MODEL),
                  _layer_vec(l, D_MODEL)],
        out_specs=[row(D_MODEL, 0), row(D_MODEL, 0)],
        out_shape=[jax.ShapeDtypeStruct((m, D_MODEL), F32), jax.ShapeDtypeStruct((m, D_MODEL), BF16)],
        compiler_params=_params(("parallel",), blocks, resident=weights, temps=[((tm, D_MODEL), F32)] * 4),
        name="mix",
    )(c_act, a_act, *([proj] * (2 * (D_MODEL // GATE_BLOCK))), x, wco, wao, wo, _as_rows(n2))


def _ffn_conv(e_ref, w_ref, b_ref, tm):
    first = FFN_HALO - (FFN_K - 1)
    out = b_ref[...]
    for j in range(FFN_K):
        out = out + w_ref[j:j + 1, :] * e_ref[first + j:first + j + tm, :]
    return out


def _up_prompt_body(h_ref, wg_ref, wv_ref, cwg_ref, cwv_ref, cbg_ref, cbv_ref, act_ref, sg_ref, sv_ref,
                    eg_ref, ev_ref, *, tm, tiles_per_seq):
    pos = pl.program_id(1) % tiles_per_seq

    @pl.when(pos == 0)
    def _():
        eg_ref[0:FFN_HALO, :] = jnp.zeros((FFN_HALO, eg_ref.shape[1]), F32)
        ev_ref[0:FFN_HALO, :] = jnp.zeros((FFN_HALO, ev_ref.shape[1]), F32)

    h = h_ref[...]
    eg_ref[FFN_HALO:FFN_HALO + tm, :] = _dot(h, wg_ref[...])
    ev_ref[FFN_HALO:FFN_HALO + tm, :] = _dot(h, wv_ref[...])
    gate = _ffn_conv(eg_ref, cwg_ref, cbg_ref, tm)
    val = _ffn_conv(ev_ref, cwv_ref, cbv_ref, tm)
    act_ref[...] = (gate * jax.nn.sigmoid(gate) * val).astype(BF16)

    @pl.when(pos == tiles_per_seq - 1)
    def _():
        hist = FFN_K - 1
        sg_ref[0] = eg_ref[FFN_HALO + tm - hist:FFN_HALO + tm, :]
        sv_ref[0] = ev_ref[FFN_HALO + tm - hist:FFN_HALO + tm, :]

    eg_ref[0:FFN_HALO, :] = eg_ref[tm:tm + FFN_HALO, :]
    ev_ref[0:FFN_HALO, :] = ev_ref[tm:tm + FFN_HALO, :]


def _up_prompt(h2, n, t, wg16, wv16, cw, cb, l, tm=1024, tile=FFN_PROMPT_TILE):
    m = n * t
    tps = t // tm
    nt = D_FF // tile
    hist = FFN_K - 1
    col = lambda rows, off: pl.BlockSpec((None, rows, tile), lambda j, i: (l, 0, j + off))
    weight = pl.BlockSpec((D_MODEL, tile), lambda j, i: (0, j))
    blocks = ([((tm, D_MODEL), BF16)] + [((D_MODEL, tile), BF16)] * 2 + [((tm, tile), BF16)])
    scratch = [((FFN_HALO + tm, tile), F32)] * 2
    state = pl.BlockSpec((1, hist, tile), lambda j, i: (i // tps, 0, j))
    return pl.pallas_call(
        functools.partial(_up_prompt_body, tm=tm, tiles_per_seq=tps),
        grid=(nt, m // tm),
        in_specs=[pl.BlockSpec((tm, D_MODEL), lambda j, i: (i, 0)),
                  weight, weight, col(FFN_K, 0), col(FFN_K, nt), col(1, 0), col(1, nt)],
        out_specs=[pl.BlockSpec((tm, tile), lambda j, i: (i, j)), state, state],
        out_shape=[jax.ShapeDtypeStruct((m, D_FF), BF16),
                   jax.ShapeDtypeStruct((n, hist, D_FF), F32), jax.ShapeDtypeStruct((n, hist, D_FF), F32)],
        scratch_shapes=[pltpu.VMEM(s, d) for s, d in scratch],
        compiler_params=_params(("parallel", "arbitrary"), blocks, resident=scratch,
                                temps=[((tm, tile), F32)] * 4),
        name="up_prompt",
    )(h2, wg16, wv16, cw, cw, _as_rows(cb), _as_rows(cb))


def _ffn_conv_sample(u3, st_ref, w_ref, b_ref, tpos):
    st0 = st_ref[:, 0:1, :]
    st1 = st_ref[:, 1:2, :]
    prev1 = jnp.where(tpos == 0, st1, pltpu.roll(u3, 1, axis=1))
    prev2 = jnp.where(tpos == 0, st0, jnp.where(tpos == 1, st1, pltpu.roll(u3, 2, axis=1)))
    return b_ref[...] + w_ref[0:1, :] * prev2 + w_ref[1:2, :] * prev1 + w_ref[2:3, :] * u3


def _up_sample_body(h_ref, wg_ref, wv_ref, cwg_ref, cwv_ref, cbg_ref, cbv_ref, stg_ref, stv_ref,
                    s_ref, act_ref, wg16_ref, wv16_ref, *, n, t):
    hist = FFN_K - 1
    h = h_ref[...]
    wg16_ref[...] = wg_ref[...].astype(BF16)
    wv16_ref[...] = wv_ref[...].astype(BF16)
    ug = _dot(h, wg16_ref[...]).reshape(n, t, FFN_TILE)
    uv = _dot(h, wv16_ref[...]).reshape(n, t, FFN_TILE)
    tpos = lax.broadcasted_iota(jnp.int32, (1, t, 1), 1)
    gate = _ffn_conv_sample(ug, stg_ref, cwg_ref, cbg_ref, tpos)
    val = _ffn_conv_sample(uv, stv_ref, cwv_ref, cbv_ref, tpos)
    act_ref[...] = (gate * jax.nn.sigmoid(gate) * val).reshape(n * t, FFN_TILE).astype(BF16)
    s_ref[:, 0] = ug[:, t - hist:t, :]
    s_ref[:, 1] = uv[:, t - hist:t, :]


def _up_sample(h2, state, prev, n, t, w_up, cw, cb, l):
    assert t == V7X_SUBLANES and FFN_K == 3
    depth = state.shape[0]
    m = n * t
    nt = D_FF // FFN_TILE
    hist = FFN_K - 1
    col = lambda rows, off: pl.BlockSpec((None, rows, FFN_TILE), lambda j: (l, 0, j + off))
    st = lambda off: pl.BlockSpec((None, n, hist, FFN_TILE), lambda j: (l, 0, 0, j + off))
    new_state = pl.BlockSpec((None, n, 2, hist, FFN_TILE), lambda j: (l, 0, 0, 0, j))
    w16 = pl.BlockSpec((D_MODEL, FFN_TILE), lambda j: (0, j))
    w16_shape = jax.ShapeDtypeStruct((D_MODEL, D_FF), BF16)
    blocks = ([((D_MODEL, FFN_TILE), F32)] * 2 + [((n, V7X_SUBLANES, FFN_TILE), F32)] * 4
              + [((m, FFN_TILE), BF16)] + [((D_MODEL, FFN_TILE), BF16)] * 2)
    return _stacked_call(
        functools.partial(_up_sample_body, n=n, t=t), l, prev, 1,
        grid=(nt,),
        in_specs=[pl.BlockSpec((m, D_MODEL), lambda j: (0, 0)),
                  col(D_MODEL, 0), col(D_MODEL, nt), col(FFN_K, 0), col(FFN_K, nt), col(1, 0), col(1, nt),
                  st(0), st(nt)],
        out_specs=[new_state, pl.BlockSpec((m, FFN_TILE), lambda j: (0, j)), w16, w16],
        out_shape=[jax.ShapeDtypeStruct((depth, n, 2, hist, D_FF), F32), jax.ShapeDtypeStruct((m, D_FF), BF16),
                   w16_shape, w16_shape],
        compiler_params=_params(("parallel",), blocks,
                                resident=[((m, D_MODEL), BF16)] * 2,
                                temps=[((m, FFN_TILE), F32)] * 8),
        name="up_sample",
    )(h2, w_up, w_up, cw, cw, _as_rows(cb), _as_rows(cb), state, state)


def _down_body(act_ref, w_ref, x_ref, o_ref, *refs, cast):
    if cast:
        w16_ref, = refs

        @pl.when(pl.program_id(1) == 0)
        def _():
            w16_ref[...] = w_ref[...].astype(BF16)

        w_ref = w16_ref
    o_ref[...] = x_ref[...] + _dot(act_ref[...], w_ref[...])


def _down(act, w, x1, l, tm):
    m = x1.shape[0]
    cast = w.dtype == F32
    w_spec = (pl.BlockSpec((None, D_FF, DOWN_TILE), lambda j, i: (l, 0, j)) if cast
              else pl.BlockSpec((D_FF, DOWN_TILE), lambda j, i: (0, j)))
    out_specs = [pl.BlockSpec((tm, DOWN_TILE), lambda j, i: (i, j))]
    out_shape = [jax.ShapeDtypeStruct((m, D_MODEL), F32)]
    blocks = [((tm, D_FF), BF16), ((D_FF, DOWN_TILE), w.dtype), ((tm, DOWN_TILE), F32), ((tm, DOWN_TILE), F32)]
    if cast:
        out_specs.append(pl.BlockSpec((D_FF, DOWN_TILE), lambda j, i: (0, j)))
        out_shape.append(jax.ShapeDtypeStruct((D_FF, D_MODEL), BF16))
        blocks.append(((D_FF, DOWN_TILE), BF16))
    outs = pl.pallas_call(
        functools.partial(_down_body, cast=cast),
        grid=(D_MODEL // DOWN_TILE, m // tm),
        in_specs=[pl.BlockSpec((tm, D_FF), lambda j, i: (i, 0)), w_spec,
                  pl.BlockSpec((tm, DOWN_TILE), lambda j, i: (i, j))],
        out_specs=out_specs,
        out_shape=out_shape,
        compiler_params=_params(("parallel", "arbitrary"), blocks, temps=[((tm, DOWN_TILE), F32)]),
        name="down",
    )(act, w, x1)
    return outs if cast else outs[0]


def _final_norm_body(x_ref, g_ref, o_ref):
    o_ref[...] = _rms(x_ref[...], g_ref[...])


def _final_norm(x, g, tm=512):
    m = x.shape[0]
    blocks = [((tm, D_MODEL), F32)] * 2
    return pl.pallas_call(
        _final_norm_body,
        grid=(m // tm,),
        in_specs=[pl.BlockSpec((tm, D_MODEL), lambda i: (i, 0)), pl.BlockSpec((1, D_MODEL), lambda i: (0, 0))],
        out_specs=pl.BlockSpec((tm, D_MODEL), lambda i: (i, 0)),
        out_shape=jax.ShapeDtypeStruct((m, D_MODEL), F32),
        compiler_params=_params(("parallel",), blocks, temps=[((tm, D_MODEL), F32)]),
        name="final_norm",
    )(x, g.reshape(1, D_MODEL))


def kernel(x_prompt, x_sample, cache_k, cache_v, state_conv, state_ffn_conv, norm1_g, w_in, conv_w, conv_b,
           conv_ln_g, conv_ln_b, w_conv_out, attn_sinks, w_attn_out, w_out, norm2_g, w_up, ffn_conv_w,
           ffn_conv_b, w_down, final_norm_g):
    depth = w_in.shape[0]
    n_p, t_p, _ = x_prompt.shape
    n_s, t_s, _ = x_sample.shape
    xp = x_prompt.reshape(n_p * t_p, D_MODEL)
    xs = x_sample.reshape(n_s * t_s, D_MODEL)
    w_co_b, w_ao_b, w_o_b = (w.astype(BF16) for w in (w_conv_out, w_attn_out, w_out))
    keys_on_lanes = lambda c: jnp.transpose(c, (0, 1, 3, 4, 2)).reshape(depth, n_s, D_KV, WINDOW)
    keys_on_rows = lambda c: jnp.transpose(c.reshape(depth, n_s, N_KV_HEADS, HEAD_DIM, WINDOW), (0, 1, 4, 2, 3))
    cache_k, cache_v = keys_on_lanes(cache_k), keys_on_lanes(cache_v)
    rows_first = lambda s: jnp.transpose(s, (0, 2, 1, 3))
    state_conv = rows_first(state_conv)
    outs = {k: [] for k in ("kp", "vp", "cp", "fp")}
    c_states, kv_states, f_states = (), (), ()
    for l in range(depth):
        conv_args = (conv_w, conv_b, conv_ln_g, conv_ln_b, l)
        proj_s, w_in16 = _inproj(xs, norm1_g, w_in, l, tm=n_s * t_s)
        proj = _inproj(xp, norm1_g, w_in16, l, tm=1024)
        c_act, c_state = _conv_prompt(proj, n_p, t_p, *conv_args)
        a_act = _attn_prompt(proj, attn_sinks, n_p, t_p, l)
        x1_p, h2_p = _mix(c_act, a_act, proj, xp, w_co_b, w_ao_b, w_o_b, norm2_g, l)
        kv = proj.reshape(n_p, t_p, IN_COLS)[:, t_p - WINDOW:, :]
        outs["kp"].append(kv[:, :, COL_K:COL_K + D_KV].astype(F32).reshape(n_p, WINDOW, N_KV_HEADS, HEAD_DIM))
        outs["vp"].append(kv[:, :, COL_V:COL_V + D_KV].astype(F32).reshape(n_p, WINDOW, N_KV_HEADS, HEAD_DIM))
        outs["cp"].append(c_state)
        proj = proj_s
        c_stack, c_act = _conv_sample(proj, state_conv, c_states, n_s, t_s, *conv_args)
        k_stack, v_stack, a_act = _attn_sample(proj, cache_k, cache_v, kv_states, attn_sinks, n_s, t_s, l)
        c_states, kv_states = (c_stack,), (k_stack, v_stack)
        x1, h2 = _mix(c_act, a_act, proj, xs, w_co_b, w_ao_b, w_o_b, norm2_g, l)
        f_stack, act, wg16, wv16 = _up_sample(h2, state_ffn_conv, f_states, n_s, t_s, w_up, ffn_conv_w,
                                              ffn_conv_b, l)
        f_states = (f_stack,)
        xs, w_down16 = _down(act, w_down, x1, l, tm=512)
        act, f_g, f_v = _up_prompt(h2_p, n_p, t_p, wg16, wv16, ffn_conv_w, ffn_conv_b, l)
        xp = _down(act, w_down16, x1_p, l, tm=1024)
        outs["fp"].append(jnp.concatenate([f_g, f_v], axis=-1))
    y_prompt = _final_norm(xp, final_norm_g).reshape(n_p, t_p, D_MODEL)
    y_sample = _final_norm(xs, final_norm_g).reshape(n_s, t_s, D_MODEL)
    return (y_prompt, y_sample, jnp.stack(outs["kp"]), jnp.stack(outs["vp"]), jnp.stack(outs["cp"]),
            jnp.stack(outs["fp"]), keys_on_rows(k_stack), keys_on_rows(v_stack), rows_first(c_stack),
            jnp.transpose(f_stack, (0, 1, 3, 2, 4)).reshape(depth, n_s, FFN_K - 1, 2 * D_FF))
```

```python
import functools

import jax
import jax.numpy as jnp
from jax import lax
from jax.experimental import pallas as pl
from jax.experimental.pallas import tpu as pltpu

F32 = jnp.float32
BF16 = jnp.bfloat16

D_MODEL = 2048
HEAD_DIM = 64
N_HEADS = 16
N_KV_HEADS = 4
GROUP = N_HEADS // N_KV_HEADS
D_ATTN = N_HEADS * HEAD_DIM
D_KV = N_KV_HEADS * HEAD_DIM
WINDOW = 128
D_CONV = D_MODEL // 2
CONV_K = 31
D_FF = 3 * D_MODEL
FFN_K = 3
EPS = 1e-6
IN_COLS = 2 * D_CONV + D_ATTN + 2 * D_KV + 2 * D_MODEL
NEG = -1e30
SCALE = HEAD_DIM ** -0.5
SLOPES = tuple(2.0 ** (-8.0 * (h + 1) / N_HEADS) for h in range(N_HEADS))

V7X_SUBLANES = 8
V7X_SCOPED_VMEM_CAP_BYTES = 60000 * 1024

IN_TILE = 1280
COL_GLU_A = 0
COL_GLU_G = D_CONV
COL_Q = 2 * D_CONV
COL_K = COL_Q + D_ATTN
COL_V = COL_K + D_KV
COL_GATE_C = COL_V + D_KV
COL_GATE_A = COL_GATE_C + D_MODEL
GATE_BLOCK = 512

CONV_HALO = 32
CONV_ROW_CHUNK = 64
CONV_LANE_CHUNK = 256
FFN_HALO = V7X_SUBLANES
FFN_TILE = 512
FFN_PROMPT_TILE = 512
DOWN_TILE = 512


def _nbytes(shape, dtype):
    n = 1
    for s in shape:
        n *= s
    return n * jnp.dtype(dtype).itemsize


def _params(semantics, pipelined, resident=(), temps=(), keep_operands_in_hbm=True):
    est = 2 * sum(_nbytes(s, d) for s, d in pipelined)
    est += sum(_nbytes(s, d) for s, d in resident)
    est += sum(_nbytes(s, d) for s, d in temps)
    limit = min(V7X_SCOPED_VMEM_CAP_BYTES, est + est // 4)
    if keep_operands_in_hbm:
        limit = V7X_SCOPED_VMEM_CAP_BYTES
    return pltpu.CompilerParams(dimension_semantics=semantics, vmem_limit_bytes=limit)


def _layer_vec(l, width, rows=1):
    return pl.BlockSpec((None, rows, width), lambda *_: (l, 0, 0))


def _as_rows(p):
    return p.reshape(p.shape[0], 1, p.shape[1])


def _rms(x, g):
    return x * lax.rsqrt(jnp.mean(x * x, axis=-1, keepdims=True) + EPS) * g


def _dot(a, b):
    return jnp.dot(a, b, preferred_element_type=F32)


def _skip_first_ref(body):
    def wrapped(_, *refs):
        body(*refs)
    return wrapped


def _stacked_call(body, l, prev, n_alias_out, **kw):
    if l == 0:
        return pl.pallas_call(body, **kw)
    for _ in prev:
        body = _skip_first_ref(body)
    kw["in_specs"] = [pl.BlockSpec(memory_space=pl.ANY)] * len(prev) + list(kw["in_specs"])
    call = pl.pallas_call(body, input_output_aliases={i: i for i in range(n_alias_out)}, **kw)
    return lambda *args: call(*prev, *args)


def _inproj_body(x_ref, g_ref, w_ref, o_ref, *refs, cast):
    h_ref = refs[-1]

    @pl.when(pl.program_id(1) == 0)
    def _():
        h_ref[...] = _rms(x_ref[...], g_ref[...]).astype(BF16)

    if cast:
        w16_ref = refs[0]
        w16_ref[...] = w_ref[...].astype(BF16)
        w_ref = w16_ref
    o_ref[...] = _dot(h_ref[...], w_ref[...]).astype(BF16)


def _inproj(x, g, w, l, tm):
    m = x.shape[0]
    cast = w.dtype == F32
    assert not cast or m == tm
    w_spec = (pl.BlockSpec((None, D_MODEL, IN_TILE), lambda i, j: (l, 0, j)) if cast
              else pl.BlockSpec((D_MODEL, IN_TILE), lambda i, j: (0, j)))
    out_specs = [pl.BlockSpec((tm, IN_TILE), lambda i, j: (i, j))]
    out_shape = [jax.ShapeDtypeStruct((m, IN_COLS), BF16)]
    blocks = [((tm, D_MODEL), F32), ((D_MODEL, IN_TILE), w.dtype), ((tm, IN_TILE), BF16)]
    if cast:
        out_specs.append(pl.BlockSpec((D_MODEL, IN_TILE), lambda i, j: (0, j)))
        out_shape.append(jax.ShapeDtypeStruct((D_MODEL, IN_COLS), BF16))
        blocks.append(((D_MODEL, IN_TILE), BF16))
    outs = pl.pallas_call(
        functools.partial(_inproj_body, cast=cast),
        grid=(m // tm, IN_COLS // IN_TILE),
        in_specs=[pl.BlockSpec((tm, D_MODEL), lambda i, j: (i, 0)), _layer_vec(l, D_MODEL), w_spec],
        out_specs=out_specs,
        out_shape=out_shape,
        scratch_shapes=[pltpu.VMEM((tm, D_MODEL), BF16)],
        compiler_params=_params(("parallel", "arbitrary"), blocks, resident=[((tm, D_MODEL), BF16)],
                                temps=[((tm, D_MODEL), F32)]),
        name="inproj",
    )(x, _as_rows(g), w)
    return outs if cast else outs[0]


def _ln_silu(c, g, b):
    mu = jnp.mean(c, axis=-1, keepdims=True)
    xc = c - mu
    y = xc * lax.rsqrt(jnp.mean(xc * xc, axis=-1, keepdims=True) + EPS) * g + b
    return y * jax.nn.sigmoid(y)


def _conv_prompt_body(a_ref, g_ref, w_ref, b_ref, lng_ref, lnb_ref, act_ref, st_ref, xx_ref, cv_ref, sh_ref,
                      *, tt):
    i = pl.program_id(1)

    @pl.when(i == 0)
    def _():
        xx_ref[0:CONV_HALO, :] = jnp.zeros((CONV_HALO, D_CONV), F32)

    xx_ref[CONV_HALO:CONV_HALO + tt, :] = a_ref[...].astype(F32) * jax.nn.sigmoid(g_ref[...].astype(F32))

    first = CONV_HALO - (CONV_K - 1)
    sh_rows = sh_ref.shape[1]
    for lc in range(D_CONV // CONV_LANE_CHUNK):
        ls = pl.ds(lc * CONV_LANE_CHUNK, CONV_LANE_CHUNK)
        for r in range(1, V7X_SUBLANES):
            sh_ref[r - 1] = xx_ref[r:r + sh_rows, ls]
        for r0 in range(0, tt, CONV_ROW_CHUNK):
            acc = jnp.broadcast_to(b_ref[:, ls], (CONV_ROW_CHUNK, CONV_LANE_CHUNK))
            for j in range(CONV_K):
                q, r = divmod(first + j, V7X_SUBLANES)
                a0 = r0 + V7X_SUBLANES * q
                if r == 0:
                    window = xx_ref[a0:a0 + CONV_ROW_CHUNK, ls]
                else:
                    window = sh_ref[r - 1, a0:a0 + CONV_ROW_CHUNK, :]
                acc = acc + w_ref[j:j + 1, ls] * window
            cv_ref[r0:r0 + CONV_ROW_CHUNK, ls] = acc
    act_ref[...] = _ln_silu(cv_ref[...], lng_ref[...], lnb_ref[...]).astype(BF16)

    @pl.when(i == pl.num_programs(1) - 1)
    def _():
        st_ref[0] = xx_ref[tt + first:tt + CONV_HALO, :]

    xx_ref[0:CONV_HALO, :] = xx_ref[tt:tt + CONV_HALO, :]


def _conv_prompt(proj, n, t, w, b, lng, lnb, l, tt=512):
    tps = t // tt
    vec = _layer_vec(l, D_CONV)
    blocks = [((tt, D_CONV), BF16)] * 3
    shifted = (V7X_SUBLANES - 1, tt + CONV_HALO - V7X_SUBLANES, CONV_LANE_CHUNK)
    return pl.pallas_call(
        functools.partial(_conv_prompt_body, tt=tt),
        grid=(n, tps),
        in_specs=[pl.BlockSpec((tt, D_CONV), lambda s, i: (s * tps + i, COL_GLU_A // D_CONV)),
                  pl.BlockSpec((tt, D_CONV), lambda s, i: (s * tps + i, COL_GLU_G // D_CONV)),
                  _layer_vec(l, D_CONV, CONV_K), vec, vec, vec],
        out_specs=[pl.BlockSpec((tt, D_CONV), lambda s, i: (s * tps + i, 0)),
                   pl.BlockSpec((1, CONV_K - 1, D_CONV), lambda s, i: (s, 0, 0))],
        out_shape=[jax.ShapeDtypeStruct((n * t, D_CONV), BF16),
                   jax.ShapeDtypeStruct((n, CONV_K - 1, D_CONV), F32)],
        scratch_shapes=[pltpu.VMEM((CONV_HALO + tt, D_CONV), F32), pltpu.VMEM((tt, D_CONV), F32),
                        pltpu.VMEM(shifted, F32)],
        compiler_params=_params(("parallel", "arbitrary"), blocks,
                                resident=[((CONV_HALO + tt, D_CONV), F32), ((tt, D_CONV), F32), (shifted, F32)],
                                temps=[((tt, D_CONV), F32)] * 4),
        name="conv_prompt",
    )(proj, proj, w, _as_rows(b), _as_rows(lng), _as_rows(lnb))


def _conv_sample_body(a_ref, g_ref, s_ref, w_ref, b_ref, lng_ref, lnb_ref, st_ref, act_ref, new_ref, y_ref,
                      *, nb, t):
    hist = CONV_K - 1
    u = (a_ref[...].astype(F32) * jax.nn.sigmoid(g_ref[...].astype(F32))).reshape(nb, t, D_CONV)
    for j in range(t):
        new_ref[j] = u[:, j, :]
    for r in range(hist):
        st_ref[r] = s_ref[r + t] if r + t < hist else new_ref[r + t - hist]
    for j in range(t):
        for lc in range(D_CONV // CONV_LANE_CHUNK):
            ls = pl.ds(lc * CONV_LANE_CHUNK, CONV_LANE_CHUNK)
            acc = jnp.broadcast_to(b_ref[:, ls], (nb, CONV_LANE_CHUNK))
            for k in range(CONV_K):
                r = j + k
                plane = s_ref[r, :, ls] if r < hist else new_ref[r - hist, :, ls]
                acc = acc + w_ref[k:k + 1, ls] * plane
            y_ref[j, :, ls] = acc
    y = _ln_silu(y_ref[...], lng_ref[...], lnb_ref[...])
    act_ref[...] = jnp.stack([y[j] for j in range(t)], axis=1).reshape(nb * t, D_CONV).astype(BF16)


def _conv_sample(proj, state, prev, n, t, w, b, lng, lnb, l, nb=32):
    depth = state.shape[0]
    hist = CONV_K - 1
    rows = nb * t
    vec = _layer_vec(l, D_CONV)
    st = pl.BlockSpec((None, hist, nb, D_CONV), lambda s: (l, 0, s, 0))
    blocks = [((rows, D_CONV), BF16)] * 3 + [((hist, nb, D_CONV), F32)] * 2
    scratch = [((t, nb, D_CONV), F32)] * 2
    return _stacked_call(
        functools.partial(_conv_sample_body, nb=nb, t=t), l, prev, 1,
        grid=(n // nb,),
        in_specs=[pl.BlockSpec((rows, D_CONV), lambda s: (s, COL_GLU_A // D_CONV)),
                  pl.BlockSpec((rows, D_CONV), lambda s: (s, COL_GLU_G // D_CONV)),
                  st, _layer_vec(l, D_CONV, CONV_K), vec, vec, vec],
        out_specs=[st, pl.BlockSpec((rows, D_CONV), lambda s: (s, 0))],
        out_shape=[jax.ShapeDtypeStruct((depth, hist, n, D_CONV), F32),
                   jax.ShapeDtypeStruct((n * t, D_CONV), BF16)],
        scratch_shapes=[pltpu.VMEM(s, d) for s, d in scratch],
        compiler_params=_params(("parallel",), blocks, resident=scratch,
                                temps=[((rows, D_CONV), F32)] * 4, keep_operands_in_hbm=True),
        name="conv_sample",
    )(proj, proj, state, w, _as_rows(b), _as_rows(lng), _as_rows(lnb))


def _softmax_probs(s, valid, distf, slope, sink):
    s = jnp.where(valid, s - slope * distf, NEG)
    m = jnp.maximum(jnp.max(s, axis=-1, keepdims=True), sink)
    return jnp.exp(s - m), jnp.exp(sink - m)


def _attn_prompt_body(sink_ref, q_ref, kc_ref, vc_ref, kp_ref, vp_ref, o_ref, *, l, nsub):
    i = pl.program_id(1)
    blk = WINDOW
    q_all = q_ref[...] * SCALE
    k_all = jnp.concatenate([kp_ref[...], kc_ref[...]], axis=0)
    v_all = jnp.concatenate([vp_ref[...], vc_ref[...]], axis=0)
    r = lax.broadcasted_iota(jnp.int32, (blk, 2 * blk), 0)
    c = lax.broadcasted_iota(jnp.int32, (blk, 2 * blk), 1)
    dist = blk + r - c
    in_window = (dist >= 0) & (dist <= WINDOW)
    distf = dist.astype(F32)
    for sb in range(nsub):
        q = q_all[sb * blk:(sb + 1) * blk]
        k = k_all[sb * blk:(sb + 2) * blk]
        v = v_all[sb * blk:(sb + 2) * blk]
        valid = in_window & ((c >= blk) | (i > 0)) if sb == 0 else in_window
        scores = []
        for kv in range(N_KV_HEADS):
            kh = k[:, kv * HEAD_DIM:(kv + 1) * HEAD_DIM]
            heads = [kv * GROUP + g for g in range(GROUP)]
            qs = jnp.concatenate([q[:, h * HEAD_DIM:(h + 1) * HEAD_DIM] for h in heads], axis=0)
            scores.append(lax.dot_general(qs, kh, (((1,), (1,)), ((), ())),
                                          preferred_element_type=F32))
        probs, sink_terms = [], []
        for kv in range(N_KV_HEADS):
            ps = []
            for g in range(GROUP):
                h = kv * GROUP + g
                p, e = _softmax_probs(scores[kv][g * blk:(g + 1) * blk], valid, distf, SLOPES[h], sink_ref[l, h])
                ps.append(p.astype(BF16))
                sink_terms.append(e)
            probs.append(jnp.concatenate(ps, axis=0))
        lane = lax.broadcasted_iota(jnp.int32, (2 * blk, 2 * HEAD_DIM), 1)
        outs = []
        for kv in range(N_KV_HEADS):
            pair = v[:, (kv // 2) * 2 * HEAD_DIM:(kv // 2 + 1) * 2 * HEAD_DIM]
            own = (lane < HEAD_DIM) if kv % 2 == 0 else (lane >= HEAD_DIM)
            outs.append(_dot(probs[kv], jnp.where(own, pair, jnp.ones_like(pair))))
        for h in range(N_HEADS):
            kv, g = divmod(h, GROUP)
            rows = outs[kv][g * blk:(g + 1) * blk]
            scaled = rows * pltpu.roll(1.0 / (rows + sink_terms[h]), HEAD_DIM, axis=1)
            o = scaled[:, :HEAD_DIM] if kv % 2 == 0 else scaled[:, HEAD_DIM:]
            o_ref[sb * blk:(sb + 1) * blk, h * HEAD_DIM:(h + 1) * HEAD_DIM] = o.astype(BF16)


def _attn_prompt(proj, sinks, n, t, l, nsub=1):
    blk = WINDOW
    rows = nsub * blk
    steps = t // rows
    kcol, vcol = COL_K // D_KV, COL_V // D_KV
    prev = lambda s, i: s * (t // blk) + jnp.maximum(nsub * i - 1, 0)
    blocks = ([((rows, D_ATTN), BF16)] * 2 + [((rows, D_KV), BF16)] * 2 + [((blk, D_KV), BF16)] * 2)
    return pl.pallas_call(
        functools.partial(_attn_prompt_body, l=l, nsub=nsub),
        grid=(n, steps),
        in_specs=[pl.BlockSpec(memory_space=pltpu.SMEM),
                  pl.BlockSpec((rows, D_ATTN), lambda s, i: (s * steps + i, COL_Q // D_ATTN)),
                  pl.BlockSpec((rows, D_KV), lambda s, i: (s * steps + i, kcol)),
                  pl.BlockSpec((rows, D_KV), lambda s, i: (s * steps + i, vcol)),
                  pl.BlockSpec((blk, D_KV), lambda s, i: (prev(s, i), kcol)),
                  pl.BlockSpec((blk, D_KV), lambda s, i: (prev(s, i), vcol))],
        out_specs=pl.BlockSpec((rows, D_ATTN), lambda s, i: (s * steps + i, 0)),
        out_shape=jax.ShapeDtypeStruct((n * t, D_ATTN), BF16),
        compiler_params=_params(("parallel", "parallel"), blocks,
                                temps=[((GROUP * blk, 2 * blk), F32)] * 16),
        name="attn_prompt",
    )(sinks, proj, proj, proj, proj, proj)


def _attn_sample_body(sink_ref, q_ref, kn_ref, vn_ref, ck_ref, cv_ref, nk_ref, nv_ref, o_ref, *, nb, t, l):
    eye = (lax.broadcasted_iota(jnp.int32, (D_KV, D_KV), 0)
           == lax.broadcasted_iota(jnp.int32, (D_KV, D_KV), 1)).astype(BF16)
    lane = lax.broadcasted_iota(jnp.int32, (D_KV, WINDOW), 1)
    for new_ref, cache_ref, out_ref in ((kn_ref, ck_ref, nk_ref), (vn_ref, cv_ref, nv_ref)):
        tr = lax.dot_general(eye, new_ref[...], (((1,), (1,)), ((), ())), preferred_element_type=F32)
        for s in range(nb):
            kept = pltpu.roll(cache_ref[s], WINDOW - t, axis=1)
            fresh = pltpu.roll(tr, (WINDOW - t - s * t) % WINDOW, axis=1)
            out_ref[s] = jnp.where(lane >= WINDOW - t, fresh, kept)
    q3 = (q_ref[...].astype(F32) * SCALE).reshape(nb, t, D_ATTN)
    rows = GROUP * t
    tq = lax.broadcasted_iota(jnp.int32, (rows, WINDOW), 0) % t
    w = lax.broadcasted_iota(jnp.int32, (rows, WINDOW), 1)
    dist_old = WINDOW + tq - w
    valid_old = (w < t) & (w >= tq)
    dist_new = WINDOW - t + tq - w
    valid_new = dist_new >= 0
    gi = lax.broadcasted_iota(jnp.int32, (rows, 1), 0) // t
    for kv in range(N_KV_HEADS):
        heads = [kv * GROUP + g for g in range(GROUP)]
        slope = jnp.zeros((rows, 1), F32)
        sink = jnp.zeros((rows, 1), F32)
        for g, h in enumerate(heads):
            slope = jnp.where(gi == g, SLOPES[h], slope)
            sink = jnp.where(gi == g, sink_ref[l, h], sink)
        hs = slice(kv * HEAD_DIM, (kv + 1) * HEAD_DIM)
        qs = jnp.concatenate([q3[:, :, h * HEAD_DIM:(h + 1) * HEAD_DIM] for h in heads], axis=1).astype(BF16)
        s_old = jnp.einsum("bqd,bdk->bqk", qs, ck_ref[:, hs, :].astype(BF16), preferred_element_type=F32)
        s_new = jnp.einsum("bqd,bdk->bqk", qs, nk_ref[:, hs, :].astype(BF16), preferred_element_type=F32)
        s_old = jnp.where(valid_old, s_old - slope * dist_old.astype(F32), NEG)
        s_new = jnp.where(valid_new, s_new - slope * dist_new.astype(F32), NEG)
        m = jnp.maximum(jnp.maximum(jnp.max(s_old, axis=-1, keepdims=True),
                                    jnp.max(s_new, axis=-1, keepdims=True)), sink)
        p_old = jnp.exp(s_old - m)
        p_new = jnp.exp(s_new - m)
        denom = (jnp.sum(p_old, axis=-1, keepdims=True) + jnp.sum(p_new, axis=-1, keepdims=True)
                 + jnp.exp(sink - m))
        o = (jnp.einsum("bqk,bdk->bqd", p_old.astype(BF16), cv_ref[:, hs, :].astype(BF16),
                        preferred_element_type=F32)
             + jnp.einsum("bqk,bdk->bqd", p_new.astype(BF16), nv_ref[:, hs, :].astype(BF16),
                          preferred_element_type=F32)) / denom
        for g, h in enumerate(heads):
            o_ref[:, h * HEAD_DIM:(h + 1) * HEAD_DIM] = (
                o[:, g * t:(g + 1) * t, :].reshape(nb * t, HEAD_DIM).astype(BF16))


def _attn_sample(proj, cache_k, cache_v, prev, sinks, n, t, l):
    depth = cache_k.shape[0]
    nb = WINDOW // t
    rows = nb * t
    kcol, vcol = COL_K // D_KV, COL_V // D_KV
    cache = pl.BlockSpec((None, nb, D_KV, WINDOW), lambda s: (l, s, 0, 0))
    stacked = jax.ShapeDtypeStruct((depth, n, D_KV, WINDOW), F32)
    blocks = [((rows, D_ATTN), BF16)] * 2 + [((rows, D_KV), BF16)] * 2 + [((nb, D_KV, WINDOW), F32)] * 4
    return _stacked_call(
        functools.partial(_attn_sample_body, nb=nb, t=t, l=l), l, prev, 2,
        grid=(n // nb,),
        in_specs=[pl.BlockSpec(memory_space=pltpu.SMEM),
                  pl.BlockSpec((rows, D_ATTN), lambda s: (s, COL_Q // D_ATTN)),
                  pl.BlockSpec((rows, D_KV), lambda s: (s, kcol)),
                  pl.BlockSpec((rows, D_KV), lambda s: (s, vcol)),
                  cache, cache],
        out_specs=[cache, cache, pl.BlockSpec((rows, D_ATTN), lambda s: (s, 0))],
        out_shape=[stacked, stacked, jax.ShapeDtypeStruct((n * t, D_ATTN), BF16)],
        compiler_params=_params(("parallel",), blocks,
                                temps=[((nb, D_KV, WINDOW), F32)] * 4 + [((nb, GROUP * t, WINDOW), F32)] * 8,
                                keep_operands_in_hbm=True),
        name="attn_sample",
    )(sinks, proj, proj, proj, cache_k, cache_v)


def _mix_body(c_ref, a_ref, *refs):
    n_gate = D_MODEL // GATE_BLOCK
    gc_refs, ga_refs = refs[:n_gate], refs[n_gate:2 * n_gate]
    x_ref, wco_ref, wao_ref, wo_ref, n2_ref, x1_ref, h2_ref = refs[2 * n_gate:]
    gate_c = jnp.concatenate([g[...] for g in gc_refs], axis=1).astype(F32)
    gate_a = jnp.concatenate([g[...] for g in ga_refs], axis=1).astype(F32)
    branch_c = _dot(c_ref[...], wco_ref[...])
    branch_a = _dot(a_ref[...], wao_ref[...])
    merged = jax.nn.sigmoid(gate_c) * branch_c + jax.nn.sigmoid(gate_a) * branch_a
    x1 = x_ref[...] + _dot(merged.astype(BF16), wo_ref[...])
    x1_ref[...] = x1
    h2_ref[...] = _rms(x1, n2_ref[...]).astype(BF16)


def _mix(c_act, a_act, proj, x, wco, wao, wo, n2, l, tm=512):
    m = x.shape[0]
    row = lambda width, col: pl.BlockSpec((tm, width), lambda i: (i, col))
    const = lambda k: pl.BlockSpec((None, k, D_MODEL), lambda i: (l, 0, 0), pipeline_mode=pl.Buffered(1))
    blocks = ([((tm, D_CONV), BF16), ((tm, D_ATTN), BF16)] + [((tm, D_MODEL), F32)] * 2 + [((tm, D_MODEL), BF16)] * 3)
    weights = [((D_CONV, D_MODEL), BF16), ((D_ATTN, D_MODEL), BF16), ((D_MODEL, D_MODEL), BF16)]
    return pl.pallas_call(
        _mix_body,
        grid=(m // tm,),
        in_specs=[row(D_CONV, 0), row(D_ATTN, 0)]
                 + [row(GATE_BLOCK, COL_GATE_C // GATE_BLOCK + b) for b in range(D_MODEL // GATE_BLOCK)]
                 + [row(GATE_BLOCK, COL_GATE_A // GATE_BLOCK + b) for b in range(D_MODEL // GATE_BLOCK)]
                 + [row(D_MODEL, 0),
                  const(D_CONV), const(D_ATTN), const(D_MODEL),
                  _layer_vec(l, D_MODEL)],
        out_specs=[row(D_MODEL, 0), row(D_MODEL, 0)],
        out_shape=[jax.ShapeDtypeStruct((m, D_MODEL), F32), jax.ShapeDtypeStruct((m, D_MODEL), BF16)],
        compiler_params=_params(("parallel",), blocks, resident=weights, temps=[((tm, D_MODEL), F32)] * 4),
        name="mix",
    )(c_act, a_act, *([proj] * (2 * (D_MODEL // GATE_BLOCK))), x, wco, wao, wo, _as_rows(n2))


def _ffn_conv(e_ref, w_ref, b_ref, tm):
    first = FFN_HALO - (FFN_K - 1)
    out = b_ref[...]
    for j in range(FFN_K):
        out = out + w_ref[j:j + 1, :] * e_ref[first + j:first + j + tm, :]
    return out


def _up_prompt_body(h_ref, wg_ref, wv_ref, cwg_ref, cwv_ref, cbg_ref, cbv_ref, act_ref, sg_ref, sv_ref,
                    eg_ref, ev_ref, *, tm, tiles_per_seq):
    pos = pl.program_id(1) % tiles_per_seq

    @pl.when(pos == 0)
    def _():
        eg_ref[0:FFN_HALO, :] = jnp.zeros((FFN_HALO, eg_ref.shape[1]), F32)
        ev_ref[0:FFN_HALO, :] = jnp.zeros((FFN_HALO, ev_ref.shape[1]), F32)

    h = h_ref[...]
    eg_ref[FFN_HALO:FFN_HALO + tm, :] = _dot(h, wg_ref[...])
    ev_ref[FFN_HALO:FFN_HALO + tm, :] = _dot(h, wv_ref[...])
    gate = _ffn_conv(eg_ref, cwg_ref, cbg_ref, tm)
    val = _ffn_conv(ev_ref, cwv_ref, cbv_ref, tm)
    act_ref[...] = (gate * jax.nn.sigmoid(gate) * val).astype(BF16)

    @pl.when(pos == tiles_per_seq - 1)
    def _():
        hist = FFN_K - 1
        sg_ref[0] = eg_ref[FFN_HALO + tm - hist:FFN_HALO + tm, :]
        sv_ref[0] = ev_ref[FFN_HALO + tm - hist:FFN_HALO + tm, :]

    eg_ref[0:FFN_HALO, :] = eg_ref[tm:tm + FFN_HALO, :]
    ev_ref[0:FFN_HALO, :] = ev_ref[tm:tm + FFN_HALO, :]


def _up_prompt(h2, n, t, wg16, wv16, cw, cb, l, tm=1024, tile=FFN_PROMPT_TILE):
    m = n * t
    tps = t // tm
    nt = D_FF // tile
    hist = FFN_K - 1
    col = lambda rows, off: pl.BlockSpec((None, rows, tile), lambda j, i: (l, 0, j + off))
    weight = pl.BlockSpec((D_MODEL, tile), lambda j, i: (0, j))
    blocks = ([((tm, D_MODEL), BF16)] + [((D_MODEL, tile), BF16)] * 2 + [((tm, tile), BF16)])
    scratch = [((FFN_HALO + tm, tile), F32)] * 2
    state = pl.BlockSpec((1, hist, tile), lambda j, i: (i // tps, 0, j))
    return pl.pallas_call(
        functools.partial(_up_prompt_body, tm=tm, tiles_per_seq=tps),
        grid=(nt, m // tm),
        in_specs=[pl.BlockSpec((tm, D_MODEL), lambda j, i: (i, 0)),
                  weight, weight, col(FFN_K, 0), col(FFN_K, nt), col(1, 0), col(1, nt)],
        out_specs=[pl.BlockSpec((tm, tile), lambda j, i: (i, j)), state, state],
        out_shape=[jax.ShapeDtypeStruct((m, D_FF), BF16),
                   jax.ShapeDtypeStruct((n, hist, D_FF), F32), jax.ShapeDtypeStruct((n, hist, D_FF), F32)],
        scratch_shapes=[pltpu.VMEM(s, d) for s, d in scratch],
        compiler_params=_params(("parallel", "arbitrary"), blocks, resident=scratch,
                                temps=[((tm, tile), F32)] * 4),
        name="up_prompt",
    )(h2, wg16, wv16, cw, cw, _as_rows(cb), _as_rows(cb))


def _ffn_conv_sample(u3, st_ref, w_ref, b_ref, tpos):
    st0 = st_ref[:, 0:1, :]
    st1 = st_ref[:, 1:2, :]
    prev1 = jnp.where(tpos == 0, st1, pltpu.roll(u3, 1, axis=1))
    prev2 = jnp.where(tpos == 0, st0, jnp.where(tpos == 1, st1, pltpu.roll(u3, 2, axis=1)))
    return b_ref[...] + w_ref[0:1, :] * prev2 + w_ref[1:2, :] * prev1 + w_ref[2:3, :] * u3


def _up_sample_body(h_ref, wg_ref, wv_ref, cwg_ref, cwv_ref, cbg_ref, cbv_ref, stg_ref, stv_ref,
                    s_ref, act_ref, wg16_ref, wv16_ref, *, n, t):
    hist = FFN_K - 1
    h = h_ref[...]
    wg16_ref[...] = wg_ref[...].astype(BF16)
    wv16_ref[...] = wv_ref[...].astype(BF16)
    ug = _dot(h, wg16_ref[...]).reshape(n, t, FFN_TILE)
    uv = _dot(h, wv16_ref[...]).reshape(n, t, FFN_TILE)
    tpos = lax.broadcasted_iota(jnp.int32, (1, t, 1), 1)
    gate = _ffn_conv_sample(ug, stg_ref, cwg_ref, cbg_ref, tpos)
    val = _ffn_conv_sample(uv, stv_ref, cwv_ref, cbv_ref, tpos)
    act_ref[...] = (gate * jax.nn.sigmoid(gate) * val).reshape(n * t, FFN_TILE).astype(BF16)
    s_ref[:, 0] = ug[:, t - hist:t, :]
    s_ref[:, 1] = uv[:, t - hist:t, :]


def _up_sample(h2, state, prev, n, t, w_up, cw, cb, l):
    assert t == V7X_SUBLANES and FFN_K == 3
    depth = state.shape[0]
    m = n * t
    nt = D_FF // FFN_TILE
    hist = FFN_K - 1
    col = lambda rows, off: pl.BlockSpec((None, rows, FFN_TILE), lambda j: (l, 0, j + off))
    st = lambda off: pl.BlockSpec((None, n, hist, FFN_TILE), lambda j: (l, 0, 0, j + off))
    new_state = pl.BlockSpec((None, n, 2, hist, FFN_TILE), lambda j: (l, 0, 0, 0, j))
    w16 = pl.BlockSpec((D_MODEL, FFN_TILE), lambda j: (0, j))
    w16_shape = jax.ShapeDtypeStruct((D_MODEL, D_FF), BF16)
    blocks = ([((D_MODEL, FFN_TILE), F32)] * 2 + [((n, V7X_SUBLANES, FFN_TILE), F32)] * 4
              + [((m, FFN_TILE), BF16)] + [((D_MODEL, FFN_TILE), BF16)] * 2)
    return _stacked_call(
        functools.partial(_up_sample_body, n=n, t=t), l, prev, 1,
        grid=(nt,),
        in_specs=[pl.BlockSpec((m, D_MODEL), lambda j: (0, 0)),
                  col(D_MODEL, 0), col(D_MODEL, nt), col(FFN_K, 0), col(FFN_K, nt), col(1, 0), col(1, nt),
                  st(0), st(nt)],
        out_specs=[new_state, pl.BlockSpec((m, FFN_TILE), lambda j: (0, j)), w16, w16],
        out_shape=[jax.ShapeDtypeStruct((depth, n, 2, hist, D_FF), F32), jax.ShapeDtypeStruct((m, D_FF), BF16),
                   w16_shape, w16_shape],
        compiler_params=_params(("parallel",), blocks,
                                resident=[((m, D_MODEL), BF16)] * 2,
                                temps=[((m, FFN_TILE), F32)] * 8),
        name="up_sample",
    )(h2, w_up, w_up, cw, cw, _as_rows(cb), _as_rows(cb), state, state)


def _down_body(act_ref, w_ref, x_ref, o_ref, *refs, cast):
    if cast:
        w16_ref, = refs

        @pl.when(pl.program_id(1) == 0)
        def _():
            w16_ref[...] = w_ref[...].astype(BF16)

        w_ref = w16_ref
    o_ref[...] = x_ref[...] + _dot(act_ref[...], w_ref[...])


def _down(act, w, x1, l, tm):
    m = x1.shape[0]
    cast = w.dtype == F32
    w_spec = (pl.BlockSpec((None, D_FF, DOWN_TILE), lambda j, i: (l, 0, j)) if cast
              else pl.BlockSpec((D_FF, DOWN_TILE), lambda j, i: (0, j)))
    out_specs = [pl.BlockSpec((tm, DOWN_TILE), lambda j, i: (i, j))]
    out_shape = [jax.ShapeDtypeStruct((m, D_MODEL), F32)]
    blocks = [((tm, D_FF), BF16), ((D_FF, DOWN_TILE), w.dtype), ((tm, DOWN_TILE), F32), ((tm, DOWN_TILE), F32)]
    if cast:
        out_specs.append(pl.BlockSpec((D_FF, DOWN_TILE), lambda j, i: (0, j)))
        out_shape.append(jax.ShapeDtypeStruct((D_FF, D_MODEL), BF16))
        blocks.append(((D_FF, DOWN_TILE), BF16))
    outs = pl.pallas_call(
        functools.partial(_down_body, cast=cast),
        grid=(D_MODEL // DOWN_TILE, m // tm),
        in_specs=[pl.BlockSpec((tm, D_FF), lambda j, i: (i, 0)), w_spec,
                  pl.BlockSpec((tm, DOWN_TILE), lambda j, i: (i, j))],
        out_specs=out_specs,
        out_shape=out_shape,
        compiler_params=_params(("parallel", "arbitrary"), blocks, temps=[((tm, DOWN_TILE), F32)]),
        name="down",
    )(act, w, x1)
    return outs if cast else outs[0]


def _final_norm_body(x_ref, g_ref, o_ref):
    o_ref[...] = _rms(x_ref[...], g_ref[...])


def _final_norm(x, g, tm=512):
    m = x.shape[0]
    blocks = [((tm, D_MODEL), F32)] * 2
    return pl.pallas_call(
        _final_norm_body,
        grid=(m // tm,),
        in_specs=[pl.BlockSpec((tm, D_MODEL), lambda i: (i, 0)), pl.BlockSpec((1, D_MODEL), lambda i: (0, 0))],
        out_specs=pl.BlockSpec((tm, D_MODEL), lambda i: (i, 0)),
        out_shape=jax.ShapeDtypeStruct((m, D_MODEL), F32),
        compiler_params=_params(("parallel",), blocks, temps=[((tm, D_MODEL), F32)]),
        name="final_norm",
    )(x, g.reshape(1, D_MODEL))


def kernel(x_prompt, x_sample, cache_k, cache_v, state_conv, state_ffn_conv, norm1_g, w_in, conv_w, conv_b,
           conv_ln_g, conv_ln_b, w_conv_out, attn_sinks, w_attn_out, w_out, norm2_g, w_up, ffn_conv_w,
           ffn_conv_b, w_down, final_norm_g):
    depth = w_in.shape[0]
    n_p, t_p, _ = x_prompt.shape
    n_s, t_s, _ = x_sample.shape
    xp = x_prompt.reshape(n_p * t_p, D_MODEL)
    xs = x_sample.reshape(n_s * t_s, D_MODEL)
    w_co_b, w_ao_b, w_o_b = (w.astype(BF16) for w in (w_conv_out, w_attn_out, w_out))
    keys_on_lanes = lambda c: jnp.transpose(c, (0, 1, 3, 4, 2)).reshape(depth, n_s, D_KV, WINDOW)
    keys_on_rows = lambda c: jnp.transpose(c.reshape(depth, n_s, N_KV_HEADS, HEAD_DIM, WINDOW), (0, 1, 4, 2, 3))
    cache_k, cache_v = keys_on_lanes(cache_k), keys_on_lanes(cache_v)
    rows_first = lambda s: jnp.transpose(s, (0, 2, 1, 3))
    state_conv = rows_first(state_conv)
    outs = {k: [] for k in ("kp", "vp", "cp", "fp")}
    c_states, kv_states, f_states = (), (), ()
    for l in range(depth):
        conv_args = (conv_w, conv_b, conv_ln_g, conv_ln_b, l)
        proj_s, w_in16 = _inproj(xs, norm1_g, w_in, l, tm=n_s * t_s)
        proj = _inproj(xp, norm1_g, w_in16, l, tm=1024)
        c_act, c_state = _conv_prompt(proj, n_p, t_p, *conv_args)
        a_act = _attn_prompt(proj, attn_sinks, n_p, t_p, l)
        x1_p, h2_p = _mix(c_act, a_act, proj, xp, w_co_b, w_ao_b, w_o_b, norm2_g, l)
        kv = proj.reshape(n_p, t_p, IN_COLS)[:, t_p - WINDOW:, :]
        outs["kp"].append(kv[:, :, COL_K:COL_K + D_KV].astype(F32).reshape(n_p, WINDOW, N_KV_HEADS, HEAD_DIM))
        outs["vp"].append(kv[:, :, COL_V:COL_V + D_KV].astype(F32).reshape(n_p, WINDOW, N_KV_HEADS, HEAD_DIM))
        outs["cp"].append(c_state)
        proj = proj_s
        c_stack, c_act = _conv_sample(proj, state_conv, c_states, n_s, t_s, *conv_args)
        k_stack, v_stack, a_act = _attn_sample(proj, cache_k, cache_v, kv_states, attn_sinks, n_s, t_s, l)
        c_states, kv_states = (c_stack,), (k_stack, v_stack)
        x1, h2 = _mix(c_act, a_act, proj, xs, w_co_b, w_ao_b, w_o_b, norm2_g, l)
        f_stack, act, wg16, wv16 = _up_sample(h2, state_ffn_conv, f_states, n_s, t_s, w_up, ffn_conv_w,
                                              ffn_conv_b, l)
        f_states = (f_stack,)
        xs, w_down16 = _down(act, w_down, x1, l, tm=512)
        act, f_g, f_v = _up_prompt(h2_p, n_p, t_p, wg16, wv16, ffn_conv_w, ffn_conv_b, l)
        xp = _down(act, w_down16, x1_p, l, tm=1024)
        outs["fp"].append(jnp.concatenate([f_g, f_v], axis=-1))
    y_prompt = _final_norm(xp, final_norm_g).reshape(n_p, t_p, D_MODEL)
    y_sample = _final_norm(xs, final_norm_g).reshape(n_s, t_s, D_MODEL)
    return (y_prompt, y_sample, jnp.stack(outs["kp"]), jnp.stack(outs["vp"]), jnp.stack(outs["cp"]),
            jnp.stack(outs["fp"]), keys_on_rows(k_stack), keys_on_rows(v_stack), rows_first(c_stack),
            jnp.transpose(f_stack, (0, 1, 3, 2, 4)).reshape(depth, n_s, FFN_K - 1, 2 * D_FF))
```
